```python
import jax
import jax.numpy as jnp
from jax import lax
import numpy as np

D_MODEL = 1024
BATCH = 2
SEQ = 8192
DEPTH = 2

GRID_W = 64
CTX_LEN = 256
HEAD_DIM = 64
BRANCH_W = 256
N_BRANCH = 4
A_Q_HEADS = 4
A_KV_HEADS = 2
D_Q_HEADS = 4
D_KV_HEADS = 2
FNET_GROUPS = 4
SGU_GROUPS = 4
GROUP_W = 64
CHUNK = 128
Q_BLOCK = 128
WINDOW = 128
ROPE_BASE = 10000.0
EPS = 1e-6
KV_W = 128
IN_SIZES = (BRANCH_W, KV_W, KV_W, BRANCH_W,
            BRANCH_W, KV_W, KV_W, BRANCH_W,
            BRANCH_W, BRANCH_W,
            BRANCH_W, BRANCH_W, BRANCH_W)
IN_W = 9 * BRANCH_W + 4 * KV_W

kernel_name = "hybrid_prefix_dit_block"


def _rms(x, w):
    xf = x.astype(jnp.float32)
    y = xf * lax.rsqrt(jnp.mean(xf * xf, axis=-1, keepdims=True) + EPS)
    return (y * w.astype(jnp.float32)).astype(x.dtype)


def _split_in(p):
    idx, acc = [], 0
    for s in IN_SIZES[:-1]:
        acc += s
        idx.append(acc)
    return jnp.split(p, idx, axis=-1)


def _modulation(cvec, w_ada, b_ada):
    m = jax.nn.silu(cvec) @ w_ada + b_ada
    return jnp.split(m, 3, axis=-1)


def _axial_rope_tables(rows):
    t = jnp.arange(rows * GRID_W, dtype=jnp.int32)
    r = (t // GRID_W).astype(jnp.float32)
    col = (t % GRID_W).astype(jnp.float32)
    nf = HEAD_DIM // 4
    inv = ROPE_BASE ** (-jnp.arange(nf, dtype=jnp.float32) / nf)
    ar = r[:, None] * inv[None, :]
    ac = col[:, None] * inv[None, :]
    return (jnp.cos(ar), jnp.sin(ar), jnp.cos(ac), jnp.sin(ac))


def _rot_half(x, cos, sin):
    nf = cos.shape[-1]
    x1, x2 = x[..., :nf], x[..., nf:]
    cos = cos[None, :, None, :]
    sin = sin[None, :, None, :]
    return jnp.concatenate([x1 * cos - x2 * sin, x1 * sin + x2 * cos], axis=-1)


def _axial_rope(x, tabs):
    cr, sr, cc, sc = tabs
    xf = x.astype(jnp.float32)
    half = HEAD_DIM // 2
    y = jnp.concatenate([_rot_half(xf[..., :half], cr, sr),
                         _rot_half(xf[..., half:], cc, sc)], axis=-1)
    return y.astype(x.dtype)


def _heads(t, n):
    return t.reshape(t.shape[0], t.shape[1], n, HEAD_DIM)


def _flat(o):
    return o.reshape(o.shape[0], o.shape[1], -1)


def _q_groups(q, qn, n_q, n_kv, tabs):
    b, n = q.shape[:2]
    q = _rms(_heads(q, n_q), qn)
    if tabs is not None:
        q = _axial_rope(q, tabs)
    return (q * HEAD_DIM ** -0.5).reshape(b, n, n_kv, n_q // n_kv, HEAD_DIM)


def _k_heads(k, kn, n_kv, tabs):
    k = _rms(_heads(k, n_kv), kn)
    if tabs is not None:
        k = _axial_rope(k, tabs)
    return k


def _attend(q, k, v, sink):
    s = jnp.einsum('bqkgd,bnkd->bkgqn', q, k, preferred_element_type=jnp.float32)
    if sink is not None:
        s_sink = jnp.broadcast_to(sink.astype(jnp.float32)[None, :, :, None, None], s.shape[:-1] + (1,))
        p = jax.nn.softmax(jnp.concatenate([s, s_sink], axis=-1), axis=-1)[..., :-1]
    else:
        p = jax.nn.softmax(s, axis=-1)
    return jnp.einsum('bkgqn,bnkd->bqkgd', p.astype(v.dtype), v)


def _attend_blocked(q, k, v):
    b, s = q.shape[:2]
    nb = s // Q_BLOCK
    qb = jnp.moveaxis(q.reshape(b, nb, Q_BLOCK, *q.shape[2:]), 1, 0)
    ob = lax.map(lambda qq: _attend(qq, k, v, None), qb)
    return jnp.moveaxis(ob, 0, 1).reshape(b, s, -1)


def _attend_window_sink(q, k, v, kc, vc, sink):
    b, s = q.shape[:2]
    nb = s // Q_BLOCK
    nc = kc.shape[1]
    pad = ((0, 0), (Q_BLOCK, Q_BLOCK), (0, 0), (0, 0))

    def band(t):
        tb = jnp.pad(t, pad).reshape(b, nb + 2, Q_BLOCK, *t.shape[2:])
        bt = jnp.concatenate([tb[:, :-2], tb[:, 1:-1], tb[:, 2:]], axis=2)
        return jnp.moveaxis(bt, 1, 0)

    kb, vb = band(k), band(v)
    qb = jnp.moveaxis(q.reshape(b, nb, Q_BLOCK, *q.shape[2:]), 1, 0)
    a_idx = jnp.arange(Q_BLOCK)[:, None]
    j_idx = jnp.arange(3 * Q_BLOCK)[None, :]
    rel = j_idx - Q_BLOCK - a_idx
    sink_f = sink.astype(jnp.float32)

    def one(args):
        qq, kk, vv, i = args
        kpos = (i - 1) * Q_BLOCK + j_idx
        mask = (jnp.abs(rel) <= WINDOW) & (kpos >= 0) & (kpos < s)
        s_loc = jnp.einsum('bqkgd,bjkd->bkgqj', qq, kk, preferred_element_type=jnp.float32)
        s_loc = jnp.where(mask, s_loc, -jnp.inf)
        s_ctx = jnp.einsum('bqkgd,bnkd->bkgqn', qq, kc, preferred_element_type=jnp.float32)
        s_sink = jnp.broadcast_to(sink_f[None, :, :, None, None], s_loc.shape[:-1] + (1,))
        p = jax.nn.softmax(jnp.concatenate([s_ctx, s_loc, s_sink], axis=-1), axis=-1).astype(vv.dtype)
        return (jnp.einsum('bkgqn,bnkd->bqkgd', p[..., :nc], vc)
                + jnp.einsum('bkgqj,bjkd->bqkgd', p[..., nc:nc + 3 * Q_BLOCK], vv))

    ob = lax.map(one, (qb, kb, vb, jnp.arange(nb)))
    return jnp.moveaxis(ob, 0, 1).reshape(b, s, -1)


def _fourier_mix(f, w_fnet):
    b, n = f.shape[:2]
    fg = f.astype(jnp.float32).reshape(b, n, FNET_GROUPS, GROUP_W)
    y = jnp.fft.fft2(fg, axes=(1, 3), norm='ortho').real.astype(f.dtype)
    return jnp.einsum('bngc,gcd->bngd', y, w_fnet).reshape(b, n, BRANCH_W)


def _spatial_gate(u, v, w_sp, b_sp):
    b, n = u.shape[:2]
    nch = n // CHUNK
    u = jax.nn.gelu(u, approximate=False).reshape(b, nch, CHUNK, SGU_GROUPS, GROUP_W)
    v = jax.nn.gelu(v, approximate=False).reshape(b, nch, CHUNK, SGU_GROUPS, GROUP_W)
    sp = jnp.einsum('gpq,bmqgc->bmpgc', w_sp, v) + b_sp.T[None, None, :, :, None]
    return (u * sp).reshape(b, n, BRANCH_W)


def _merge(h, outs, gates, w_br, w_merge, b_merge, w_out):
    b, n = h.shape[:2]
    yb = jnp.stack([o * jax.nn.silu(z) for o, z in zip(outs, gates)], axis=2)
    y = jnp.einsum('bnrw,rwd->bnrd', yb, w_br)
    g = jax.nn.sigmoid(h @ w_merge + b_merge).reshape(b, n, N_BRANCH, D_MODEL)
    return jnp.sum(g * y, axis=2) @ w_out


def _layer(x, xc, c, c_ctx, tabs, norm_w, w_ada, b_ada, w_in, qn_a, kn_a, qn_d, kn_d,
           sink_d, w_fnet, w_sp, b_sp, w_br, w_merge, b_merge, w_out, need_ctx_out):
    sh, sc, gt = _modulation(c, w_ada, b_ada)
    sh_c, sc_c, gt_c = _modulation(c_ctx, w_ada, b_ada)
    h = _rms(x, norm_w) * (1.0 + sc[:, None, :]) + sh[:, None, :]
    hc = _rms(xc, norm_w) * (1.0 + sc_c) + sh_c
    aq, ak, av, az, dq, dk, dv, dz, bf, bz, cu, cv, cz = _split_in(h @ w_in)
    aqc, akc, avc, azc, dqc, dkc, dvc, dzc, bfc, bzc, cuc, cvc, czc = _split_in(hc @ w_in)
    sink = sink_d.reshape(D_KV_HEADS, D_Q_HEADS // D_KV_HEADS)

    ka_c = _k_heads(akc, kn_a, A_KV_HEADS, None)
    va_c = _heads(avc, A_KV_HEADS)
    kd_c = _k_heads(dkc, kn_d, D_KV_HEADS, None)
    vd_c = _heads(dvc, D_KV_HEADS)

    qa = _q_groups(aq, qn_a, A_Q_HEADS, A_KV_HEADS, tabs)
    ka = jnp.concatenate([ka_c, _k_heads(ak, kn_a, A_KV_HEADS, tabs)], axis=1)
    va = jnp.concatenate([va_c, _heads(av, A_KV_HEADS)], axis=1)
    o_a = _attend_blocked(qa, ka, va)
    qd = _q_groups(dq, qn_d, D_Q_HEADS, D_KV_HEADS, tabs)
    o_d = _attend_window_sink(qd, _k_heads(dk, kn_d, D_KV_HEADS, tabs), _heads(dv, D_KV_HEADS),
                              kd_c, vd_c, sink)
    o_b = _fourier_mix(bf, w_fnet)
    o_c = _spatial_gate(cu, cv, w_sp, b_sp)
    y = _merge(h, (o_a, o_d, o_b, o_c), (az, dz, bz, cz), w_br, w_merge, b_merge, w_out)
    x = x + gt[:, None, :] * y

    if need_ctx_out:
        qa_c = _q_groups(aqc, qn_a, A_Q_HEADS, A_KV_HEADS, None)
        o_ac = _flat(_attend(qa_c, ka_c, va_c, None))
        qd_c = _q_groups(dqc, qn_d, D_Q_HEADS, D_KV_HEADS, None)
        o_dc = _flat(_attend(qd_c, kd_c, vd_c, sink))
        o_bc = _fourier_mix(bfc, w_fnet)
        o_cc = _spatial_gate(cuc, cvc, w_sp, b_sp)
        yc = _merge(hc, (o_ac, o_dc, o_bc, o_cc), (azc, dzc, bzc, czc), w_br, w_merge, b_merge, w_out)
        xc = xc + gt_c * yc
    return x, xc


def setup_inputs(seed: int = 0) -> dict:
    key = jax.random.key(seed)
    ks = jax.random.split(key, 24)
    f32 = jnp.float32
    nrm = lambda k, shape, s: jax.random.normal(k, shape, f32) * s
    return {
        'x': nrm(ks[0], (BATCH, SEQ, D_MODEL), 1.0),
        'c': nrm(ks[1], (BATCH, D_MODEL), 1.0),
        'ctx': nrm(ks[2], (BATCH, CTX_LEN, D_MODEL), 1.0),
        'c_ctx': nrm(ks[3], (D_MODEL,), 1.0),
        'norm_w': 1.0 + nrm(ks[4], (DEPTH, D_MODEL), 0.02),
        'w_ada': nrm(ks[5], (DEPTH, D_MODEL, 3 * D_MODEL), 0.5 * D_MODEL ** -0.5),
        'b_ada': nrm(ks[6], (DEPTH, 3 * D_MODEL), 0.02),
        'w_in': nrm(ks[7], (DEPTH, D_MODEL, IN_W), D_MODEL ** -0.5),
        'qn_a': 1.0 + nrm(ks[8], (DEPTH, HEAD_DIM), 0.02),
        'kn_a': 1.0 + nrm(ks[9], (DEPTH, HEAD_DIM), 0.02),
        'qn_d': 1.0 + nrm(ks[10], (DEPTH, HEAD_DIM), 0.02),
        'kn_d': 1.0 + nrm(ks[11], (DEPTH, HEAD_DIM), 0.02),
        'sink_d': nrm(ks[12], (DEPTH, D_Q_HEADS), 0.5),
        'w_fnet': nrm(ks[13], (DEPTH, FNET_GROUPS, GROUP_W, GROUP_W), GROUP_W ** -0.5),
        'w_sp': nrm(ks[14], (DEPTH, SGU_GROUPS, CHUNK, CHUNK), CHUNK ** -0.5),
        'b_sp': 1.0 + nrm(ks[15], (DEPTH, SGU_GROUPS, CHUNK), 0.02),
        'w_br': nrm(ks[16], (DEPTH, N_BRANCH, BRANCH_W, D_MODEL), BRANCH_W ** -0.5),
        'w_merge': nrm(ks[17], (DEPTH, D_MODEL, N_BRANCH * D_MODEL), D_MODEL ** -0.5),
        'b_merge': nrm(ks[18], (DEPTH, N_BRANCH * D_MODEL), 0.02),
        'w_out': nrm(ks[19], (DEPTH, D_MODEL, D_MODEL), D_MODEL ** -0.5),
    }


def reference(x, c, ctx, c_ctx, norm_w, w_ada, b_ada, w_in, qn_a, kn_a, qn_d, kn_d, sink_d,
              w_fnet, w_sp, b_sp, w_br, w_merge, b_merge, w_out):
    rows = x.shape[1] // GRID_W
    tabs = _axial_rope_tables(rows)
    xc = ctx
    for l in range(DEPTH):
        x, xc = _layer(x, xc, c, c_ctx, tabs, norm_w[l], w_ada[l], b_ada[l], w_in[l],
                       qn_a[l], kn_a[l], qn_d[l], kn_d[l], sink_d[l], w_fnet[l], w_sp[l],
                       b_sp[l], w_br[l], w_merge[l], b_merge[l], w_out[l],
                       need_ctx_out=(l < DEPTH - 1))
    return x
```

```python
import functools

import numpy as np
import jax
import jax.numpy as jnp
from jax import lax
from jax.experimental import pallas as pl
from jax.experimental.pallas import tpu as pltpu

F32 = jnp.float32
BF16 = jnp.bfloat16

GRID_W = 64
HEAD_DIM = 64
BRANCH_W = 256
KV_W = 128
N_BRANCH = 4
GROUP_W = 64
CHUNK = 128
Q_BLOCK = 128
WINDOW = 128
ROPE_BASE = 10000.0
EPS = 1e-6
LANES = 128
TOKEN_TILE = 256
DFT_INNER = 64
VMEM_LIMIT = 56 * 1024 * 1024

_IN_SIZES = (BRANCH_W, KV_W, KV_W, BRANCH_W, BRANCH_W, KV_W, KV_W, BRANCH_W,
             BRANCH_W, BRANCH_W, BRANCH_W, BRANCH_W, BRANCH_W)
_IN_OFF = tuple(int(v) for v in np.cumsum((0,) + _IN_SIZES))
IN_W = _IN_OFF[-1]
(_AQ, _AK, _AV, _AZ, _DQ, _DK, _DV, _DZ, _BF, _BZ, _CU, _CV, _CZ) = _IN_OFF[:-1]


def _silu(z):
    return z * jax.nn.sigmoid(z)


def _gelu(x):
    return 0.5 * x * (1.0 + lax.erf(x * np.float32(np.sqrt(0.5))))


def _dot(a, b):
    return jnp.dot(a, b, preferred_element_type=F32)


def _dot_hi(a, b):
    return jnp.dot(a, b, preferred_element_type=F32, precision=lax.Precision.HIGHEST)


def _lane_lo(width=LANES):
    lane = lax.broadcasted_iota(jnp.int32, (1, width), 1)
    return (lane % LANES) < HEAD_DIM


def _mod_kernel(cv_ref, w_ref, b_ref, o_ref):
    s = _silu(cv_ref[...])
    o_ref[0] = _dot(s.astype(BF16), w_ref[0].astype(BF16)) + b_ref[0]


def _modulation(cvecs, w_ada, b_ada):
    depth, d, d3 = w_ada.shape
    nblk = d3 // d
    return pl.pallas_call(
        _mod_kernel,
        grid=(depth, nblk),
        in_specs=[pl.BlockSpec((8, d), lambda l, n: (0, 0)),
                  pl.BlockSpec((1, d, d), lambda l, n: (l, 0, n)),
                  pl.BlockSpec((1, 1, d), lambda l, n: (l, 0, n))],
        out_specs=pl.BlockSpec((1, 8, d), lambda l, n: (l, 0, n)),
        out_shape=jax.ShapeDtypeStruct((depth, 8, d3), F32),
        name="modulation",
    )(cvecs, w_ada, b_ada.reshape(depth, 1, d3))


def _modulated_norm(x, mod_ref, nw_ref, row, d):
    m = mod_ref[pl.ds(row, 1), :]
    sh, sc = m[:, :d], m[:, d:2 * d]
    ms = jnp.mean(x * x, axis=-1, keepdims=True)
    xn = x * lax.rsqrt(ms + EPS) * nw_ref[...]
    return xn * (1.0 + sc) + sh, m[:, 2 * d:]


def _head_norm_rope(xs, ones_ref, wn, cos, sin, scale):
    sq = xs * xs
    hi = sq.astype(BF16)
    lo = (sq - hi.astype(F32)).astype(BF16)
    ms = _dot(hi, ones_ref[...]) + _dot(lo, ones_ref[...])
    y = xs * lax.rsqrt(ms + EPS) * wn
    lane = lax.broadcasted_iota(jnp.int32, (1, LANES), 1)
    first = (lane % 32) < 16
    sw = jnp.where(first, pltpu.roll(y, LANES - 16, axis=1), pltpu.roll(y, 16, axis=1))
    y = y * cos + sw * sin
    return y * scale if scale != 1.0 else y


def _dup_heads(x):
    r = pltpu.roll(x, HEAD_DIM, axis=1)
    lo = _lane_lo()
    return jnp.concatenate([jnp.where(lo, x, r), jnp.where(lo, r, x)], axis=1)


def _proj_kernel(x_ref, mod_ref, nw_ref, win_ref, ones_ref, cos_ref, sin_ref,
                 qna_ref, kna_ref, qnd_ref, knd_ref, wsp_ref, bsp_ref,
                 qa_ref, kat_ref, va_ref, za_ref, qd_ref, kdt_ref, vd_ref, zd_ref,
                 fb_ref, zb_ref, yc_ref, *, n_batch):
    b, j = pl.program_id(0), pl.program_id(1)
    d = x_ref.shape[-1]
    row = jnp.where(j == 0, n_batch, b)
    h, _ = _modulated_norm(x_ref[0], mod_ref, nw_ref, row, d)
    p = _dot(h.astype(BF16), win_ref[...])
    cos, sin = cos_ref[...], sin_ref[...]
    q_scale = HEAD_DIM ** -0.5

    def attn_branch(q0, k0, v0, z0, qn_ref, kn_ref, q_ref, kt_ref, v_ref, z_ref):
        for s in range(BRANCH_W // LANES):
            q_ref[0, :, s * LANES:(s + 1) * LANES] = _head_norm_rope(
                p[:, q0 + s * LANES:q0 + (s + 1) * LANES], ones_ref, qn_ref[...], cos, sin,
                q_scale).astype(BF16)
        k = _head_norm_rope(p[:, k0:k0 + KV_W], ones_ref, kn_ref[...], cos, sin, 1.0)
        kt_ref[0] = _dup_heads(k).T.astype(BF16)
        v_ref[0] = _dup_heads(p[:, v0:v0 + KV_W]).astype(BF16)
        z_ref[0] = _silu(p[:, z0:z0 + BRANCH_W])

    attn_branch(_AQ, _AK, _AV, _AZ, qna_ref, kna_ref, qa_ref, kat_ref, va_ref, za_ref)
    attn_branch(_DQ, _DK, _DV, _DZ, qnd_ref, knd_ref, qd_ref, kdt_ref, vd_ref, zd_ref)

    fb_ref[0] = p[:, _BF:_BF + BRANCH_W]
    zb_ref[0] = _silu(p[:, _BZ:_BZ + BRANCH_W])

    u = _gelu(p[:, _CU:_CU + BRANCH_W])
    v = _gelu(p[:, _CV:_CV + BRANCH_W]).astype(BF16)
    zc = _silu(p[:, _CZ:_CZ + BRANCH_W])
    group = lax.broadcasted_iota(jnp.int32, (1, BRANCH_W), 1) // GROUP_W
    for c in range(x_ref.shape[1] // CHUNK):
        rows = slice(c * CHUNK, (c + 1) * CHUNK)
        sp = bsp_ref[...]
        for g in range(BRANCH_W // GROUP_W):
            sp = sp + jnp.where(group == g, _dot(wsp_ref[g], v[rows]), 0.0)
        yc_ref[0, rows, :] = (u[rows] * sp * zc[rows]).astype(BF16)


def _projection(x_all, mod, nw, w_in, ones_bd, cos_t, sin_t, qna, kna, qnd, knd, w_sp, b_sp_t):
    nb, t, d = x_all.shape
    tm = TOKEN_TILE
    nt = t // tm
    tok = lambda w, dt: jax.ShapeDtypeStruct((nb, t, w), dt)
    tok_spec = lambda w: pl.BlockSpec((1, tm, w), lambda b, j: (b, j, 0))
    kt_shape = jax.ShapeDtypeStruct((nb, 2 * KV_W, t), BF16)
    kt_spec = pl.BlockSpec((1, 2 * KV_W, tm), lambda b, j: (b, 0, j))
    const = lambda a: pl.BlockSpec(a.shape, lambda b, j: (0,) * a.ndim)
    return pl.pallas_call(
        functools.partial(_proj_kernel, n_batch=nb),
        grid=(nb, nt),
        in_specs=[tok_spec(d), const(mod), const(nw), const(w_in), const(ones_bd),
                  pl.BlockSpec((tm, LANES), lambda b, j: (j, 0)),
                  pl.BlockSpec((tm, LANES), lambda b, j: (j, 0)),
                  const(qna), const(kna), const(qnd), const(knd), const(w_sp), const(b_sp_t)],
        out_specs=[tok_spec(BRANCH_W), kt_spec, tok_spec(2 * KV_W), tok_spec(BRANCH_W),
                   tok_spec(BRANCH_W), kt_spec, tok_spec(2 * KV_W), tok_spec(BRANCH_W),
                   tok_spec(BRANCH_W), tok_spec(BRANCH_W), tok_spec(BRANCH_W)],
        out_shape=[tok(BRANCH_W, BF16), kt_shape, tok(2 * KV_W, BF16), tok(BRANCH_W, F32),
                   tok(BRANCH_W, BF16), kt_shape, tok(2 * KV_W, BF16), tok(BRANCH_W, F32),
                   tok(BRANCH_W, F32), tok(BRANCH_W, F32), tok(BRANCH_W, BF16)],
        compiler_params=pltpu.CompilerParams(vmem_limit_bytes=VMEM_LIMIT),
        name="projection",
    )(x_all, mod, nw, w_in, ones_bd, cos_t, sin_t, qna, kna, qnd, knd, w_sp, b_sp_t)


def _stack_heads(q2):
    lo = _lane_lo()
    zero = jnp.zeros_like(q2)
    return jnp.concatenate([jnp.where(lo, q2, zero), jnp.where(lo, zero, q2)], axis=0)


def _unstack_heads(o, tq):
    return jnp.where(_lane_lo(), o[:tq], o[tq:])


def _attn_a_kernel(q_ref, kt_ref, v_ref, z_ref, o_ref, *, tk, ctx_len):
    j = pl.program_id(1)
    tq = q_ref.shape[1]
    t = kt_ref.shape[2]
    n_chunks = jnp.where(j == 0, ctx_len // tk, t // tk)
    for g in range(2):
        cols = slice(g * LANES, (g + 1) * LANES)
        qs = _stack_heads(q_ref[0, :, cols])

        def body(c, carry):
            m, l, acc = carry
            start = pl.multiple_of(c * tk, tk)
            s = _dot(qs, kt_ref[0, cols, pl.ds(start, tk)])
            m_new = jnp.maximum(m, jnp.max(s, axis=1, keepdims=True))
            alpha = jnp.exp(m - m_new)
            pr = jnp.exp(s - m_new)
            l = alpha * l + jnp.sum(pr, axis=1, keepdims=True)
            acc = alpha * acc + _dot(pr.astype(BF16), v_ref[0, pl.ds(start, tk), cols])
            return m_new, l, acc

        init = (jnp.full((2 * tq, 1), -jnp.inf, F32), jnp.zeros((2 * tq, 1), F32),
                jnp.zeros((2 * tq, LANES), F32))
        _, l, acc = lax.fori_loop(0, n_chunks, body, init)
        o = _unstack_heads(acc / l, tq)
        o_ref[0, :, cols] = (o * z_ref[0, :, cols]).astype(BF16)


def _attention_a(q, kt, v, z, ctx_len):
    nb, t, _ = q.shape
    tq = TOKEN_TILE
    return pl.pallas_call(
        functools.partial(_attn_a_kernel, tk=TOKEN_TILE, ctx_len=ctx_len),
        grid=(nb, t // tq),
        in_specs=[pl.BlockSpec((1, tq, BRANCH_W), lambda b, j: (b, j, 0)),
                  pl.BlockSpec((1, 2 * KV_W, t), lambda b, j: (b, 0, 0)),
                  pl.BlockSpec((1, t, 2 * KV_W), lambda b, j: (b, 0, 0)),
                  pl.BlockSpec((1, tq, BRANCH_W), lambda b, j: (b, j, 0))],
        out_specs=pl.BlockSpec((1, tq, BRANCH_W), lambda b, j: (b, j, 0)),
        out_shape=jax.ShapeDtypeStruct((nb, t, BRANCH_W), BF16),
        compiler_params=pltpu.CompilerParams(vmem_limit_bytes=VMEM_LIMIT),
        name="attention_a",
    )(q, kt, v, z)


def _attn_d_kernel(sink_ref, q_ref, ktc_ref, kt0_ref, kt1_ref, kt2_ref,
                   vc_ref, v0_ref, v1_ref, v2_ref, z_ref, o_ref, *, ctx_blocks, seq_len):
    bi = pl.program_id(1)
    qb = Q_BLOCK
    i = bi - ctx_blocks
    r = lax.broadcasted_iota(jnp.int32, (2 * qb, 3 * qb), 0) % qb
    jj = lax.broadcasted_iota(jnp.int32, (2 * qb, 3 * qb), 1)
    rel = jj - qb - r
    kpos = (i - 1) * qb + jj
    mask = (jnp.abs(rel) <= WINDOW) & (kpos >= 0) & (kpos < seq_len) & (i >= 0)
    top = lax.broadcasted_iota(jnp.int32, (2 * qb, 1), 0) < qb
    for g in range(2):
        cols = slice(g * LANES, (g + 1) * LANES)
        qs = _stack_heads(q_ref[0, :, cols])
        kt_loc = jnp.concatenate([kt0_ref[0, cols, :], kt1_ref[0, cols, :], kt2_ref[0, cols, :]],
                                 axis=1)
        v_loc = jnp.concatenate([v0_ref[0, :, cols], v1_ref[0, :, cols], v2_ref[0, :, cols]],
                                axis=0)
        s_ctx = _dot(qs, ktc_ref[0, cols, :])
        s_loc = jnp.where(mask, _dot(qs, kt_loc), -jnp.inf)
        sink = jnp.where(top, sink_ref[2 * g], sink_ref[2 * g + 1])
        m = jnp.maximum(jnp.maximum(jnp.max(s_ctx, axis=1, keepdims=True),
                                    jnp.max(s_loc, axis=1, keepdims=True)), sink)
        p_ctx = jnp.exp(s_ctx - m)
        p_loc = jnp.exp(s_loc - m)
        l = (jnp.sum(p_ctx, axis=1, keepdims=True) + jnp.sum(p_loc, axis=1, keepdims=True)
             + jnp.exp(sink - m))
        acc = _dot(p_ctx.astype(BF16), vc_ref[0, :, cols]) + _dot(p_loc.astype(BF16), v_loc)
        o = _unstack_heads(acc / l, qb)
        o_ref[0, :, cols] = (o * z_ref[0, :, cols]).astype(BF16)


def _attention_d(sink, q, kt, v, z, ctx_len, seq_len):
    nb, t, _ = q.shape
    qb = Q_BLOCK
    nblk = t // qb
    cb = ctx_len // qb
    prev = lambda b, i: (b, 0, jnp.maximum(i - 1, 0))
    cur = lambda b, i: (b, 0, i)
    nxt = lambda b, i: (b, 0, jnp.minimum(i + 1, nblk - 1))
    vprev = lambda b, i: (b, jnp.maximum(i - 1, 0), 0)
    vcur = lambda b, i: (b, i, 0)
    vnxt = lambda b, i: (b, jnp.minimum(i + 1, nblk - 1), 0)
    kt_blk = lambda im: pl.BlockSpec((1, 2 * KV_W, qb), im)
    v_blk = lambda im: pl.BlockSpec((1, qb, 2 * KV_W), im)
    return pl.pallas_call(
        functools.partial(_attn_d_kernel, ctx_blocks=cb, seq_len=seq_len),
        grid=(nb, nblk),
        in_specs=[pl.BlockSpec(memory_space=pltpu.SMEM),
                  pl.BlockSpec((1, qb, BRANCH_W), vcur),
                  pl.BlockSpec((1, 2 * KV_W, ctx_len), lambda b, i: (b, 0, 0)),
                  kt_blk(prev), kt_blk(cur), kt_blk(nxt),
                  pl.BlockSpec((1, ctx_len, 2 * KV_W), lambda b, i: (b, 0, 0)),
                  v_blk(vprev), v_blk(vcur), v_blk(vnxt),
                  pl.BlockSpec((1, qb, BRANCH_W), vcur)],
        out_specs=pl.BlockSpec((1, qb, BRANCH_W), vcur),
        out_shape=jax.ShapeDtypeStruct((nb, t, BRANCH_W), BF16),
        compiler_params=pltpu.CompilerParams(vmem_limit_bytes=VMEM_LIMIT),
        name="attention_d",
    )(sink, q, kt, kt, kt, kt, v, v, v, v, z)


def _dft_tables(seq_len, ctx_len):
    n2c = DFT_INNER
    n1c = seq_len // n2c
    scale = 1.0 / np.sqrt(float(seq_len) * GROUP_W)
    k1 = np.arange(n1c)[:, None]
    n1 = np.arange(n1c)[None, :]
    m1 = np.zeros((n2c, 2 * n1c, n1c), np.float64)
    for n2 in range(n2c):
        ang = -2.0 * np.pi * (k1 * n1 / n1c + n2 * k1 / seq_len)
        m1[n2, :n1c] = np.cos(ang) * scale
        m1[n2, n1c:] = np.sin(ang) * scale
    ang3 = 2.0 * np.pi * np.outer(np.arange(n2c), np.arange(n2c)) / n2c
    c3, s3 = np.cos(ang3), np.sin(ang3)
    m3 = np.block([[c3, s3], [-s3, c3]])
    cscale = 1.0 / np.sqrt(float(ctx_len) * GROUP_W)
    angc = 2.0 * np.pi * np.outer(np.arange(ctx_len), np.arange(ctx_len)) / ctx_len
    mc = np.concatenate([np.cos(angc), -np.sin(angc)], axis=0) * cscale
    angg = 2.0 * np.pi * np.outer(np.arange(GROUP_W), np.arange(GROUP_W)) / GROUP_W
    eye = np.eye(BRANCH_W // GROUP_W)
    chan = np.concatenate([np.kron(eye, np.cos(angg)), np.kron(eye, np.sin(angg))], axis=0)
    as32 = lambda a: jnp.asarray(a.astype(np.float32))
    return as32(m1), as32(m3), as32(mc), as32(chan)


def _fnet_dft_kernel(f_ref, m1_ref, m3_ref, mc_ref, pr_ref, pi_ref, br_ref, bi_ref,
                     *, ctx_len, n1c):
    n2 = pl.program_id(2)
    n2c = DFT_INNER

    @pl.when(n2 == 0)
    def _():
        res = _dot_hi(mc_ref[...], f_ref[0, :ctx_len, :])
        pr_ref[0, :ctx_len, :] = res[:ctx_len]
        pi_ref[0, :ctx_len, :] = res[ctx_len:]

    xs = f_ref[0, pl.ds(ctx_len + n2, n1c, stride=n2c), :]
    res = _dot_hi(m1_ref[0], xs)
    br_ref[pl.ds(n2, n1c, stride=n2c), :] = res[:n1c]
    bi_ref[pl.ds(n2, n1c, stride=n2c), :] = res[n1c:]

    @pl.when(n2 == n2c - 1)
    def _():
        def body(k1, carry):
            start = pl.multiple_of(k1 * n2c, n2c)
            xin = jnp.concatenate([br_ref[pl.ds(start, n2c), :], bi_ref[pl.ds(start, n2c), :]],
                                  axis=0)
            res = _dot_hi(m3_ref[...], xin)
            pr_ref[0, pl.ds(ctx_len + k1, n2c, stride=n1c), :] = res[:n2c]
            pi_ref[0, pl.ds(ctx_len + k1, n2c, stride=n1c), :] = res[n2c:]
            return carry
        lax.fori_loop(0, n1c, body, 0)


def _fnet_dft(fb, m1, m3, mc, ctx_len):
    nb, t, w = fb.shape
    seq = t - ctx_len
    n1c = seq // DFT_INNER
    blk = pl.BlockSpec((1, t, LANES), lambda b, c, n: (b, 0, c))
    return pl.pallas_call(
        functools.partial(_fnet_dft_kernel, ctx_len=ctx_len, n1c=n1c),
        grid=(nb, w // LANES, DFT_INNER),
        in_specs=[blk,
                  pl.BlockSpec((1, 2 * n1c, n1c), lambda b, c, n: (n, 0, 0)),
                  pl.BlockSpec(m3.shape, lambda b, c, n: (0, 0)),
                  pl.BlockSpec(mc.shape, lambda b, c, n: (0, 0))],
        out_specs=[blk, blk],
        out_shape=[jax.ShapeDtypeStruct((nb, t, w), F32)] * 2,
        scratch_shapes=[pltpu.VMEM((seq, LANES), F32), pltpu.VMEM((seq, LANES), F32)],
        compiler_params=pltpu.CompilerParams(vmem_limit_bytes=VMEM_LIMIT),
        name="fnet_dft",
    )(fb, m1, m3, mc)


def _merge_kernel(x_ref, mod_ref, nw_ref, ya_ref, yd_ref, pr_ref, pi_ref, zb_ref, yc_ref,
                  chan_ref, wf_ref, wbr_ref, wm_ref, bm_ref, wo_ref, o_ref, *, n_batch, tile_off):
    b, j = pl.program_id(0), pl.program_id(1)
    d = x_ref.shape[-1]
    x = x_ref[0]
    row = jnp.where(j + tile_off == 0, n_batch, b)
    h, gate = _modulated_norm(x, mod_ref, nw_ref, row, d)
    pc = jnp.concatenate([pr_ref[0], pi_ref[0]], axis=1)
    yb = _dot(_dot_hi(pc, chan_ref[...]).astype(BF16), wf_ref[...]) * zb_ref[0]
    branches = (ya_ref[0], yd_ref[0], yb.astype(BF16), yc_ref[0])
    hb = h.astype(BF16)
    acc = jnp.zeros((x.shape[0], d), F32)
    for r in range(N_BRANCH):
        g = jax.nn.sigmoid(_dot(hb, wm_ref[:, r * d:(r + 1) * d]) + bm_ref[:, r * d:(r + 1) * d])
        acc = acc + g * _dot(branches[r], wbr_ref[r])
    o_ref[0] = x + gate * _dot(acc.astype(BF16), wo_ref[...])


def _merge(x_all, mod, nw, ya, yd, pr, pi, zb, yc, chan, wf_bd, w_br, w_merge, b_merge, w_out,
           skip_ctx):
    nb, t, d = x_all.shape
    tm = TOKEN_TILE
    off = 1 if skip_ctx else 0
    nt = t // tm - off
    tok_spec = lambda w: pl.BlockSpec((1, tm, w), lambda b, j: (b, j + off, 0))
    const = lambda a: pl.BlockSpec(a.shape, lambda b, j: (0,) * a.ndim)
    return pl.pallas_call(
        functools.partial(_merge_kernel, n_batch=nb, tile_off=off),
        grid=(nb, nt),
        in_specs=[tok_spec(d), const(mod), const(nw)] + [tok_spec(BRANCH_W)] * 6
                 + [const(chan), const(wf_bd), const(w_br), const(w_merge), const(b_merge),
                    const(w_out)],
        out_specs=pl.BlockSpec((1, tm, d), lambda b, j: (b, j, 0)),
        out_shape=jax.ShapeDtypeStruct((nb, nt * tm, d), F32),
        compiler_params=pltpu.CompilerParams(vmem_limit_bytes=VMEM_LIMIT),
        name="merge",
    )(x_all, mod, nw, ya, yd, pr, pi, zb, yc, chan, wf_bd, w_br, w_merge, b_merge, w_out)


def _rope_tables(seq_len, ctx_len):
    t = jnp.arange(seq_len, dtype=jnp.int32)
    r = (t // GRID_W).astype(F32)
    col = (t % GRID_W).astype(F32)
    nf = HEAD_DIM // 4
    inv = ROPE_BASE ** (-jnp.arange(nf, dtype=F32) / nf)
    ar = r[:, None] * inv[None, :]
    ac = col[:, None] * inv[None, :]
    cr, sr, cc, sc = jnp.cos(ar), jnp.sin(ar), jnp.cos(ac), jnp.sin(ac)
    cos_h = jnp.concatenate([cr, cr, cc, cc], axis=1)
    sin_h = jnp.concatenate([-sr, sr, -sc, sc], axis=1)
    cos_t = jnp.concatenate([jnp.ones((ctx_len, HEAD_DIM), F32), cos_h], axis=0)
    sin_t = jnp.concatenate([jnp.zeros((ctx_len, HEAD_DIM), F32), sin_h], axis=0)
    return jnp.tile(cos_t, (1, 2)), jnp.tile(sin_t, (1, 2))


def kernel(x, c, ctx, c_ctx, norm_w, w_ada, b_ada, w_in, qn_a, kn_a, qn_d, kn_d, sink_d,
           w_fnet, w_sp, b_sp, w_br, w_merge, b_merge, w_out):
    nb, seq, d = x.shape
    ctx_len = ctx.shape[1]
    depth = norm_w.shape[0]
    assert ctx_len == TOKEN_TILE and seq % TOKEN_TILE == 0 and nb < 8

    cvecs = jnp.zeros((8, d), F32).at[:nb].set(c).at[nb].set(c_ctx)
    mods = _modulation(cvecs, w_ada, b_ada)

    cos_t, sin_t = _rope_tables(seq, ctx_len)
    m1, m3, mc, chan = _dft_tables(seq, ctx_len)
    ones_bd = jnp.asarray(np.kron(np.eye(2), np.full((HEAD_DIM, HEAD_DIM), 1.0 / HEAD_DIM)), BF16)
    pair = lambda w: jnp.tile(w, 2).reshape(1, LANES)
    eye_g = jnp.eye(BRANCH_W // GROUP_W, dtype=F32)

    x_all = jnp.concatenate([ctx, x], axis=1)
    for l in range(depth):
        nw = norm_w[l].reshape(1, d)
        wf_bd = jnp.einsum('gh,gcd->gchd', eye_g, w_fnet[l]).reshape(BRANCH_W, BRANCH_W)
        b_sp_t = jnp.repeat(b_sp[l].T, GROUP_W, axis=1)
        (qa, kat, va, za, qd, kdt, vd, zd, fb, zb, yc) = _projection(
            x_all, mods[l], nw, w_in[l].astype(BF16), ones_bd, cos_t, sin_t,
            pair(qn_a[l]), pair(kn_a[l]), pair(qn_d[l]), pair(kn_d[l]),
            w_sp[l].astype(BF16), b_sp_t)
        ya = _attention_a(qa, kat, va, za, ctx_len)
        yd = _attention_d(sink_d[l], qd, kdt, vd, zd, ctx_len, seq)
        pr, pi = _fnet_dft(fb, m1, m3, mc, ctx_len)
        x_all = _merge(x_all, mods[l], nw, ya, yd, pr, pi, zb, yc, chan, wf_bd.astype(BF16),
                       w_br[l].astype(BF16), w_merge[l].astype(BF16),
                       b_merge[l].reshape(1, -1), w_out[l].astype(BF16),
                       skip_ctx=(l == depth - 1))
    return x_all
```

```python
import functools

import numpy as np
import jax
import jax.numpy as jnp
from jax import lax
from jax.experimental import pallas as pl
from jax.experimental.pallas import tpu as pltpu

F32 = jnp.float32
BF16 = jnp.bfloat16

GRID_W = 64
HEAD_DIM = 64
BRANCH_W = 256
KV_W = 128
N_BRANCH = 4
GROUP_W = 64
CHUNK = 128
Q_BLOCK = 128
WINDOW = 128
ROPE_BASE = 10000.0
EPS = 1e-6
LANES = 128
TOKEN_TILE = 256
V_ROWS = 80
ATTN_A_KEYS = 512
ATTN_A_UNROLL = 2
DFT_INNER = 64
VMEM_LIMIT = 56 * 1024 * 1024

_IN_SIZES = (BRANCH_W, KV_W, KV_W, BRANCH_W, BRANCH_W, KV_W, KV_W, BRANCH_W,
             BRANCH_W, BRANCH_W, BRANCH_W, BRANCH_W, BRANCH_W)
_IN_OFF = tuple(int(v) for v in np.cumsum((0,) + _IN_SIZES))
IN_W = _IN_OFF[-1]
(_AQ, _AK, _AV, _AZ, _DQ, _DK, _DV, _DZ, _BF, _BZ, _CU, _CV, _CZ) = _IN_OFF[:-1]


def _silu(z):
    return z * jax.nn.sigmoid(z)


def _gelu(x):
    return 0.5 * x * (1.0 + lax.erf(x * np.float32(np.sqrt(0.5))))


def _dot(a, b):
    return jnp.dot(a, b, preferred_element_type=F32)


def _dot_hi(a, b):
    return jnp.dot(a, b, preferred_element_type=F32, precision=lax.Precision.HIGHEST)


def _lane_lo(width=LANES):
    lane = lax.broadcasted_iota(jnp.int32, (1, width), 1)
    return (lane % LANES) < HEAD_DIM


def _mod_kernel(cv_ref, w_ref, b_ref, o_ref):
    s = _silu(cv_ref[...])
    o_ref[0] = _dot(s.astype(BF16), w_ref[0].astype(BF16)) + b_ref[0]


def _modulation(cvecs, w_ada, b_ada):
    depth, d, d3 = w_ada.shape
    nblk = d3 // d
    return pl.pallas_call(
        _mod_kernel,
        grid=(depth, nblk),
        in_specs=[pl.BlockSpec((8, d), lambda l, n: (0, 0)),
                  pl.BlockSpec((1, d, d), lambda l, n: (l, 0, n)),
                  pl.BlockSpec((1, 1, d), lambda l, n: (l, 0, n))],
        out_specs=pl.BlockSpec((1, 8, d), lambda l, n: (l, 0, n)),
        out_shape=jax.ShapeDtypeStruct((depth, 8, d3), F32),
        name="modulation",
    )(cvecs, w_ada, b_ada.reshape(depth, 1, d3))


def _modulated_norm(x, mod_ref, nw_ref, row, d):
    m = mod_ref[pl.ds(row, 1), :]
    sh, sc = m[:, :d], m[:, d:2 * d]
    ms = jnp.mean(x * x, axis=-1, keepdims=True)
    xn = x * lax.rsqrt(ms + EPS) * nw_ref[...]
    return xn * (1.0 + sc) + sh, m[:, 2 * d:]


def _head_norm_rope(xs, ones_ref, wn, cos, sin, scale):
    sq = xs * xs
    hi = sq.astype(BF16)
    lo = (sq - hi.astype(F32)).astype(BF16)
    ms = _dot(hi, ones_ref[...]) + _dot(lo, ones_ref[...])
    y = xs * lax.rsqrt(ms + EPS) * wn
    lane = lax.broadcasted_iota(jnp.int32, (1, LANES), 1)
    first = (lane % 32) < 16
    sw = jnp.where(first, pltpu.roll(y, LANES - 16, axis=1), pltpu.roll(y, 16, axis=1))
    y = y * cos + sw * sin
    return y * scale if scale != 1.0 else y


def _dup_heads(x):
    r = pltpu.roll(x, HEAD_DIM, axis=1)
    lo = _lane_lo()
    return jnp.concatenate([jnp.where(lo, x, r), jnp.where(lo, r, x)], axis=1)


def _proj_kernel(x_ref, mod_ref, nw_ref, win_ref, ones_ref, cos_ref, sin_ref,
                 qna_ref, kna_ref, qnd_ref, knd_ref, wsp_ref, bsp_ref,
                 qat_ref, ka_ref, vat_ref, za_ref, qd_ref, kdt_ref, vd_ref, zd_ref,
                 fb_ref, zb_ref, yc_ref, *, n_batch):
    b, j = pl.program_id(0), pl.program_id(1)
    d = x_ref.shape[-1]
    row = jnp.where(j == 0, n_batch, b)
    h, _ = _modulated_norm(x_ref[0], mod_ref, nw_ref, row, d)
    p = _dot(h.astype(BF16), win_ref[...])
    cos, sin = cos_ref[...], sin_ref[...]
    q_scale = HEAD_DIM ** -0.5

    def head_pairs(q0, qn_ref):
        return [_head_norm_rope(p[:, q0 + s * LANES:q0 + (s + 1) * LANES], ones_ref, qn_ref[...],
                                cos, sin, q_scale) for s in range(BRANCH_W // LANES)]

    qat_ref[0] = jnp.concatenate(head_pairs(_AQ, qna_ref), axis=1).T.astype(BF16)
    ka_ref[0] = _head_norm_rope(p[:, _AK:_AK + KV_W], ones_ref, kna_ref[...], cos, sin,
                                1.0).astype(BF16)
    vt = p[:, _AV:_AV + KV_W].T
    ones = jnp.ones((V_ROWS - HEAD_DIM, vt.shape[1]), F32)
    vat_ref[0] = jnp.concatenate([vt[:HEAD_DIM], ones, vt[HEAD_DIM:], ones], axis=0).astype(BF16)
    za_ref[0] = _silu(p[:, _AZ:_AZ + BRANCH_W])

    for s, y in enumerate(head_pairs(_DQ, qnd_ref)):
        qd_ref[0, :, s * LANES:(s + 1) * LANES] = y.astype(BF16)
    kd = _head_norm_rope(p[:, _DK:_DK + KV_W], ones_ref, knd_ref[...], cos, sin, 1.0)
    kdt_ref[0] = _dup_heads(kd).T.astype(BF16)
    vd_ref[0] = _dup_heads(p[:, _DV:_DV + KV_W]).astype(BF16)
    zd_ref[0] = _silu(p[:, _DZ:_DZ + BRANCH_W])

    fb_ref[0] = p[:, _BF:_BF + BRANCH_W]
    zb_ref[0] = _silu(p[:, _BZ:_BZ + BRANCH_W])

    u = _gelu(p[:, _CU:_CU + BRANCH_W])
    v = _gelu(p[:, _CV:_CV + BRANCH_W]).astype(BF16)
    zc = _silu(p[:, _CZ:_CZ + BRANCH_W])
    group = lax.broadcasted_iota(jnp.int32, (1, BRANCH_W), 1) // GROUP_W
    for c in range(x_ref.shape[1] // CHUNK):
        rows = slice(c * CHUNK, (c + 1) * CHUNK)
        sp = bsp_ref[...]
        for g in range(BRANCH_W // GROUP_W):
            sp = sp + jnp.where(group == g, _dot(wsp_ref[g], v[rows]), 0.0)
        yc_ref[0, rows, :] = (u[rows] * sp * zc[rows]).astype(BF16)


def _projection(x_all, mod, nw, w_in, ones_bd, cos_t, sin_t, qna, kna, qnd, knd, w_sp, b_sp_t):
    nb, t, d = x_all.shape
    tm = TOKEN_TILE
    nt = t // tm
    tok = lambda w, dt: jax.ShapeDtypeStruct((nb, t, w), dt)
    tok_spec = lambda w: pl.BlockSpec((1, tm, w), lambda b, j: (b, j, 0))
    tr = lambda w: jax.ShapeDtypeStruct((nb, w, t), BF16)
    tr_spec = lambda w: pl.BlockSpec((1, w, tm), lambda b, j: (b, 0, j))
    const = lambda a: pl.BlockSpec(a.shape, lambda b, j: (0,) * a.ndim)
    return pl.pallas_call(
        functools.partial(_proj_kernel, n_batch=nb),
        grid=(nb, nt),
        in_specs=[tok_spec(d), const(mod), const(nw), const(w_in), const(ones_bd),
                  pl.BlockSpec((tm, LANES), lambda b, j: (j, 0)),
                  pl.BlockSpec((tm, LANES), lambda b, j: (j, 0)),
                  const(qna), const(kna), const(qnd), const(knd), const(w_sp), const(b_sp_t)],
        out_specs=[tr_spec(BRANCH_W), tok_spec(KV_W), tr_spec(2 * V_ROWS), tok_spec(BRANCH_W),
                   tok_spec(BRANCH_W), tr_spec(2 * KV_W), tok_spec(2 * KV_W), tok_spec(BRANCH_W),
                   tok_spec(BRANCH_W), tok_spec(BRANCH_W), tok_spec(BRANCH_W)],
        out_shape=[tr(BRANCH_W), tok(KV_W, BF16), tr(2 * V_ROWS), tok(BRANCH_W, F32),
                   tok(BRANCH_W, BF16), tr(2 * KV_W), tok(2 * KV_W, BF16), tok(BRANCH_W, F32),
                   tok(BRANCH_W, F32), tok(BRANCH_W, F32), tok(BRANCH_W, BF16)],
        compiler_params=pltpu.CompilerParams(vmem_limit_bytes=VMEM_LIMIT),
        name="projection",
    )(x_all, mod, nw, w_in, ones_bd, cos_t, sin_t, qna, kna, qnd, knd, w_sp, b_sp_t)


def _stack_heads(q2):
    lo = _lane_lo()
    zero = jnp.zeros_like(q2)
    return jnp.concatenate([jnp.where(lo, q2, zero), jnp.where(lo, zero, q2)], axis=0)


def _unstack_heads(o, tq):
    return jnp.where(_lane_lo(), o[:tq], o[tq:])


def _attn_a_kernel(qt_ref, k_ref, vt_ref, z_ref, o_ref, *, tk, ctx_len):
    j = pl.program_id(1)
    tq = qt_ref.shape[2]
    t = k_ref.shape[1]
    n_steps = jnp.where(j == 0, 0, (t - ctx_len) // (tk * ATTN_A_UNROLL))
    zeros = jnp.zeros((HEAD_DIM, 2 * tq), BF16)
    groups = range(2)
    qt_g = []
    for g in groups:
        r0 = g * LANES
        qrow = jnp.concatenate([qt_ref[0, r0:r0 + HEAD_DIM, :],
                                qt_ref[0, r0 + HEAD_DIM:r0 + LANES, :]], axis=1)
        qt_g.append(jnp.concatenate([qrow, zeros] if g == 0 else [zeros, qrow], axis=0))

    def step(g, start, size, m, acc):
        s = _dot(k_ref[0, pl.ds(start, size), :], qt_g[g])
        smax = jnp.max(s, axis=0, keepdims=True)
        m_new = smax if m is None else jnp.maximum(m, smax)
        pv = _dot(vt_ref[0, g * V_ROWS:(g + 1) * V_ROWS, pl.ds(start, size)],
                  jnp.exp(s - m_new).astype(BF16))
        return m_new, (pv if m is None else jnp.exp(m - m_new) * acc + pv)

    carry = tuple(step(g, 0, ctx_len, None, None) for g in groups)

    def body(c, carry):
        for u in range(ATTN_A_UNROLL):
            start = pl.multiple_of(ctx_len + (c * ATTN_A_UNROLL + u) * tk, LANES)
            carry = tuple(step(g, start, tk, *carry[g]) for g in groups)
        return carry

    carry = lax.fori_loop(0, n_steps, body, carry)
    for g in groups:
        acc = carry[g][1]
        o = acc[:HEAD_DIM] / acc[HEAD_DIM:HEAD_DIM + 1]
        o = jnp.concatenate([o[:, :tq], o[:, tq:]], axis=0).T
        cols = slice(g * LANES, (g + 1) * LANES)
        o_ref[0, :, cols] = (o * z_ref[0, :, cols]).astype(BF16)


def _attention_a(qt, k, vt, z, ctx_len):
    nb, t, _ = k.shape
    tq = TOKEN_TILE
    return pl.pallas_call(
        functools.partial(_attn_a_kernel, tk=ATTN_A_KEYS, ctx_len=ctx_len),
        grid=(nb, t // tq),
        in_specs=[pl.BlockSpec((1, BRANCH_W, tq), lambda b, j: (b, 0, j)),
                  pl.BlockSpec((1, t, KV_W), lambda b, j: (b, 0, 0)),
                  pl.BlockSpec((1, 2 * V_ROWS, t), lambda b, j: (b, 0, 0)),
                  pl.BlockSpec((1, tq, BRANCH_W), lambda b, j: (b, j, 0))],
        out_specs=pl.BlockSpec((1, tq, BRANCH_W), lambda b, j: (b, j, 0)),
        out_shape=jax.ShapeDtypeStruct((nb, t, BRANCH_W), BF16),
        compiler_params=pltpu.CompilerParams(vmem_limit_bytes=VMEM_LIMIT),
        name="attention_a",
    )(qt, k, vt, z)


def _attn_d_kernel(sink_ref, q_ref, ktc_ref, kt0_ref, kt1_ref, kt2_ref,
                   vc_ref, v0_ref, v1_ref, v2_ref, z_ref, o_ref, *, ctx_blocks, seq_len):
    bi = pl.program_id(1)
    qb = Q_BLOCK
    i = bi - ctx_blocks
    r = lax.broadcasted_iota(jnp.int32, (2 * qb, 3 * qb), 0) % qb
    jj = lax.broadcasted_iota(jnp.int32, (2 * qb, 3 * qb), 1)
    rel = jj - qb - r
    kpos = (i - 1) * qb + jj
    mask = (jnp.abs(rel) <= WINDOW) & (kpos >= 0) & (kpos < seq_len) & (i >= 0)
    top = lax.broadcasted_iota(jnp.int32, (2 * qb, 1), 0) < qb
    for g in range(2):
        cols = slice(g * LANES, (g + 1) * LANES)
        qs = _stack_heads(q_ref[0, :, cols])
        kt_loc = jnp.concatenate([kt0_ref[0, cols, :], kt1_ref[0, cols, :], kt2_ref[0, cols, :]],
                                 axis=1)
        v_loc = jnp.concatenate([v0_ref[0, :, cols], v1_ref[0, :, cols], v2_ref[0, :, cols]],
                                axis=0)
        s_ctx = _dot(qs, ktc_ref[0, cols, :])
        s_loc = jnp.where(mask, _dot(qs, kt_loc), -jnp.inf)
        sink = jnp.where(top, sink_ref[2 * g], sink_ref[2 * g + 1])
        m = jnp.maximum(jnp.maximum(jnp.max(s_ctx, axis=1, keepdims=True),
                                    jnp.max(s_loc, axis=1, keepdims=True)), sink)
        p_ctx = jnp.exp(s_ctx - m)
        p_loc = jnp.exp(s_loc - m)
        l = (jnp.sum(p_ctx, axis=1, keepdims=True) + jnp.sum(p_loc, axis=1, keepdims=True)
             + jnp.exp(sink - m))
        acc = _dot(p_ctx.astype(BF16), vc_ref[0, :, cols]) + _dot(p_loc.astype(BF16), v_loc)
        o = _unstack_heads(acc / l, qb)
        o_ref[0, :, cols] = (o * z_ref[0, :, cols]).astype(BF16)


def _attention_d(sink, q, kt, v, z, ctx_len, seq_len):
    nb, t, _ = q.shape
    qb = Q_BLOCK
    nblk = t // qb
    cb = ctx_len // qb
    prev = lambda b, i: (b, 0, jnp.maximum(i - 1, 0))
    cur = lambda b, i: (b, 0, i)
    nxt = lambda b, i: (b, 0, jnp.minimum(i + 1, nblk - 1))
    vprev = lambda b, i: (b, jnp.maximum(i - 1, 0), 0)
    vcur = lambda b, i: (b, i, 0)
    vnxt = lambda b, i: (b, jnp.minimum(i + 1, nblk - 1), 0)
    kt_blk = lambda im: pl.BlockSpec((1, 2 * KV_W, qb), im)
    v_blk = lambda im: pl.BlockSpec((1, qb, 2 * KV_W), im)
    return pl.pallas_call(
        functools.partial(_attn_d_kernel, ctx_blocks=cb, seq_len=seq_len),
        grid=(nb, nblk),
        in_specs=[pl.BlockSpec(memory_space=pltpu.SMEM),
                  pl.BlockSpec((1, qb, BRANCH_W), vcur),
                  pl.BlockSpec((1, 2 * KV_W, ctx_len), lambda b, i: (b, 0, 0)),
                  kt_blk(prev), kt_blk(cur), kt_blk(nxt),
                  pl.BlockSpec((1, ctx_len, 2 * KV_W), lambda b, i: (b, 0, 0)),
                  v_blk(vprev), v_blk(vcur), v_blk(vnxt),
                  pl.BlockSpec((1, qb, BRANCH_W), vcur)],
        out_specs=pl.BlockSpec((1, qb, BRANCH_W), vcur),
        out_shape=jax.ShapeDtypeStruct((nb, t, BRANCH_W), BF16),
        compiler_params=pltpu.CompilerParams(vmem_limit_bytes=VMEM_LIMIT),
        name="attention_d",
    )(sink, q, kt, kt, kt, kt, v, v, v, v, z)


def _dft_tables(seq_len, ctx_len):
    n2c = DFT_INNER
    n1c = seq_len // n2c
    scale = 1.0 / np.sqrt(float(seq_len) * GROUP_W)
    k1 = np.arange(n1c)[:, None]
    n1 = np.arange(n1c)[None, :]
    m1 = np.zeros((n2c, 2 * n1c, n1c), np.float64)
    for n2 in range(n2c):
        ang = -2.0 * np.pi * (k1 * n1 / n1c + n2 * k1 / seq_len)
        m1[n2, :n1c] = np.cos(ang) * scale
        m1[n2, n1c:] = np.sin(ang) * scale
    ang3 = 2.0 * np.pi * np.outer(np.arange(n2c), np.arange(n2c)) / n2c
    c3, s3 = np.cos(ang3), np.sin(ang3)
    m3 = np.block([[c3, s3], [-s3, c3]])
    cscale = 1.0 / np.sqrt(float(ctx_len) * GROUP_W)
    angc = 2.0 * np.pi * np.outer(np.arange(ctx_len), np.arange(ctx_len)) / ctx_len
    mc = np.concatenate([np.cos(angc), -np.sin(angc)], axis=0) * cscale
    angg = 2.0 * np.pi * np.outer(np.arange(GROUP_W), np.arange(GROUP_W)) / GROUP_W
    eye = np.eye(BRANCH_W // GROUP_W)
    chan = np.concatenate([np.kron(eye, np.cos(angg)), np.kron(eye, np.sin(angg))], axis=0)
    as32 = lambda a: jnp.asarray(a.astype(np.float32))
    return as32(m1), as32(m3), as32(mc), as32(chan)


def _fnet_dft_kernel(f_ref, m1_ref, m3_ref, mc_ref, pr_ref, pi_ref, br_ref, bi_ref,
                     *, ctx_len, n1c):
    n2 = pl.program_id(2)
    n2c = DFT_INNER

    @pl.when(n2 == 0)
    def _():
        res = _dot_hi(mc_ref[...], f_ref[0, :ctx_len, :])
        pr_ref[0, :ctx_len, :] = res[:ctx_len]
        pi_ref[0, :ctx_len, :] = res[ctx_len:]

    xs = f_ref[0, pl.ds(ctx_len + n2, n1c, stride=n2c), :]
    res = _dot_hi(m1_ref[0], xs)
    br_ref[pl.ds(n2, n1c, stride=n2c), :] = res[:n1c]
    bi_ref[pl.ds(n2, n1c, stride=n2c), :] = res[n1c:]

    @pl.when(n2 == n2c - 1)
    def _():
        def body(k1, carry):
            start = pl.multiple_of(k1 * n2c, n2c)
            xin = jnp.concatenate([br_ref[pl.ds(start, n2c), :], bi_ref[pl.ds(start, n2c), :]],
                                  axis=0)
            res = _dot_hi(m3_ref[...], xin)
            pr_ref[0, pl.ds(ctx_len + k1, n2c, stride=n1c), :] = res[:n2c]
            pi_ref[0, pl.ds(ctx_len + k1, n2c, stride=n1c), :] = res[n2c:]
            return carry
        lax.fori_loop(0, n1c, body, 0)


def _fnet_dft(fb, m1, m3, mc, ctx_len):
    nb, t, w = fb.shape
    seq = t - ctx_len
    n1c = seq // DFT_INNER
    blk = pl.BlockSpec((1, t, LANES), lambda b, c, n: (b, 0, c))
    return pl.pallas_call(
        functools.partial(_fnet_dft_kernel, ctx_len=ctx_len, n1c=n1c),
        grid=(nb, w // LANES, DFT_INNER),
        in_specs=[blk,
                  pl.BlockSpec((1, 2 * n1c, n1c), lambda b, c, n: (n, 0, 0)),
                  pl.BlockSpec(m3.shape, lambda b, c, n: (0, 0)),
                  pl.BlockSpec(mc.shape, lambda b, c, n: (0, 0))],
        out_specs=[blk, blk],
        out_shape=[jax.ShapeDtypeStruct((nb, t, w), F32)] * 2,
        scratch_shapes=[pltpu.VMEM((seq, LANES), F32), pltpu.VMEM((seq, LANES), F32)],
        compiler_params=pltpu.CompilerParams(vmem_limit_bytes=VMEM_LIMIT),
        name="fnet_dft",
    )(fb, m1, m3, mc)


def _merge_kernel(x_ref, mod_ref, nw_ref, ya_ref, yd_ref, pr_ref, pi_ref, zb_ref, yc_ref,
                  chan_ref, wf_ref, wbr_ref, wm_ref, bm_ref, wo_ref, o_ref, *, n_batch, tile_off):
    b, j = pl.program_id(0), pl.program_id(1)
    d = x_ref.shape[-1]
    x = x_ref[0]
    row = jnp.where(j + tile_off == 0, n_batch, b)
    h, gate = _modulated_norm(x, mod_ref, nw_ref, row, d)
    pc = jnp.concatenate([pr_ref[0], pi_ref[0]], axis=1)
    yb = _dot(_dot_hi(pc, chan_ref[...]).astype(BF16), wf_ref[...]) * zb_ref[0]
    branches = (ya_ref[0], yd_ref[0], yb.astype(BF16), yc_ref[0])
    hb = h.astype(BF16)
    acc = jnp.zeros((x.shape[0], d), F32)
    for r in range(N_BRANCH):
        g = jax.nn.sigmoid(_dot(hb, wm_ref[:, r * d:(r + 1) * d]) + bm_ref[:, r * d:(r + 1) * d])
        acc = acc + g * _dot(branches[r], wbr_ref[r])
    o_ref[0] = x + gate * _dot(acc.astype(BF16), wo_ref[...])


def _merge(x_all, mod, nw, ya, yd, pr, pi, zb, yc, chan, wf_bd, w_br, w_merge, b_merge, w_out,
           skip_ctx):
    nb, t, d = x_all.shape
    tm = TOKEN_TILE
    off = 1 if skip_ctx else 0
    nt = t // tm - off
    tok_spec = lambda w: pl.BlockSpec((1, tm, w), lambda b, j: (b, j + off, 0))
    const = lambda a: pl.BlockSpec(a.shape, lambda b, j: (0,) * a.ndim)
    return pl.pallas_call(
        functools.partial(_merge_kernel, n_batch=nb, tile_off=off),
        grid=(nb, nt),
        in_specs=[tok_spec(d), const(mod), const(nw)] + [tok_spec(BRANCH_W)] * 6
                 + [const(chan), const(wf_bd), const(w_br), const(w_merge), const(b_merge),
                    const(w_out)],
        out_specs=pl.BlockSpec((1, tm, d), lambda b, j: (b, j, 0)),
        out_shape=jax.ShapeDtypeStruct((nb, nt * tm, d), F32),
        compiler_params=pltpu.CompilerParams(vmem_limit_bytes=VMEM_LIMIT),
        name="merge",
    )(x_all, mod, nw, ya, yd, pr, pi, zb, yc, chan, wf_bd, w_br, w_merge, b_merge, w_out)


def _rope_tables(seq_len, ctx_len):
    t = jnp.arange(seq_len, dtype=jnp.int32)
    r = (t // GRID_W).astype(F32)
    col = (t % GRID_W).astype(F32)
    nf = HEAD_DIM // 4
    inv = ROPE_BASE ** (-jnp.arange(nf, dtype=F32) / nf)
    ar = r[:, None] * inv[None, :]
    ac = col[:, None] * inv[None, :]
    cr, sr, cc, sc = jnp.cos(ar), jnp.sin(ar), jnp.cos(ac), jnp.sin(ac)
    cos_h = jnp.concatenate([cr, cr, cc, cc], axis=1)
    sin_h = jnp.concatenate([-sr, sr, -sc, sc], axis=1)
    cos_t = jnp.concatenate([jnp.ones((ctx_len, HEAD_DIM), F32), cos_h], axis=0)
    sin_t = jnp.concatenate([jnp.zeros((ctx_len, HEAD_DIM), F32), sin_h], axis=0)
    return jnp.tile(cos_t, (1, 2)), jnp.tile(sin_t, (1, 2))


def kernel(x, c, ctx, c_ctx, norm_w, w_ada, b_ada, w_in, qn_a, kn_a, qn_d, kn_d, sink_d,
           w_fnet, w_sp, b_sp, w_br, w_merge, b_merge, w_out):
    nb, seq, d = x.shape
    ctx_len = ctx.shape[1]
    depth = norm_w.shape[0]
    assert ctx_len == TOKEN_TILE and seq % (ATTN_A_KEYS * ATTN_A_UNROLL) == 0 and nb < 8

    cvecs = jnp.zeros((8, d), F32).at[:nb].set(c).at[nb].set(c_ctx)
    mods = _modulation(cvecs, w_ada, b_ada)

    cos_t, sin_t = _rope_tables(seq, ctx_len)
    m1, m3, mc, chan = _dft_tables(seq, ctx_len)
    ones_bd = jnp.asarray(np.kron(np.eye(2), np.full((HEAD_DIM, HEAD_DIM), 1.0 / HEAD_DIM)), BF16)
    pair = lambda w: jnp.tile(w, 2).reshape(1, LANES)
    eye_g = jnp.eye(BRANCH_W // GROUP_W, dtype=F32)

    x_all = jnp.concatenate([ctx, x], axis=1)
    for l in range(depth):
        nw = norm_w[l].reshape(1, d)
        wf_bd = jnp.einsum('gh,gcd->gchd', eye_g, w_fnet[l]).reshape(BRANCH_W, BRANCH_W)
        b_sp_t = jnp.repeat(b_sp[l].T, GROUP_W, axis=1)
        (qat, ka, vat, za, qd, kdt, vd, zd, fb, zb, yc) = _projection(
            x_all, mods[l], nw, w_in[l].astype(BF16), ones_bd, cos_t, sin_t,
            pair(qn_a[l]), pair(kn_a[l]), pair(qn_d[l]), pair(kn_d[l]),
            w_sp[l].astype(BF16), b_sp_t)
        ya = _attention_a(qat, ka, vat, za, ctx_len)
        yd = _attention_d(sink_d[l], qd, kdt, vd, zd, ctx_len, seq)
        pr, pi = _fnet_dft(fb, m1, m3, mc, ctx_len)
        x_all = _merge(x_all, mods[l], nw, ya, yd, pr, pi, zb, yc, chan, wf_bd.astype(BF16),
                       w_br[l].astype(BF16), w_merge[l].astype(BF16),
                       b_merge[l].reshape(1, -1), w_out[l].astype(BF16),
                       skip_ctx=(l == depth - 1))
    return x_all
```

```python
import functools

import numpy as np
import jax
import jax.numpy as jnp
from jax import lax
from jax.experimental import pallas as pl
from jax.experimental.pallas import tpu as pltpu

F32 = jnp.float32
BF16 = jnp.bfloat16

GRID_W = 64
HEAD_DIM = 64
BRANCH_W = 256
KV_W = 128
N_BRANCH = 4
GROUP_W = 64
CHUNK = 128
Q_BLOCK = 128
WINDOW = 128
ROPE_BASE = 10000.0
EPS = 1e-6
LANES = 128
TOKEN_TILE = 256
V_ROWS = 80
ATTN_A_KEYS = 512
DFT_INNER = 64
VMEM_LIMIT = 56 * 1024 * 1024

_IN_SIZES = (BRANCH_W, KV_W, KV_W, BRANCH_W, BRANCH_W, KV_W, KV_W, BRANCH_W,
             BRANCH_W, BRANCH_W, BRANCH_W, BRANCH_W, BRANCH_W)
_IN_OFF = tuple(int(v) for v in np.cumsum((0,) + _IN_SIZES))
IN_W = _IN_OFF[-1]
(_AQ, _AK, _AV, _AZ, _DQ, _DK, _DV, _DZ, _BF, _BZ, _CU, _CV, _CZ) = _IN_OFF[:-1]


def _silu(z):
    return z * jax.nn.sigmoid(z)


def _gelu(x):
    return 0.5 * x * (1.0 + lax.erf(x * np.float32(np.sqrt(0.5))))


def _dot(a, b):
    return jnp.dot(a, b, preferred_element_type=F32)


def _dot_hi(a, b):
    return jnp.dot(a, b, preferred_element_type=F32, precision=lax.Precision.HIGHEST)


def _lane_lo(width=LANES):
    lane = lax.broadcasted_iota(jnp.int32, (1, width), 1)
    return (lane % LANES) < HEAD_DIM


def _mod_kernel(cv_ref, w_ref, b_ref, o_ref):
    s = _silu(cv_ref[...])
    o_ref[0] = _dot(s.astype(BF16), w_ref[0].astype(BF16)) + b_ref[0]


def _modulation(cvecs, w_ada, b_ada):
    depth, d, d3 = w_ada.shape
    nblk = d3 // d
    return pl.pallas_call(
        _mod_kernel,
        grid=(depth, nblk),
        in_specs=[pl.BlockSpec((8, d), lambda l, n: (0, 0)),
                  pl.BlockSpec((1, d, d), lambda l, n: (l, 0, n)),
                  pl.BlockSpec((1, 1, d), lambda l, n: (l, 0, n))],
        out_specs=pl.BlockSpec((1, 8, d), lambda l, n: (l, 0, n)),
        out_shape=jax.ShapeDtypeStruct((depth, 8, d3), F32),
        name="modulation",
    )(cvecs, w_ada, b_ada.reshape(depth, 1, d3))


def _modulated_norm(x, mod_ref, nw_ref, row, d):
    m = mod_ref[pl.ds(row, 1), :]
    sh, sc = m[:, :d], m[:, d:2 * d]
    ms = jnp.mean(x * x, axis=-1, keepdims=True)
    xn = x * lax.rsqrt(ms + EPS) * nw_ref[...]
    return xn * (1.0 + sc) + sh, m[:, 2 * d:]


def _head_norm_rope(xs, ones_ref, wn, cos, sin, scale):
    sq = xs * xs
    hi = sq.astype(BF16)
    lo = (sq - hi.astype(F32)).astype(BF16)
    ms = _dot(hi, ones_ref[...]) + _dot(lo, ones_ref[...])
    y = xs * lax.rsqrt(ms + EPS) * wn
    lane = lax.broadcasted_iota(jnp.int32, (1, LANES), 1)
    first = (lane % 32) < 16
    sw = jnp.where(first, pltpu.roll(y, LANES - 16, axis=1), pltpu.roll(y, 16, axis=1))
    y = y * cos + sw * sin
    return y * scale if scale != 1.0 else y


def _dup_heads(x):
    r = pltpu.roll(x, HEAD_DIM, axis=1)
    lo = _lane_lo()
    return jnp.concatenate([jnp.where(lo, x, r), jnp.where(lo, r, x)], axis=1)


def _proj_kernel(x_ref, mod_ref, nw_ref, win_ref, ones_ref, cos_ref, sin_ref,
                 qna_ref, kna_ref, qnd_ref, knd_ref, wsp_ref, bsp_ref,
                 qat_ref, ka_ref, vat_ref, za_ref, qd_ref, kdt_ref, vd_ref, zd_ref,
                 fb_ref, zb_ref, yc_ref, *, n_batch):
    b, j = pl.program_id(0), pl.program_id(1)
    d = x_ref.shape[-1]
    row = jnp.where(j == 0, n_batch, b)
    h, _ = _modulated_norm(x_ref[0], mod_ref, nw_ref, row, d)
    p = _dot(h.astype(BF16), win_ref[...])
    cos, sin = cos_ref[...], sin_ref[...]
    q_scale = HEAD_DIM ** -0.5

    def head_pairs(q0, qn_ref):
        return [_head_norm_rope(p[:, q0 + s * LANES:q0 + (s + 1) * LANES], ones_ref, qn_ref[...],
                                cos, sin, q_scale) for s in range(BRANCH_W // LANES)]

    qat_ref[0] = jnp.concatenate(head_pairs(_AQ, qna_ref), axis=1).T.astype(BF16)
    ka_ref[0] = _head_norm_rope(p[:, _AK:_AK + KV_W], ones_ref, kna_ref[...], cos, sin,
                                1.0).astype(BF16)
    vt = p[:, _AV:_AV + KV_W].T
    ones = jnp.ones((V_ROWS - HEAD_DIM, vt.shape[1]), F32)
    vat_ref[0] = jnp.concatenate([vt[:HEAD_DIM], ones, vt[HEAD_DIM:], ones], axis=0).astype(BF16)
    za_ref[0] = _silu(p[:, _AZ:_AZ + BRANCH_W])

    for s, y in enumerate(head_pairs(_DQ, qnd_ref)):
        qd_ref[0, :, s * LANES:(s + 1) * LANES] = y.astype(BF16)
    kd = _head_norm_rope(p[:, _DK:_DK + KV_W], ones_ref, knd_ref[...], cos, sin, 1.0)
    kdt_ref[0] = _dup_heads(kd).T.astype(BF16)
    vd_ref[0] = _dup_heads(p[:, _DV:_DV + KV_W]).astype(BF16)
    zd_ref[0] = _silu(p[:, _DZ:_DZ + BRANCH_W])

    fb_ref[0] = p[:, _BF:_BF + BRANCH_W]
    zb_ref[0] = _silu(p[:, _BZ:_BZ + BRANCH_W])

    u = _gelu(p[:, _CU:_CU + BRANCH_W])
    v = _gelu(p[:, _CV:_CV + BRANCH_W]).astype(BF16)
    zc = _silu(p[:, _CZ:_CZ + BRANCH_W])
    group = lax.broadcasted_iota(jnp.int32, (1, BRANCH_W), 1) // GROUP_W
    for c in range(x_ref.shape[1] // CHUNK):
        rows = slice(c * CHUNK, (c + 1) * CHUNK)
        sp = bsp_ref[...]
        for g in range(BRANCH_W // GROUP_W):
            sp = sp + jnp.where(group == g, _dot(wsp_ref[g], v[rows]), 0.0)
        yc_ref[0, rows, :] = (u[rows] * sp * zc[rows]).astype(BF16)


def _projection(x_all, mod, nw, w_in, ones_bd, cos_t, sin_t, qna, kna, qnd, knd, w_sp, b_sp_t):
    nb, t, d = x_all.shape
    tm = TOKEN_TILE
    nt = t // tm
    tok = lambda w, dt: jax.ShapeDtypeStruct((nb, t, w), dt)
    tok_spec = lambda w: pl.BlockSpec((1, tm, w), lambda b, j: (b, j, 0))
    tr = lambda w: jax.ShapeDtypeStruct((nb, w, t), BF16)
    tr_spec = lambda w: pl.BlockSpec((1, w, tm), lambda b, j: (b, 0, j))
    const = lambda a: pl.BlockSpec(a.shape, lambda b, j: (0,) * a.ndim)
    return pl.pallas_call(
        functools.partial(_proj_kernel, n_batch=nb),
        grid=(nb, nt),
        in_specs=[tok_spec(d), const(mod), const(nw), const(w_in), const(ones_bd),
                  pl.BlockSpec((tm, LANES), lambda b, j: (j, 0)),
                  pl.BlockSpec((tm, LANES), lambda b, j: (j, 0)),
                  const(qna), const(kna), const(qnd), const(knd), const(w_sp), const(b_sp_t)],
        out_specs=[tr_spec(BRANCH_W), tok_spec(KV_W), tr_spec(2 * V_ROWS), tok_spec(BRANCH_W),
                   tok_spec(BRANCH_W), tr_spec(2 * KV_W), tok_spec(2 * KV_W), tok_spec(BRANCH_W),
                   tok_spec(BRANCH_W), tok_spec(BRANCH_W), tok_spec(BRANCH_W)],
        out_shape=[tr(BRANCH_W), tok(KV_W, BF16), tr(2 * V_ROWS), tok(BRANCH_W, F32),
                   tok(BRANCH_W, BF16), tr(2 * KV_W), tok(2 * KV_W, BF16), tok(BRANCH_W, F32),
                   tok(BRANCH_W, F32), tok(BRANCH_W, F32), tok(BRANCH_W, BF16)],
        compiler_params=pltpu.CompilerParams(vmem_limit_bytes=VMEM_LIMIT),
        name="projection",
    )(x_all, mod, nw, w_in, ones_bd, cos_t, sin_t, qna, kna, qnd, knd, w_sp, b_sp_t)


def _stack_heads(q2):
    lo = _lane_lo()
    zero = jnp.zeros_like(q2)
    return jnp.concatenate([jnp.where(lo, q2, zero), jnp.where(lo, zero, q2)], axis=0)


def _unstack_heads(o, tq):
    return jnp.where(_lane_lo(), o[:tq], o[tq:])


def _attn_a_kernel(qt_ref, k_ref, vt_ref, z_ref, o_ref, s_ref, smax_ref, m_ref, acc_ref,
                   *, tk, ctx_len):
    j = pl.program_id(1)
    tq = qt_ref.shape[2]
    t = k_ref.shape[1]
    n_lat = (t - ctx_len) // tk
    zeros = jnp.zeros((HEAD_DIM, 2 * tq), BF16)
    groups = range(2)
    qt_g = []
    for g in groups:
        r0 = g * LANES
        qrow = jnp.concatenate([qt_ref[0, r0:r0 + HEAD_DIM, :],
                                qt_ref[0, r0 + HEAD_DIM:r0 + LANES, :]], axis=1)
        qt_g.append(jnp.concatenate([qrow, zeros] if g == 0 else [zeros, qrow], axis=0))

    def scores(g, start, size):
        return _dot(k_ref[0, pl.ds(start, size), :], qt_g[g])

    def values(g, start, size):
        return vt_ref[0, g * V_ROWS:(g + 1) * V_ROWS, pl.ds(start, size)]

    def lat_start(c):
        return pl.multiple_of(ctx_len + c * tk, LANES)

    for g in groups:
        s = scores(g, 0, ctx_len)
        m = jnp.max(s, axis=0, keepdims=True)
        m_ref[g] = m
        acc_ref[g] = _dot(values(g, 0, ctx_len), jnp.exp(s - m).astype(BF16))

    def produce(g, slot, c):
        s = scores(g, lat_start(c), tk)
        smax_ref[slot, g] = jnp.max(s, axis=0, keepdims=True)
        s_ref[slot, g] = s

    def consume(g, slot, c):
        m = m_ref[g]
        m_new = jnp.maximum(m, smax_ref[slot, g])
        pr = jnp.exp(s_ref[slot, g] - m_new).astype(BF16)
        acc_ref[g] = jnp.exp(m - m_new) * acc_ref[g] + _dot(values(g, lat_start(c), tk), pr)
        m_ref[g] = m_new

    @pl.when(j > 0)
    def _():
        for g in groups:
            produce(g, 0, 0)

        def body(i, carry):
            c = 2 * i
            for g in groups:
                produce(g, 1, c + 1)
                consume(g, 0, c)
            for g in groups:
                produce(g, 0, c + 2)
                consume(g, 1, c + 1)
            return carry

        lax.fori_loop(0, n_lat // 2 - 1, body, 0)
        for g in groups:
            produce(g, 1, n_lat - 1)
            consume(g, 0, n_lat - 2)
        for g in groups:
            consume(g, 1, n_lat - 1)

    for g in groups:
        acc = acc_ref[g]
        o = acc[:HEAD_DIM] / acc[HEAD_DIM:HEAD_DIM + 1]
        o = jnp.concatenate([o[:, :tq], o[:, tq:]], axis=0).T
        cols = slice(g * LANES, (g + 1) * LANES)
        o_ref[0, :, cols] = (o * z_ref[0, :, cols]).astype(BF16)


def _attention_a(qt, k, vt, z, ctx_len):
    nb, t, _ = k.shape
    tq = TOKEN_TILE
    tk = ATTN_A_KEYS
    return pl.pallas_call(
        functools.partial(_attn_a_kernel, tk=tk, ctx_len=ctx_len),
        grid=(nb, t // tq),
        in_specs=[pl.BlockSpec((1, BRANCH_W, tq), lambda b, j: (b, 0, j)),
                  pl.BlockSpec((1, t, KV_W), lambda b, j: (b, 0, 0)),
                  pl.BlockSpec((1, 2 * V_ROWS, t), lambda b, j: (b, 0, 0)),
                  pl.BlockSpec((1, tq, BRANCH_W), lambda b, j: (b, j, 0))],
        out_specs=pl.BlockSpec((1, tq, BRANCH_W), lambda b, j: (b, j, 0)),
        out_shape=jax.ShapeDtypeStruct((nb, t, BRANCH_W), BF16),
        scratch_shapes=[pltpu.VMEM((2, 2, tk, 2 * tq), F32),
                        pltpu.VMEM((2, 2, 1, 2 * tq), F32),
                        pltpu.VMEM((2, 1, 2 * tq), F32),
                        pltpu.VMEM((2, V_ROWS, 2 * tq), F32)],
        compiler_params=pltpu.CompilerParams(vmem_limit_bytes=VMEM_LIMIT),
        name="attention_a",
    )(qt, k, vt, z)


def _attn_d_kernel(sink_ref, q_ref, ktc_ref, kt0_ref, kt1_ref, kt2_ref,
                   vc_ref, v0_ref, v1_ref, v2_ref, z_ref, o_ref, *, ctx_blocks, seq_len):
    bi = pl.program_id(1)
    qb = Q_BLOCK
    i = bi - ctx_blocks
    r = lax.broadcasted_iota(jnp.int32, (2 * qb, 3 * qb), 0) % qb
    jj = lax.broadcasted_iota(jnp.int32, (2 * qb, 3 * qb), 1)
    rel = jj - qb - r
    kpos = (i - 1) * qb + jj
    mask = (jnp.abs(rel) <= WINDOW) & (kpos >= 0) & (kpos < seq_len) & (i >= 0)
    top = lax.broadcasted_iota(jnp.int32, (2 * qb, 1), 0) < qb
    for g in range(2):
        cols = slice(g * LANES, (g + 1) * LANES)
        qs = _stack_heads(q_ref[0, :, cols])
        kt_loc = jnp.concatenate([kt0_ref[0, cols, :], kt1_ref[0, cols, :], kt2_ref[0, cols, :]],
                                 axis=1)
        v_loc = jnp.concatenate([v0_ref[0, :, cols], v1_ref[0, :, cols], v2_ref[0, :, cols]],
                                axis=0)
        s_ctx = _dot(qs, ktc_ref[0, cols, :])
        s_loc = jnp.where(mask, _dot(qs, kt_loc), -jnp.inf)
        sink = jnp.where(top, sink_ref[2 * g], sink_ref[2 * g + 1])
        m = jnp.maximum(jnp.maximum(jnp.max(s_ctx, axis=1, keepdims=True),
                                    jnp.max(s_loc, axis=1, keepdims=True)), sink)
        p_ctx = jnp.exp(s_ctx - m)
        p_loc = jnp.exp(s_loc - m)
        l = (jnp.sum(p_ctx, axis=1, keepdims=True) + jnp.sum(p_loc, axis=1, keepdims=True)
             + jnp.exp(sink - m))
        acc = _dot(p_ctx.astype(BF16), vc_ref[0, :, cols]) + _dot(p_loc.astype(BF16), v_loc)
        o = _unstack_heads(acc / l, qb)
        o_ref[0, :, cols] = (o * z_ref[0, :, cols]).astype(BF16)


def _attention_d(sink, q, kt, v, z, ctx_len, seq_len):
    nb, t, _ = q.shape
    qb = Q_BLOCK
    nblk = t // qb
    cb = ctx_len // qb
    prev = lambda b, i: (b, 0, jnp.maximum(i - 1, 0))
    cur = lambda b, i: (b, 0, i)
    nxt = lambda b, i: (b, 0, jnp.minimum(i + 1, nblk - 1))
    vprev = lambda b, i: (b, jnp.maximum(i - 1, 0), 0)
    vcur = lambda b, i: (b, i, 0)
    vnxt = lambda b, i: (b, jnp.minimum(i + 1, nblk - 1), 0)
    kt_blk = lambda im: pl.BlockSpec((1, 2 * KV_W, qb), im)
    v_blk = lambda im: pl.BlockSpec((1, qb, 2 * KV_W), im)
    return pl.pallas_call(
        functools.partial(_attn_d_kernel, ctx_blocks=cb, seq_len=seq_len),
        grid=(nb, nblk),
        in_specs=[pl.BlockSpec(memory_space=pltpu.SMEM),
                  pl.BlockSpec((1, qb, BRANCH_W), vcur),
                  pl.BlockSpec((1, 2 * KV_W, ctx_len), lambda b, i: (b, 0, 0)),
                  kt_blk(prev), kt_blk(cur), kt_blk(nxt),
                  pl.BlockSpec((1, ctx_len, 2 * KV_W), lambda b, i: (b, 0, 0)),
                  v_blk(vprev), v_blk(vcur), v_blk(vnxt),
                  pl.BlockSpec((1, qb, BRANCH_W), vcur)],
        out_specs=pl.BlockSpec((1, qb, BRANCH_W), vcur),
        out_shape=jax.ShapeDtypeStruct((nb, t, BRANCH_W), BF16),
        compiler_params=pltpu.CompilerParams(vmem_limit_bytes=VMEM_LIMIT),
        name="attention_d",
    )(sink, q, kt, kt, kt, kt, v, v, v, v, z)


def _dft_tables(seq_len, ctx_len):
    n2c = DFT_INNER
    n1c = seq_len // n2c
    scale = 1.0 / np.sqrt(float(seq_len) * GROUP_W)
    k1 = np.arange(n1c)[:, None]
    n1 = np.arange(n1c)[None, :]
    m1 = np.zeros((n2c, 2 * n1c, n1c), np.float64)
    for n2 in range(n2c):
        ang = -2.0 * np.pi * (k1 * n1 / n1c + n2 * k1 / seq_len)
        m1[n2, :n1c] = np.cos(ang) * scale
        m1[n2, n1c:] = np.sin(ang) * scale
    ang3 = 2.0 * np.pi * np.outer(np.arange(n2c), np.arange(n2c)) / n2c
    c3, s3 = np.cos(ang3), np.sin(ang3)
    m3 = np.block([[c3, s3], [-s3, c3]])
    cscale = 1.0 / np.sqrt(float(ctx_len) * GROUP_W)
    angc = 2.0 * np.pi * np.outer(np.arange(ctx_len), np.arange(ctx_len)) / ctx_len
    mc = np.concatenate([np.cos(angc), -np.sin(angc)], axis=0) * cscale
    angg = 2.0 * np.pi * np.outer(np.arange(GROUP_W), np.arange(GROUP_W)) / GROUP_W
    eye = np.eye(BRANCH_W // GROUP_W)
    chan = np.concatenate([np.kron(eye, np.cos(angg)), np.kron(eye, np.sin(angg))], axis=0)
    as32 = lambda a: jnp.asarray(a.astype(np.float32))
    return as32(m1), as32(m3), as32(mc), as32(chan)


def _fnet_dft_kernel(f_ref, m1_ref, m3_ref, mc_ref, pr_ref, pi_ref, br_ref, bi_ref,
                     *, ctx_len, n1c):
    n2 = pl.program_id(2)
    n2c = DFT_INNER

    @pl.when(n2 == 0)
    def _():
        res = _dot_hi(mc_ref[...], f_ref[0, :ctx_len, :])
        pr_ref[0, :ctx_len, :] = res[:ctx_len]
        pi_ref[0, :ctx_len, :] = res[ctx_len:]

    xs = f_ref[0, pl.ds(ctx_len + n2, n1c, stride=n2c), :]
    res = _dot_hi(m1_ref[0], xs)
    br_ref[pl.ds(n2, n1c, stride=n2c), :] = res[:n1c]
    bi_ref[pl.ds(n2, n1c, stride=n2c), :] = res[n1c:]

    @pl.when(n2 == n2c - 1)
    def _():
        def body(k1, carry):
            start = pl.multiple_of(k1 * n2c, n2c)
            xin = jnp.concatenate([br_ref[pl.ds(start, n2c), :], bi_ref[pl.ds(start, n2c), :]],
                                  axis=0)
            res = _dot_hi(m3_ref[...], xin)
            pr_ref[0, pl.ds(ctx_len + k1, n2c, stride=n1c), :] = res[:n2c]
            pi_ref[0, pl.ds(ctx_len + k1, n2c, stride=n1c), :] = res[n2c:]
            return carry
        lax.fori_loop(0, n1c, body, 0)


def _fnet_dft(fb, m1, m3, mc, ctx_len):
    nb, t, w = fb.shape
    seq = t - ctx_len
    n1c = seq // DFT_INNER
    blk = pl.BlockSpec((1, t, LANES), lambda b, c, n: (b, 0, c))
    return pl.pallas_call(
        functools.partial(_fnet_dft_kernel, ctx_len=ctx_len, n1c=n1c),
        grid=(nb, w // LANES, DFT_INNER),
        in_specs=[blk,
                  pl.BlockSpec((1, 2 * n1c, n1c), lambda b, c, n: (n, 0, 0)),
                  pl.BlockSpec(m3.shape, lambda b, c, n: (0, 0)),
                  pl.BlockSpec(mc.shape, lambda b, c, n: (0, 0))],
        out_specs=[blk, blk],
        out_shape=[jax.ShapeDtypeStruct((nb, t, w), F32)] * 2,
        scratch_shapes=[pltpu.VMEM((seq, LANES), F32), pltpu.VMEM((seq, LANES), F32)],
        compiler_params=pltpu.CompilerParams(vmem_limit_bytes=VMEM_LIMIT),
        name="fnet_dft",
    )(fb, m1, m3, mc)


def _merge_kernel(x_ref, mod_ref, nw_ref, ya_ref, yd_ref, pr_ref, pi_ref, zb_ref, yc_ref,
                  chan_ref, wf_ref, wbr_ref, wm_ref, bm_ref, wo_ref, o_ref, *, n_batch, tile_off):
    b, j = pl.program_id(0), pl.program_id(1)
    d = x_ref.shape[-1]
    x = x_ref[0]
    row = jnp.where(j + tile_off == 0, n_batch, b)
    h, gate = _modulated_norm(x, mod_ref, nw_ref, row, d)
    pc = jnp.concatenate([pr_ref[0], pi_ref[0]], axis=1)
    yb = _dot(_dot_hi(pc, chan_ref[...]).astype(BF16), wf_ref[...]) * zb_ref[0]
    branches = (ya_ref[0], yd_ref[0], yb.astype(BF16), yc_ref[0])
    hb = h.astype(BF16)
    acc = jnp.zeros((x.shape[0], d), F32)
    for r in range(N_BRANCH):
        g = jax.nn.sigmoid(_dot(hb, wm_ref[:, r * d:(r + 1) * d]) + bm_ref[:, r * d:(r + 1) * d])
        acc = acc + g * _dot(branches[r], wbr_ref[r])
    o_ref[0] = x + gate * _dot(acc.astype(BF16), wo_ref[...])


def _merge(x_all, mod, nw, ya, yd, pr, pi, zb, yc, chan, wf_bd, w_br, w_merge, b_merge, w_out,
           skip_ctx):
    nb, t, d = x_all.shape
    tm = TOKEN_TILE
    off = 1 if skip_ctx else 0
    nt = t // tm - off
    tok_spec = lambda w: pl.BlockSpec((1, tm, w), lambda b, j: (b, j + off, 0))
    const = lambda a: pl.BlockSpec(a.shape, lambda b, j: (0,) * a.ndim)
    return pl.pallas_call(
        functools.partial(_merge_kernel, n_batch=nb, tile_off=off),
        grid=(nb, nt),
        in_specs=[tok_spec(d), const(mod), const(nw)] + [tok_spec(BRANCH_W)] * 6
                 + [const(chan), const(wf_bd), const(w_br), const(w_merge), const(b_merge),
                    const(w_out)],
        out_specs=pl.BlockSpec((1, tm, d), lambda b, j: (b, j, 0)),
        out_shape=jax.ShapeDtypeStruct((nb, nt * tm, d), F32),
        compiler_params=pltpu.CompilerParams(vmem_limit_bytes=VMEM_LIMIT),
        name="merge",
    )(x_all, mod, nw, ya, yd, pr, pi, zb, yc, chan, wf_bd, w_br, w_merge, b_merge, w_out)


def _rope_tables(seq_len, ctx_len):
    t = jnp.arange(seq_len, dtype=jnp.int32)
    r = (t // GRID_W).astype(F32)
    col = (t % GRID_W).astype(F32)
    nf = HEAD_DIM // 4
    inv = ROPE_BASE ** (-jnp.arange(nf, dtype=F32) / nf)
    ar = r[:, None] * inv[None, :]
    ac = col[:, None] * inv[None, :]
    cr, sr, cc, sc = jnp.cos(ar), jnp.sin(ar), jnp.cos(ac), jnp.sin(ac)
    cos_h = jnp.concatenate([cr, cr, cc, cc], axis=1)
    sin_h = jnp.concatenate([-sr, sr, -sc, sc], axis=1)
    cos_t = jnp.concatenate([jnp.ones((ctx_len, HEAD_DIM), F32), cos_h], axis=0)
    sin_t = jnp.concatenate([jnp.zeros((ctx_len, HEAD_DIM), F32), sin_h], axis=0)
    return jnp.tile(cos_t, (1, 2)), jnp.tile(sin_t, (1, 2))


def kernel(x, c, ctx, c_ctx, norm_w, w_ada, b_ada, w_in, qn_a, kn_a, qn_d, kn_d, sink_d,
           w_fnet, w_sp, b_sp, w_br, w_merge, b_merge, w_out):
    nb, seq, d = x.shape
    ctx_len = ctx.shape[1]
    depth = norm_w.shape[0]
    assert ctx_len == TOKEN_TILE and seq % (2 * ATTN_A_KEYS) == 0 and nb < 8

    cvecs = jnp.zeros((8, d), F32).at[:nb].set(c).at[nb].set(c_ctx)
    mods = _modulation(cvecs, w_ada, b_ada)

    cos_t, sin_t = _rope_tables(seq, ctx_len)
    m1, m3, mc, chan = _dft_tables(seq, ctx_len)
    ones_bd = jnp.asarray(np.kron(np.eye(2), np.full((HEAD_DIM, HEAD_DIM), 1.0 / HEAD_DIM)), BF16)
    pair = lambda w: jnp.tile(w, 2).reshape(1, LANES)
    eye_g = jnp.eye(BRANCH_W // GROUP_W, dtype=F32)

    x_all = jnp.concatenate([ctx, x], axis=1)
    for l in range(depth):
        nw = norm_w[l].reshape(1, d)
        wf_bd = jnp.einsum('gh,gcd->gchd', eye_g, w_fnet[l]).reshape(BRANCH_W, BRANCH_W)
        b_sp_t = jnp.repeat(b_sp[l].T, GROUP_W, axis=1)
        (qat, ka, vat, za, qd, kdt, vd, zd, fb, zb, yc) = _projection(
            x_all, mods[l], nw, w_in[l].astype(BF16), ones_bd, cos_t, sin_t,
            pair(qn_a[l]), pair(kn_a[l]), pair(qn_d[l]), pair(kn_d[l]),
            w_sp[l].astype(BF16), b_sp_t)
        ya = _attention_a(qat, ka, vat, za, ctx_len)
        yd = _attention_d(sink_d[l], qd, kdt, vd, zd, ctx_len, seq)
        pr, pi = _fnet_dft(fb, m1, m3, mc, ctx_len)
        x_all = _merge(x_all, mods[l], nw, ya, yd, pr, pi, zb, yc, chan, wf_bd.astype(BF16),
                       w_br[l].astype(BF16), w_merge[l].astype(BF16),
                       b_merge[l].reshape(1, -1), w_out[l].astype(BF16),
                       skip_ctx=(l == depth - 1))
    return x_all
```

```python
import functools

import numpy as np
import jax
import jax.numpy as jnp
from jax import lax
from jax.experimental import pallas as pl
from jax.experimental.pallas import tpu as pltpu

F32 = jnp.float32
BF16 = jnp.bfloat16

GRID_W = 64
HEAD_DIM = 64
BRANCH_W = 256
KV_W = 128
N_BRANCH = 4
GROUP_W = 64
CHUNK = 128
Q_BLOCK = 128
WINDOW = 128
ROPE_BASE = 10000.0
EPS = 1e-6
LANES = 128
TOKEN_TILE = 256
V_ROWS = 80
ATTN_A_KEYS = 512
DFT_BLOCK = 8
DFT_INNER = 64
VMEM_LIMIT = 56 * 1024 * 1024

_IN_SIZES = (BRANCH_W, KV_W, KV_W, BRANCH_W, BRANCH_W, KV_W, KV_W, BRANCH_W,
             BRANCH_W, BRANCH_W, BRANCH_W, BRANCH_W, BRANCH_W)
_IN_OFF = tuple(int(v) for v in np.cumsum((0,) + _IN_SIZES))
IN_W = _IN_OFF[-1]
(_AQ, _AK, _AV, _AZ, _DQ, _DK, _DV, _DZ, _BF, _BZ, _CU, _CV, _CZ) = _IN_OFF[:-1]


def _silu(z):
    return z * jax.nn.sigmoid(z)


def _gelu(x):
    return 0.5 * x * (1.0 + lax.erf(x * np.float32(np.sqrt(0.5))))


def _dot(a, b):
    return jnp.dot(a, b, preferred_element_type=F32)


def _lane_lo(width=LANES):
    lane = lax.broadcasted_iota(jnp.int32, (1, width), 1)
    return (lane % LANES) < HEAD_DIM


def _mod_kernel(cv_ref, w_ref, b_ref, o_ref):
    s = _silu(cv_ref[...])
    o_ref[0] = _dot(s.astype(BF16), w_ref[0].astype(BF16)) + b_ref[0]


def _modulation(cvecs, w_ada, b_ada):
    depth, d, d3 = w_ada.shape
    nblk = d3 // d
    return pl.pallas_call(
        _mod_kernel,
        grid=(depth, nblk),
        in_specs=[pl.BlockSpec((8, d), lambda l, n: (0, 0)),
                  pl.BlockSpec((1, d, d), lambda l, n: (l, 0, n)),
                  pl.BlockSpec((1, 1, d), lambda l, n: (l, 0, n))],
        out_specs=pl.BlockSpec((1, 8, d), lambda l, n: (l, 0, n)),
        out_shape=jax.ShapeDtypeStruct((depth, 8, d3), F32),
        name="modulation",
    )(cvecs, w_ada, b_ada.reshape(depth, 1, d3))


def _modulated_norm(x, mod_ref, nw_ref, row, d):
    m = mod_ref[pl.ds(row, 1), :]
    sh, sc = m[:, :d], m[:, d:2 * d]
    ms = jnp.mean(x * x, axis=-1, keepdims=True)
    xn = x * lax.rsqrt(ms + EPS) * nw_ref[...]
    return xn * (1.0 + sc) + sh, m[:, 2 * d:]


def _head_norm_rope(xs, ones_ref, wn, cos, sin, scale):
    sq = xs * xs
    hi = sq.astype(BF16)
    lo = (sq - hi.astype(F32)).astype(BF16)
    ms = _dot(hi, ones_ref[...]) + _dot(lo, ones_ref[...])
    y = xs * lax.rsqrt(ms + EPS) * wn
    lane = lax.broadcasted_iota(jnp.int32, (1, LANES), 1)
    first = (lane % 32) < 16
    sw = jnp.where(first, pltpu.roll(y, LANES - 16, axis=1), pltpu.roll(y, 16, axis=1))
    y = y * cos + sw * sin
    return y * scale if scale != 1.0 else y


def _dup_heads(x):
    r = pltpu.roll(x, HEAD_DIM, axis=1)
    lo = _lane_lo()
    return jnp.concatenate([jnp.where(lo, x, r), jnp.where(lo, r, x)], axis=1)


def _proj_kernel(x_ref, mod_ref, nw_ref, win_ref, ones_ref, cos_ref, sin_ref,
                 qna_ref, kna_ref, qnd_ref, knd_ref, wsp_ref, bsp_ref,
                 qat_ref, ka_ref, vat_ref, za_ref, qd_ref, kdt_ref, vd_ref, zd_ref,
                 fb_ref, fbc_ref, zb_ref, yc_ref, *, n_batch):
    b, j = pl.program_id(0), pl.program_id(1)
    d = x_ref.shape[-1]
    row = jnp.where(j == 0, n_batch, b)
    h, _ = _modulated_norm(x_ref[0], mod_ref, nw_ref, row, d)
    p = _dot(h.astype(BF16), win_ref[...])
    cos, sin = cos_ref[...], sin_ref[...]
    q_scale = HEAD_DIM ** -0.5

    def head_pairs(q0, qn_ref):
        return [_head_norm_rope(p[:, q0 + s * LANES:q0 + (s + 1) * LANES], ones_ref, qn_ref[...],
                                cos, sin, q_scale) for s in range(BRANCH_W // LANES)]

    qat_ref[0] = jnp.concatenate(head_pairs(_AQ, qna_ref), axis=1).T.astype(BF16)
    ka_ref[0] = _head_norm_rope(p[:, _AK:_AK + KV_W], ones_ref, kna_ref[...], cos, sin,
                                1.0).astype(BF16)
    vt = p[:, _AV:_AV + KV_W].T
    ones = jnp.ones((V_ROWS - HEAD_DIM, vt.shape[1]), F32)
    vat_ref[0] = jnp.concatenate([vt[:HEAD_DIM], ones, vt[HEAD_DIM:], ones], axis=0).astype(BF16)
    za_ref[0] = _silu(p[:, _AZ:_AZ + BRANCH_W])

    for s, y in enumerate(head_pairs(_DQ, qnd_ref)):
        qd_ref[0, :, s * LANES:(s + 1) * LANES] = y.astype(BF16)
    kd = _head_norm_rope(p[:, _DK:_DK + KV_W], ones_ref, knd_ref[...], cos, sin, 1.0)
    kdt_ref[0] = _dup_heads(kd).T.astype(BF16)
    vd_ref[0] = _dup_heads(p[:, _DV:_DV + KV_W]).astype(BF16)
    zd_ref[0] = _silu(p[:, _DZ:_DZ + BRANCH_W])

    fb_ref[0] = p[:, _BF:_BF + BRANCH_W]

    @pl.when(j == 0)
    def _():
        fbc_ref[0] = p[:, _BF:_BF + BRANCH_W]

    zb_ref[0] = _silu(p[:, _BZ:_BZ + BRANCH_W])

    u = _gelu(p[:, _CU:_CU + BRANCH_W])
    v = _gelu(p[:, _CV:_CV + BRANCH_W]).astype(BF16)
    zc = _silu(p[:, _CZ:_CZ + BRANCH_W])
    group = lax.broadcasted_iota(jnp.int32, (1, BRANCH_W), 1) // GROUP_W
    for c in range(x_ref.shape[1] // CHUNK):
        rows = slice(c * CHUNK, (c + 1) * CHUNK)
        sp = bsp_ref[...]
        for g in range(BRANCH_W // GROUP_W):
            sp = sp + jnp.where(group == g, _dot(wsp_ref[g], v[rows]), 0.0)
        yc_ref[0, rows, :] = (u[rows] * sp * zc[rows]).astype(BF16)


def _projection(x_all, mod, nw, w_in, ones_bd, cos_t, sin_t, qna, kna, qnd, knd, w_sp, b_sp_t):
    nb, t, d = x_all.shape
    tm = TOKEN_TILE
    nt = t // tm
    tok = lambda w, dt: jax.ShapeDtypeStruct((nb, t, w), dt)
    tok_spec = lambda w: pl.BlockSpec((1, tm, w), lambda b, j: (b, j, 0))
    tr = lambda w: jax.ShapeDtypeStruct((nb, w, t), BF16)
    tr_spec = lambda w: pl.BlockSpec((1, w, tm), lambda b, j: (b, 0, j))
    const = lambda a: pl.BlockSpec(a.shape, lambda b, j: (0,) * a.ndim)
    return pl.pallas_call(
        functools.partial(_proj_kernel, n_batch=nb),
        grid=(nb, nt),
        in_specs=[tok_spec(d), const(mod), const(nw), const(w_in), const(ones_bd),
                  pl.BlockSpec((tm, LANES), lambda b, j: (j, 0)),
                  pl.BlockSpec((tm, LANES), lambda b, j: (j, 0)),
                  const(qna), const(kna), const(qnd), const(knd), const(w_sp), const(b_sp_t)],
        out_specs=[tr_spec(BRANCH_W), tok_spec(KV_W), tr_spec(2 * V_ROWS), tok_spec(BRANCH_W),
                   tok_spec(BRANCH_W), tr_spec(2 * KV_W), tok_spec(2 * KV_W), tok_spec(BRANCH_W),
                   pl.BlockSpec((1, tm, BRANCH_W), lambda b, j: (b, jnp.maximum(j - 1, 0), 0)),
                   pl.BlockSpec((1, tm, BRANCH_W), lambda b, j: (b, 0, 0)),
                   tok_spec(BRANCH_W), tok_spec(BRANCH_W)],
        out_shape=[tr(BRANCH_W), tok(KV_W, BF16), tr(2 * V_ROWS), tok(BRANCH_W, F32),
                   tok(BRANCH_W, BF16), tr(2 * KV_W), tok(2 * KV_W, BF16), tok(BRANCH_W, F32),
                   jax.ShapeDtypeStruct((nb, t - tm, BRANCH_W), F32),
                   jax.ShapeDtypeStruct((nb, tm, BRANCH_W), F32),
                   tok(BRANCH_W, F32), tok(BRANCH_W, BF16)],
        compiler_params=pltpu.CompilerParams(vmem_limit_bytes=VMEM_LIMIT),
        name="projection",
    )(x_all, mod, nw, w_in, ones_bd, cos_t, sin_t, qna, kna, qnd, knd, w_sp, b_sp_t)


def _stack_heads(q2):
    lo = _lane_lo()
    zero = jnp.zeros_like(q2)
    return jnp.concatenate([jnp.where(lo, q2, zero), jnp.where(lo, zero, q2)], axis=0)


def _unstack_heads(o, tq):
    return jnp.where(_lane_lo(), o[:tq], o[tq:])


def _attn_a_kernel(qt_ref, k_ref, vt_ref, z_ref, o_ref, s_ref, smax_ref, m_ref, acc_ref,
                   *, tk, ctx_len):
    j = pl.program_id(1)
    tq = qt_ref.shape[2]
    t = k_ref.shape[1]
    n_lat = (t - ctx_len) // tk
    zeros = jnp.zeros((HEAD_DIM, 2 * tq), BF16)
    groups = range(2)
    qt_g = []
    for g in groups:
        r0 = g * LANES
        qrow = jnp.concatenate([qt_ref[0, r0:r0 + HEAD_DIM, :],
                                qt_ref[0, r0 + HEAD_DIM:r0 + LANES, :]], axis=1)
        qt_g.append(jnp.concatenate([qrow, zeros] if g == 0 else [zeros, qrow], axis=0))

    def scores(g, start, size):
        return _dot(k_ref[0, pl.ds(start, size), :], qt_g[g])

    def values(g, start, size):
        return vt_ref[0, g * V_ROWS:(g + 1) * V_ROWS, pl.ds(start, size)]

    def lat_start(c):
        return pl.multiple_of(ctx_len + c * tk, LANES)

    for g in groups:
        s = scores(g, 0, ctx_len)
        m = jnp.max(s, axis=0, keepdims=True)
        m_ref[g] = m
        acc_ref[g] = _dot(values(g, 0, ctx_len), jnp.exp(s - m).astype(BF16))

    def produce(g, slot, c):
        s = scores(g, lat_start(c), tk)
        smax_ref[slot, g] = jnp.max(s, axis=0, keepdims=True)
        s_ref[slot, g] = s

    def consume(g, slot, c):
        m = m_ref[g]
        m_new = jnp.maximum(m, smax_ref[slot, g])
        pr = jnp.exp(s_ref[slot, g] - m_new).astype(BF16)
        acc_ref[g] = jnp.exp(m - m_new) * acc_ref[g] + _dot(values(g, lat_start(c), tk), pr)
        m_ref[g] = m_new

    @pl.when(j > 0)
    def _():
        for g in groups:
            produce(g, 0, 0)

        def body(i, carry):
            c = 2 * i
            for g in groups:
                produce(g, 1, c + 1)
                consume(g, 0, c)
            for g in groups:
                produce(g, 0, c + 2)
                consume(g, 1, c + 1)
            return carry

        lax.fori_loop(0, n_lat // 2 - 1, body, 0)
        for g in groups:
            produce(g, 1, n_lat - 1)
            consume(g, 0, n_lat - 2)
        for g in groups:
            consume(g, 1, n_lat - 1)

    for g in groups:
        acc = acc_ref[g]
        o = acc[:HEAD_DIM] / acc[HEAD_DIM:HEAD_DIM + 1]
        o = jnp.concatenate([o[:, :tq], o[:, tq:]], axis=0).T
        cols = slice(g * LANES, (g + 1) * LANES)
        o_ref[0, :, cols] = (o * z_ref[0, :, cols]).astype(BF16)


def _attention_a(qt, k, vt, z, ctx_len):
    nb, t, _ = k.shape
    tq = TOKEN_TILE
    tk = ATTN_A_KEYS
    return pl.pallas_call(
        functools.partial(_attn_a_kernel, tk=tk, ctx_len=ctx_len),
        grid=(nb, t // tq),
        in_specs=[pl.BlockSpec((1, BRANCH_W, tq), lambda b, j: (b, 0, j)),
                  pl.BlockSpec((1, t, KV_W), lambda b, j: (b, 0, 0)),
                  pl.BlockSpec((1, 2 * V_ROWS, t), lambda b, j: (b, 0, 0)),
                  pl.BlockSpec((1, tq, BRANCH_W), lambda b, j: (b, j, 0))],
        out_specs=pl.BlockSpec((1, tq, BRANCH_W), lambda b, j: (b, j, 0)),
        out_shape=jax.ShapeDtypeStruct((nb, t, BRANCH_W), BF16),
        scratch_shapes=[pltpu.VMEM((2, 2, tk, 2 * tq), F32),
                        pltpu.VMEM((2, 2, 1, 2 * tq), F32),
                        pltpu.VMEM((2, 1, 2 * tq), F32),
                        pltpu.VMEM((2, V_ROWS, 2 * tq), F32)],
        compiler_params=pltpu.CompilerParams(vmem_limit_bytes=VMEM_LIMIT),
        name="attention_a",
    )(qt, k, vt, z)


def _attn_d_kernel(sink_ref, q_ref, ktc_ref, kt0_ref, kt1_ref, kt2_ref,
                   vc_ref, v0_ref, v1_ref, v2_ref, z_ref, o_ref, *, ctx_blocks, seq_len):
    bi = pl.program_id(1)
    qb = Q_BLOCK
    i = bi - ctx_blocks
    r = lax.broadcasted_iota(jnp.int32, (2 * qb, 3 * qb), 0) % qb
    jj = lax.broadcasted_iota(jnp.int32, (2 * qb, 3 * qb), 1)
    rel = jj - qb - r
    kpos = (i - 1) * qb + jj
    mask = (jnp.abs(rel) <= WINDOW) & (kpos >= 0) & (kpos < seq_len) & (i >= 0)
    top = lax.broadcasted_iota(jnp.int32, (2 * qb, 1), 0) < qb
    for g in range(2):
        cols = slice(g * LANES, (g + 1) * LANES)
        qs = _stack_heads(q_ref[0, :, cols])
        kt_loc = jnp.concatenate([kt0_ref[0, cols, :], kt1_ref[0, cols, :], kt2_ref[0, cols, :]],
                                 axis=1)
        v_loc = jnp.concatenate([v0_ref[0, :, cols], v1_ref[0, :, cols], v2_ref[0, :, cols]],
                                axis=0)
        s_ctx = _dot(qs, ktc_ref[0, cols, :])
        s_loc = jnp.where(mask, _dot(qs, kt_loc), -jnp.inf)
        sink = jnp.where(top, sink_ref[2 * g], sink_ref[2 * g + 1])
        m = jnp.maximum(jnp.maximum(jnp.max(s_ctx, axis=1, keepdims=True),
                                    jnp.max(s_loc, axis=1, keepdims=True)), sink)
        p_ctx = jnp.exp(s_ctx - m)
        p_loc = jnp.exp(s_loc - m)
        l = (jnp.sum(p_ctx, axis=1, keepdims=True) + jnp.sum(p_loc, axis=1, keepdims=True)
             + jnp.exp(sink - m))
        acc = _dot(p_ctx.astype(BF16), vc_ref[0, :, cols]) + _dot(p_loc.astype(BF16), v_loc)
        o = _unstack_heads(acc / l, qb)
        o_ref[0, :, cols] = (o * z_ref[0, :, cols]).astype(BF16)


def _attention_d(sink, q, kt, v, z, ctx_len, seq_len):
    nb, t, _ = q.shape
    qb = Q_BLOCK
    nblk = t // qb
    cb = ctx_len // qb
    prev = lambda b, i: (b, 0, jnp.maximum(i - 1, 0))
    cur = lambda b, i: (b, 0, i)
    nxt = lambda b, i: (b, 0, jnp.minimum(i + 1, nblk - 1))
    vprev = lambda b, i: (b, jnp.maximum(i - 1, 0), 0)
    vcur = lambda b, i: (b, i, 0)
    vnxt = lambda b, i: (b, jnp.minimum(i + 1, nblk - 1), 0)
    kt_blk = lambda im: pl.BlockSpec((1, 2 * KV_W, qb), im)
    v_blk = lambda im: pl.BlockSpec((1, qb, 2 * KV_W), im)
    return pl.pallas_call(
        functools.partial(_attn_d_kernel, ctx_blocks=cb, seq_len=seq_len),
        grid=(nb, nblk),
        in_specs=[pl.BlockSpec(memory_space=pltpu.SMEM),
                  pl.BlockSpec((1, qb, BRANCH_W), vcur),
                  pl.BlockSpec((1, 2 * KV_W, ctx_len), lambda b, i: (b, 0, 0)),
                  kt_blk(prev), kt_blk(cur), kt_blk(nxt),
                  pl.BlockSpec((1, ctx_len, 2 * KV_W), lambda b, i: (b, 0, 0)),
                  v_blk(vprev), v_blk(vcur), v_blk(vnxt),
                  pl.BlockSpec((1, qb, BRANCH_W), vcur)],
        out_specs=pl.BlockSpec((1, qb, BRANCH_W), vcur),
        out_shape=jax.ShapeDtypeStruct((nb, t, BRANCH_W), BF16),
        compiler_params=pltpu.CompilerParams(vmem_limit_bytes=VMEM_LIMIT),
        name="attention_d",
    )(sink, q, kt, kt, kt, kt, v, v, v, v, z)


def _dft_tables(seq_len, ctx_len):
    n2c = DFT_INNER
    n1c = seq_len // n2c
    scale = 1.0 / np.sqrt(float(seq_len) * GROUP_W)
    k1 = np.arange(n1c)[:, None]
    n1 = np.arange(n1c)[None, :]
    m1 = np.zeros((n2c, 2 * n1c, n1c), np.float64)
    for n2 in range(n2c):
        ang = -2.0 * np.pi * (k1 * n1 / n1c + n2 * k1 / seq_len)
        m1[n2, :n1c] = np.cos(ang) * scale
        m1[n2, n1c:] = np.sin(ang) * scale
    ang3 = 2.0 * np.pi * np.outer(np.arange(n2c), np.arange(n2c)) / n2c
    c3, s3 = np.cos(ang3), np.sin(ang3)
    m3 = np.block([[c3, s3], [-s3, c3]])
    cscale = 1.0 / np.sqrt(float(ctx_len) * GROUP_W)
    angc = 2.0 * np.pi * np.outer(np.arange(ctx_len), np.arange(ctx_len)) / ctx_len
    mc = np.concatenate([np.cos(angc), -np.sin(angc)], axis=0) * cscale
    angg = 2.0 * np.pi * np.outer(np.arange(GROUP_W), np.arange(GROUP_W)) / GROUP_W
    eye = np.eye(BRANCH_W // GROUP_W)
    chan = np.concatenate([np.kron(eye, np.cos(angg)), np.kron(eye, np.sin(angg))], axis=0)
    return _split(m1), _split(m3), _split(mc), _split(chan)


def _split(a):
    a32 = jnp.asarray(a.astype(np.float32))
    hi = a32.astype(BF16)
    return hi, (a32 - hi.astype(F32)).astype(BF16)


def _dot3_left(m_hi, m_lo, x):
    x_hi = x.astype(BF16)
    x_lo = (x - x_hi.astype(F32)).astype(BF16)
    return _dot(m_hi, x_hi) + _dot(m_lo, x_hi) + _dot(m_hi, x_lo)


def _dot3_right(x, m_hi, m_lo):
    x_hi = x.astype(BF16)
    x_lo = (x - x_hi.astype(F32)).astype(BF16)
    return _dot(x_hi, m_hi) + _dot(x_hi, m_lo) + _dot(x_lo, m_hi)


def _fnet_stage1_kernel(f_ref, mh_ref, ml_ref, br_ref, bi_ref):
    n1c = f_ref.shape[1]
    for q in range(f_ref.shape[2]):
        res = _dot3_left(mh_ref[q], ml_ref[q], f_ref[0, :, q, :])
        br_ref[0, :, q, :] = res[:n1c]
        bi_ref[0, :, q, :] = res[n1c:]


def _fnet_stage2_kernel(br_ref, bi_ref, fc_ref, mh_ref, ml_ref, mch_ref, mcl_ref,
                        pr_ref, pi_ref, prc_ref, pic_ref):
    n2c = br_ref.shape[2]
    ctx_len = fc_ref.shape[1]

    @pl.when(pl.program_id(1) == 0)
    def _():
        res = _dot3_left(mch_ref[...], mcl_ref[...], fc_ref[0])
        prc_ref[0] = res[:ctx_len]
        pic_ref[0] = res[ctx_len:]

    for q in range(br_ref.shape[1]):
        xin = jnp.concatenate([br_ref[0, q], bi_ref[0, q]], axis=0)
        res = _dot3_left(mh_ref[...], ml_ref[...], xin)
        pr_ref[0, :, q, :] = res[:n2c]
        pi_ref[0, :, q, :] = res[n2c:]


def _fnet_dft(fb, fb_ctx, m1, m3, mc):
    nb, seq, w = fb.shape
    ctx_len = fb_ctx.shape[1]
    n2c = DFT_INNER
    n1c = seq // n2c
    sub = DFT_BLOCK
    f4 = fb.reshape(nb, n1c, n2c, w)
    blk1 = pl.BlockSpec((1, n1c, sub, w), lambda b, i: (b, 0, i, 0))
    tab1 = pl.BlockSpec((sub, 2 * n1c, n1c), lambda b, i: (i, 0, 0))
    b4 = jax.ShapeDtypeStruct((nb, n1c, n2c, w), F32)
    br, bi = pl.pallas_call(
        _fnet_stage1_kernel,
        grid=(nb, n2c // sub),
        in_specs=[blk1, tab1, tab1],
        out_specs=[blk1, blk1],
        out_shape=[b4, b4],
        compiler_params=pltpu.CompilerParams(vmem_limit_bytes=VMEM_LIMIT),
        name="fnet_stage1",
    )(f4, *m1)
    blk2 = pl.BlockSpec((1, sub, n2c, w), lambda b, i: (b, i, 0, 0))
    oblk = pl.BlockSpec((1, n2c, sub, w), lambda b, i: (b, 0, i, 0))
    cblk = pl.BlockSpec((1, ctx_len, w), lambda b, i: (b, 0, 0))
    const = lambda a: pl.BlockSpec(a.shape, lambda b, i: (0,) * a.ndim)
    p4 = jax.ShapeDtypeStruct((nb, n2c, n1c, w), F32)
    pc = jax.ShapeDtypeStruct((nb, ctx_len, w), F32)
    pr, pi, prc, pic = pl.pallas_call(
        _fnet_stage2_kernel,
        grid=(nb, n1c // sub),
        in_specs=[blk2, blk2, cblk, const(m3[0]), const(m3[1]), const(mc[0]), const(mc[1])],
        out_specs=[oblk, oblk, cblk, cblk],
        out_shape=[p4, p4, pc, pc],
        compiler_params=pltpu.CompilerParams(vmem_limit_bytes=VMEM_LIMIT),
        name="fnet_stage2",
    )(br, bi, fb_ctx, *m3, *mc)
    return pr.reshape(nb, seq, w), pi.reshape(nb, seq, w), prc, pic


def _merge_kernel(x_ref, mod_ref, nw_ref, ya_ref, yd_ref, pr_ref, pi_ref, prc_ref, pic_ref,
                  zb_ref, yc_ref, chh_ref, chl_ref, wf_ref, wbr_ref, wm_ref, bm_ref, wo_ref, o_ref,
                  *, n_batch, tile_off):
    b, j = pl.program_id(0), pl.program_id(1)
    d = x_ref.shape[-1]
    x = x_ref[0]
    is_ctx = j + tile_off == 0
    row = jnp.where(is_ctx, n_batch, b)
    h, gate = _modulated_norm(x, mod_ref, nw_ref, row, d)
    pc = jnp.concatenate([jnp.where(is_ctx, prc_ref[0], pr_ref[0]),
                          jnp.where(is_ctx, pic_ref[0], pi_ref[0])], axis=1)
    yb = _dot(_dot3_right(pc, chh_ref[...], chl_ref[...]).astype(BF16), wf_ref[...]) * zb_ref[0]
    branches = (ya_ref[0], yd_ref[0], yb.astype(BF16), yc_ref[0])
    hb = h.astype(BF16)
    acc = jnp.zeros((x.shape[0], d), F32)
    for r in range(N_BRANCH):
        g = jax.nn.sigmoid(_dot(hb, wm_ref[:, r * d:(r + 1) * d]) + bm_ref[:, r * d:(r + 1) * d])
        acc = acc + g * _dot(branches[r], wbr_ref[r])
    o_ref[0] = x + gate * _dot(acc.astype(BF16), wo_ref[...])


def _merge(x_all, mod, nw, ya, yd, pr, pi, prc, pic, zb, yc, chan, wf_bd, w_br, w_merge, b_merge,
           w_out, skip_ctx):
    nb, t, d = x_all.shape
    tm = TOKEN_TILE
    off = 1 if skip_ctx else 0
    nt = t // tm - off
    tok_spec = lambda w: pl.BlockSpec((1, tm, w), lambda b, j: (b, j + off, 0))
    lat_spec = lambda w: pl.BlockSpec((1, tm, w), lambda b, j: (b, jnp.maximum(j + off - 1, 0), 0))
    ctx_spec = lambda a: pl.BlockSpec((1,) + a.shape[1:], lambda b, j: (b, 0, 0))
    const = lambda a: pl.BlockSpec(a.shape, lambda b, j: (0,) * a.ndim)
    return pl.pallas_call(
        functools.partial(_merge_kernel, n_batch=nb, tile_off=off),
        grid=(nb, nt),
        in_specs=[tok_spec(d), const(mod), const(nw), tok_spec(BRANCH_W), tok_spec(BRANCH_W),
                  lat_spec(BRANCH_W), lat_spec(BRANCH_W), ctx_spec(prc), ctx_spec(pic),
                  tok_spec(BRANCH_W), tok_spec(BRANCH_W), const(chan[0]), const(chan[1]),
                  const(wf_bd), const(w_br), const(w_merge), const(b_merge), const(w_out)],
        out_specs=pl.BlockSpec((1, tm, d), lambda b, j: (b, j, 0)),
        out_shape=jax.ShapeDtypeStruct((nb, nt * tm, d), F32),
        compiler_params=pltpu.CompilerParams(vmem_limit_bytes=VMEM_LIMIT),
        name="merge",
    )(x_all, mod, nw, ya, yd, pr, pi, prc, pic, zb, yc, *chan, wf_bd, w_br, w_merge, b_merge,
      w_out)


def _rope_tables(seq_len, ctx_len):
    t = jnp.arange(seq_len, dtype=jnp.int32)
    r = (t // GRID_W).astype(F32)
    col = (t % GRID_W).astype(F32)
    nf = HEAD_DIM // 4
    inv = ROPE_BASE ** (-jnp.arange(nf, dtype=F32) / nf)
    ar = r[:, None] * inv[None, :]
    ac = col[:, None] * inv[None, :]
    cr, sr, cc, sc = jnp.cos(ar), jnp.sin(ar), jnp.cos(ac), jnp.sin(ac)
    cos_h = jnp.concatenate([cr, cr, cc, cc], axis=1)
    sin_h = jnp.concatenate([-sr, sr, -sc, sc], axis=1)
    cos_t = jnp.concatenate([jnp.ones((ctx_len, HEAD_DIM), F32), cos_h], axis=0)
    sin_t = jnp.concatenate([jnp.zeros((ctx_len, HEAD_DIM), F32), sin_h], axis=0)
    return jnp.tile(cos_t, (1, 2)), jnp.tile(sin_t, (1, 2))


def kernel(x, c, ctx, c_ctx, norm_w, w_ada, b_ada, w_in, qn_a, kn_a, qn_d, kn_d, sink_d,
           w_fnet, w_sp, b_sp, w_br, w_merge, b_merge, w_out):
    nb, seq, d = x.shape
    ctx_len = ctx.shape[1]
    depth = norm_w.shape[0]
    assert ctx_len == TOKEN_TILE and seq % (2 * ATTN_A_KEYS) == 0 and nb < 8

    cvecs = jnp.zeros((8, d), F32).at[:nb].set(c).at[nb].set(c_ctx)
    mods = _modulation(cvecs, w_ada, b_ada)

    cos_t, sin_t = _rope_tables(seq, ctx_len)
    m1, m3, mc, chan = _dft_tables(seq, ctx_len)
    ones_bd = jnp.asarray(np.kron(np.eye(2), np.full((HEAD_DIM, HEAD_DIM), 1.0 / HEAD_DIM)), BF16)
    pair = lambda w: jnp.tile(w, 2).reshape(1, LANES)
    eye_g = jnp.eye(BRANCH_W // GROUP_W, dtype=F32)

    x_all = jnp.concatenate([ctx, x], axis=1)
    for l in range(depth):
        nw = norm_w[l].reshape(1, d)
        wf_bd = jnp.einsum('gh,gcd->gchd', eye_g, w_fnet[l]).reshape(BRANCH_W, BRANCH_W)
        b_sp_t = jnp.repeat(b_sp[l].T, GROUP_W, axis=1)
        (qat, ka, vat, za, qd, kdt, vd, zd, fb, fbc, zb, yc) = _projection(
            x_all, mods[l], nw, w_in[l].astype(BF16), ones_bd, cos_t, sin_t,
            pair(qn_a[l]), pair(kn_a[l]), pair(qn_d[l]), pair(kn_d[l]),
            w_sp[l].astype(BF16), b_sp_t)
        ya = _attention_a(qat, ka, vat, za, ctx_len)
        yd = _attention_d(sink_d[l], qd, kdt, vd, zd, ctx_len, seq)
        pr, pi, prc, pic = _fnet_dft(fb, fbc, m1, m3, mc)
        x_all = _merge(x_all, mods[l], nw, ya, yd, pr, pi, prc, pic, zb, yc, chan,
                       wf_bd.astype(BF16),
                       w_br[l].astype(BF16), w_merge[l].astype(BF16),
                       b_merge[l].reshape(1, -1), w_out[l].astype(BF16),
                       skip_ctx=(l == depth - 1))
    return x_all
```

```python
import functools

import numpy as np
import jax
import jax.numpy as jnp
from jax import lax
from jax.experimental import pallas as pl
from jax.experimental.pallas import tpu as pltpu

F32 = jnp.float32
BF16 = jnp.bfloat16

GRID_W = 64
HEAD_DIM = 64
BRANCH_W = 256
KV_W = 128
N_BRANCH = 4
GROUP_W = 64
CHUNK = 128
Q_BLOCK = 128
WINDOW = 128
ROPE_BASE = 10000.0
EPS = 1e-6
LANES = 128
TOKEN_TILE = 256
V_ROWS = 80
ATTN_A_KEYS = 512
ATTN_A_UNROLL = 16
DFT_BLOCK = 8
DFT_INNER = 64
VMEM_LIMIT = 56 * 1024 * 1024

_IN_SIZES = (BRANCH_W, KV_W, KV_W, BRANCH_W, BRANCH_W, KV_W, KV_W, BRANCH_W,
             BRANCH_W, BRANCH_W, BRANCH_W, BRANCH_W, BRANCH_W)
_IN_OFF = tuple(int(v) for v in np.cumsum((0,) + _IN_SIZES))
IN_W = _IN_OFF[-1]
(_AQ, _AK, _AV, _AZ, _DQ, _DK, _DV, _DZ, _BF, _BZ, _CU, _CV, _CZ) = _IN_OFF[:-1]


def _silu(z):
    return z * jax.nn.sigmoid(z)


def _gelu(x):
    return 0.5 * x * (1.0 + lax.erf(x * np.float32(np.sqrt(0.5))))


def _dot(a, b):
    return jnp.dot(a, b, preferred_element_type=F32)


def _lane_lo(width=LANES):
    lane = lax.broadcasted_iota(jnp.int32, (1, width), 1)
    return (lane % LANES) < HEAD_DIM


def _mod_kernel(cv_ref, w_ref, b_ref, o_ref):
    s = _silu(cv_ref[...])
    o_ref[0] = _dot(s.astype(BF16), w_ref[0].astype(BF16)) + b_ref[0]


def _modulation(cvecs, w_ada, b_ada):
    depth, d, d3 = w_ada.shape
    nblk = d3 // d
    return pl.pallas_call(
        _mod_kernel,
        grid=(depth, nblk),
        in_specs=[pl.BlockSpec((8, d), lambda l, n: (0, 0)),
                  pl.BlockSpec((1, d, d), lambda l, n: (l, 0, n)),
                  pl.BlockSpec((1, 1, d), lambda l, n: (l, 0, n))],
        out_specs=pl.BlockSpec((1, 8, d), lambda l, n: (l, 0, n)),
        out_shape=jax.ShapeDtypeStruct((depth, 8, d3), F32),
        name="modulation",
    )(cvecs, w_ada, b_ada.reshape(depth, 1, d3))


def _modulated_norm(x, mod_ref, nw_ref, row, d):
    m = mod_ref[pl.ds(row, 1), :]
    sh, sc = m[:, :d], m[:, d:2 * d]
    ms = jnp.mean(x * x, axis=-1, keepdims=True)
    xn = x * lax.rsqrt(ms + EPS) * nw_ref[...]
    return xn * (1.0 + sc) + sh, m[:, 2 * d:]


def _head_norm_rope(xs, ones_ref, wn, cos, sin, scale):
    sq = xs * xs
    hi = sq.astype(BF16)
    lo = (sq - hi.astype(F32)).astype(BF16)
    ms = _dot(hi, ones_ref[...]) + _dot(lo, ones_ref[...])
    y = xs * lax.rsqrt(ms + EPS) * wn
    lane = lax.broadcasted_iota(jnp.int32, (1, LANES), 1)
    first = (lane % 32) < 16
    sw = jnp.where(first, pltpu.roll(y, LANES - 16, axis=1), pltpu.roll(y, 16, axis=1))
    y = y * cos + sw * sin
    return y * scale if scale != 1.0 else y


def _proj_kernel(x_ref, mod_ref, nw_ref, win_ref, ones_ref, cos_ref, sin_ref,
                 qna_ref, kna_ref, qnd_ref, knd_ref, wsp_ref, bsp_ref,
                 qat_ref, ka_ref, vat_ref, za_ref, qdt_ref, kd_ref, vdt_ref, zd_ref,
                 fb_ref, fbc_ref, zb_ref, yc_ref, *, n_batch):
    b, j = pl.program_id(0), pl.program_id(1)
    d = x_ref.shape[-1]
    row = jnp.where(j == 0, n_batch, b)
    h, _ = _modulated_norm(x_ref[0], mod_ref, nw_ref, row, d)
    p = _dot(h.astype(BF16), win_ref[...])
    cos, sin = cos_ref[...], sin_ref[...]
    q_scale = HEAD_DIM ** -0.5

    def attn_branch(q0, k0, v0, z0, qn_ref, kn_ref, qt_ref, k_ref, vt_ref, z_ref):
        q = [_head_norm_rope(p[:, q0 + s * LANES:q0 + (s + 1) * LANES], ones_ref, qn_ref[...],
                             cos, sin, q_scale) for s in range(BRANCH_W // LANES)]
        qt_ref[0] = jnp.concatenate(q, axis=1).T.astype(BF16)
        k_ref[0] = _head_norm_rope(p[:, k0:k0 + KV_W], ones_ref, kn_ref[...], cos, sin,
                                   1.0).astype(BF16)
        vt = p[:, v0:v0 + KV_W].T
        ones = jnp.ones((V_ROWS - HEAD_DIM, vt.shape[1]), F32)
        vt_ref[0] = jnp.concatenate([vt[:HEAD_DIM], ones, vt[HEAD_DIM:], ones],
                                    axis=0).astype(BF16)
        z_ref[0] = _silu(p[:, z0:z0 + BRANCH_W])

    attn_branch(_AQ, _AK, _AV, _AZ, qna_ref, kna_ref, qat_ref, ka_ref, vat_ref, za_ref)
    attn_branch(_DQ, _DK, _DV, _DZ, qnd_ref, knd_ref, qdt_ref, kd_ref, vdt_ref, zd_ref)

    fb_ref[0] = p[:, _BF:_BF + BRANCH_W]

    @pl.when(j == 0)
    def _():
        fbc_ref[0] = p[:, _BF:_BF + BRANCH_W]

    zb_ref[0] = _silu(p[:, _BZ:_BZ + BRANCH_W])

    u = _gelu(p[:, _CU:_CU + BRANCH_W])
    v = _gelu(p[:, _CV:_CV + BRANCH_W]).astype(BF16)
    zc = _silu(p[:, _CZ:_CZ + BRANCH_W])
    group = lax.broadcasted_iota(jnp.int32, (1, BRANCH_W), 1) // GROUP_W
    for c in range(x_ref.shape[1] // CHUNK):
        rows = slice(c * CHUNK, (c + 1) * CHUNK)
        sp = bsp_ref[...]
        for g in range(BRANCH_W // GROUP_W):
            sp = sp + jnp.where(group == g, _dot(wsp_ref[g], v[rows]), 0.0)
        yc_ref[0, rows, :] = (u[rows] * sp * zc[rows]).astype(BF16)


def _projection(x_all, mod, nw, w_in, ones_bd, cos_t, sin_t, qna, kna, qnd, knd, w_sp, b_sp_t):
    nb, t, d = x_all.shape
    tm = TOKEN_TILE
    nt = t // tm
    tok = lambda w, dt: jax.ShapeDtypeStruct((nb, t, w), dt)
    tok_spec = lambda w: pl.BlockSpec((1, tm, w), lambda b, j: (b, j, 0))
    tr = lambda w: jax.ShapeDtypeStruct((nb, w, t), BF16)
    tr_spec = lambda w: pl.BlockSpec((1, w, tm), lambda b, j: (b, 0, j))
    const = lambda a: pl.BlockSpec(a.shape, lambda b, j: (0,) * a.ndim)
    attn_specs = [tr_spec(BRANCH_W), tok_spec(KV_W), tr_spec(2 * V_ROWS), tok_spec(BRANCH_W)]
    attn_shapes = [tr(BRANCH_W), tok(KV_W, BF16), tr(2 * V_ROWS), tok(BRANCH_W, F32)]
    return pl.pallas_call(
        functools.partial(_proj_kernel, n_batch=nb),
        grid=(nb, nt),
        in_specs=[tok_spec(d), const(mod), const(nw), const(w_in), const(ones_bd),
                  pl.BlockSpec((tm, LANES), lambda b, j: (j, 0)),
                  pl.BlockSpec((tm, LANES), lambda b, j: (j, 0)),
                  const(qna), const(kna), const(qnd), const(knd), const(w_sp), const(b_sp_t)],
        out_specs=attn_specs * 2 + [
            pl.BlockSpec((1, tm, BRANCH_W), lambda b, j: (b, jnp.maximum(j - 1, 0), 0)),
            pl.BlockSpec((1, tm, BRANCH_W), lambda b, j: (b, 0, 0)),
            tok_spec(BRANCH_W), tok_spec(BRANCH_W)],
        out_shape=attn_shapes * 2 + [
            jax.ShapeDtypeStruct((nb, t - tm, BRANCH_W), F32),
            jax.ShapeDtypeStruct((nb, tm, BRANCH_W), F32),
            tok(BRANCH_W, F32), tok(BRANCH_W, BF16)],
        compiler_params=pltpu.CompilerParams(vmem_limit_bytes=VMEM_LIMIT),
        name="projection",
    )(x_all, mod, nw, w_in, ones_bd, cos_t, sin_t, qna, kna, qnd, knd, w_sp, b_sp_t)


def _group_queries(qt_ref, g):
    r0 = g * LANES
    qrow = jnp.concatenate([qt_ref[0, r0:r0 + HEAD_DIM, :], qt_ref[0, r0 + HEAD_DIM:r0 + LANES, :]],
                           axis=1)
    zeros = jnp.zeros_like(qrow)
    return jnp.concatenate([qrow, zeros] if g == 0 else [zeros, qrow], axis=0)


def _store_heads(o_ref, z_ref, g, o_t):
    tq = o_t.shape[1] // 2
    o = jnp.concatenate([o_t[:, :tq], o_t[:, tq:]], axis=0).T
    cols = slice(g * LANES, (g + 1) * LANES)
    o_ref[0, :, cols] = (o * z_ref[0, :, cols]).astype(BF16)


def _attn_a_kernel(qt_ref, k_ref, vt_ref, z_ref, o_ref, s_ref, smax_ref, m_ref, acc_ref,
                   *, tk, ctx_len):
    j = pl.program_id(1)
    tq = qt_ref.shape[2]
    t = k_ref.shape[1]
    n_lat = (t - ctx_len) // tk
    groups = range(2)
    qt_g = [_group_queries(qt_ref, g) for g in groups]

    def scores(g, start, size):
        return _dot(k_ref[0, pl.ds(start, size), :], qt_g[g])

    def values(g, start, size):
        return vt_ref[0, g * V_ROWS:(g + 1) * V_ROWS, pl.ds(start, size)]

    def lat_start(c):
        return pl.multiple_of(ctx_len + c * tk, LANES)

    def produce(g, slot, c):
        s = scores(g, lat_start(c), tk)
        smax_ref[slot, g] = jnp.max(s, axis=0, keepdims=True)
        s_ref[slot, g] = s

    s_ctx = [scores(g, 0, ctx_len) for g in groups]
    for g in groups:
        produce(g, 0, 0)
    for g in groups:
        m = jnp.max(s_ctx[g], axis=0, keepdims=True)
        m_ref[g] = m
        acc_ref[g] = _dot(values(g, 0, ctx_len), jnp.exp(s_ctx[g] - m).astype(BF16))

    def consume(g, slot, c):
        m = m_ref[g]
        m_new = jnp.maximum(m, smax_ref[slot, g])
        pr = jnp.exp(s_ref[slot, g] - m_new).astype(BF16)
        acc_ref[g] = jnp.exp(m - m_new) * acc_ref[g] + _dot(values(g, lat_start(c), tk), pr)
        m_ref[g] = m_new

    @pl.when(j > 0)
    def _():
        unroll = ATTN_A_UNROLL

        def body(i, carry):
            for k in range(unroll):
                c = unroll * i + k
                for g in groups:
                    produce(g, (k + 1) % 2, c + 1)
                    consume(g, k % 2, c)
            return carry

        n_loop = (n_lat - 1) // unroll
        if n_loop:
            lax.fori_loop(0, n_loop, body, 0)
        for c in range(n_loop * unroll, n_lat):
            for g in groups:
                if c + 1 < n_lat:
                    produce(g, (c + 1) % 2, c + 1)
                consume(g, c % 2, c)

    for g in groups:
        acc = acc_ref[g]
        _store_heads(o_ref, z_ref, g, acc[:HEAD_DIM] / acc[HEAD_DIM:HEAD_DIM + 1])


def _attention_a(qt, k, vt, z, ctx_len):
    nb, t, _ = k.shape
    tq = TOKEN_TILE
    tk = ATTN_A_KEYS
    return pl.pallas_call(
        functools.partial(_attn_a_kernel, tk=tk, ctx_len=ctx_len),
        grid=(nb, t // tq),
        in_specs=[pl.BlockSpec((1, BRANCH_W, tq), lambda b, j: (b, 0, j)),
                  pl.BlockSpec((1, t, KV_W), lambda b, j: (b, 0, 0)),
                  pl.BlockSpec((1, 2 * V_ROWS, t), lambda b, j: (b, 0, 0)),
                  pl.BlockSpec((1, tq, BRANCH_W), lambda b, j: (b, j, 0))],
        out_specs=pl.BlockSpec((1, tq, BRANCH_W), lambda b, j: (b, j, 0)),
        out_shape=jax.ShapeDtypeStruct((nb, t, BRANCH_W), BF16),
        scratch_shapes=[pltpu.VMEM((2, 2, tk, 2 * tq), F32),
                        pltpu.VMEM((2, 2, 1, 2 * tq), F32),
                        pltpu.VMEM((2, 1, 2 * tq), F32),
                        pltpu.VMEM((2, V_ROWS, 2 * tq), F32)],
        compiler_params=pltpu.CompilerParams(vmem_limit_bytes=VMEM_LIMIT),
        name="attention_a",
    )(qt, k, vt, z)


def _attn_d_kernel(sink_ref, qt_ref, k_ref, vt_ref, z_ref, o_ref, *, ctx_len):
    j = pl.program_id(1)
    tq = qt_ref.shape[2]
    t = k_ref.shape[1]
    span = tq + 2 * WINDOW
    q0 = j * tq
    start = pl.multiple_of(jnp.clip(q0 - WINDOW, ctx_len, t - span), LANES)
    kpos = start + lax.broadcasted_iota(jnp.int32, (span, 2 * tq), 0)
    qpos = q0 + lax.broadcasted_iota(jnp.int32, (span, 2 * tq), 1) % tq
    mask = (jnp.abs(kpos - qpos) <= WINDOW) & (j > 0)
    first = lax.broadcasted_iota(jnp.int32, (1, 2 * tq), 1) < tq
    scores = []
    for g in range(2):
        qt_g = _group_queries(qt_ref, g)
        scores.append((_dot(k_ref[0, :ctx_len, :], qt_g),
                       _dot(k_ref[0, pl.ds(start, span), :], qt_g)))
    for g in range(2):
        vrows = slice(g * V_ROWS, (g + 1) * V_ROWS)
        s_ctx = scores[g][0]
        s_loc = jnp.where(mask, scores[g][1], -jnp.inf)
        sink = jnp.where(first, sink_ref[2 * g], sink_ref[2 * g + 1])
        m = jnp.maximum(jnp.maximum(jnp.max(s_ctx, axis=0, keepdims=True),
                                    jnp.max(s_loc, axis=0, keepdims=True)), sink)
        acc = (_dot(vt_ref[0, vrows, :ctx_len], jnp.exp(s_ctx - m).astype(BF16))
               + _dot(vt_ref[0, vrows, pl.ds(start, span)], jnp.exp(s_loc - m).astype(BF16)))
        l = acc[HEAD_DIM:HEAD_DIM + 1] + jnp.exp(sink - m)
        _store_heads(o_ref, z_ref, g, acc[:HEAD_DIM] / l)


def _attention_d(sink, qt, k, vt, z, ctx_len):
    nb, t, _ = k.shape
    tq = TOKEN_TILE
    return pl.pallas_call(
        functools.partial(_attn_d_kernel, ctx_len=ctx_len),
        grid=(nb, t // tq),
        in_specs=[pl.BlockSpec(memory_space=pltpu.SMEM),
                  pl.BlockSpec((1, BRANCH_W, tq), lambda b, j: (b, 0, j)),
                  pl.BlockSpec((1, t, KV_W), lambda b, j: (b, 0, 0)),
                  pl.BlockSpec((1, 2 * V_ROWS, t), lambda b, j: (b, 0, 0)),
                  pl.BlockSpec((1, tq, BRANCH_W), lambda b, j: (b, j, 0))],
        out_specs=pl.BlockSpec((1, tq, BRANCH_W), lambda b, j: (b, j, 0)),
        out_shape=jax.ShapeDtypeStruct((nb, t, BRANCH_W), BF16),
        compiler_params=pltpu.CompilerParams(vmem_limit_bytes=VMEM_LIMIT),
        name="attention_d",
    )(sink, qt, k, vt, z)


def _dft_tables(seq_len, ctx_len):
    n2c = DFT_INNER
    n1c = seq_len // n2c
    scale = 1.0 / np.sqrt(float(seq_len) * GROUP_W)
    k1 = np.arange(n1c)[:, None]
    n1 = np.arange(n1c)[None, :]
    m1 = np.zeros((n2c, 2 * n1c, n1c), np.float64)
    for n2 in range(n2c):
        ang = -2.0 * np.pi * (k1 * n1 / n1c + n2 * k1 / seq_len)
        m1[n2, :n1c] = np.cos(ang) * scale
        m1[n2, n1c:] = np.sin(ang) * scale
    ang3 = 2.0 * np.pi * np.outer(np.arange(n2c), np.arange(n2c)) / n2c
    c3, s3 = np.cos(ang3), np.sin(ang3)
    m3 = np.block([[c3, s3], [-s3, c3]])
    cscale = 1.0 / np.sqrt(float(ctx_len) * GROUP_W)
    angc = 2.0 * np.pi * np.outer(np.arange(ctx_len), np.arange(ctx_len)) / ctx_len
    mc = np.concatenate([np.cos(angc), -np.sin(angc)], axis=0) * cscale
    angg = 2.0 * np.pi * np.outer(np.arange(GROUP_W), np.arange(GROUP_W)) / GROUP_W
    eye = np.eye(BRANCH_W // GROUP_W)
    chan = np.concatenate([np.kron(eye, np.cos(angg)), np.kron(eye, np.sin(angg))], axis=0)
    return _split(m1), _split(m3), _split(mc), _split(chan)


def _split(a):
    a32 = jnp.asarray(a.astype(np.float32))
    hi = a32.astype(BF16)
    return hi, (a32 - hi.astype(F32)).astype(BF16)


def _dot3_left(m_hi, m_lo, x):
    x_hi = x.astype(BF16)
    x_lo = (x - x_hi.astype(F32)).astype(BF16)
    return _dot(m_hi, x_hi) + _dot(m_lo, x_hi) + _dot(m_hi, x_lo)


def _dot3_right(x, m_hi, m_lo):
    x_hi = x.astype(BF16)
    x_lo = (x - x_hi.astype(F32)).astype(BF16)
    return _dot(x_hi, m_hi) + _dot(x_hi, m_lo) + _dot(x_lo, m_hi)


def _fnet_stage1_kernel(f_ref, mh_ref, ml_ref, br_ref, bi_ref):
    n1c = f_ref.shape[1]
    for q in range(f_ref.shape[2]):
        res = _dot3_left(mh_ref[q], ml_ref[q], f_ref[0, :, q, :])
        br_ref[0, :, q, :] = res[:n1c]
        bi_ref[0, :, q, :] = res[n1c:]


def _fnet_stage2_kernel(br_ref, bi_ref, fc_ref, mh_ref, ml_ref, mch_ref, mcl_ref,
                        pr_ref, pi_ref, prc_ref, pic_ref):
    n2c = br_ref.shape[2]
    ctx_len = fc_ref.shape[1]

    @pl.when(pl.program_id(1) == 0)
    def _():
        res = _dot3_left(mch_ref[...], mcl_ref[...], fc_ref[0])
        prc_ref[0] = res[:ctx_len]
        pic_ref[0] = res[ctx_len:]

    for q in range(br_ref.shape[1]):
        xin = jnp.concatenate([br_ref[0, q], bi_ref[0, q]], axis=0)
        res = _dot3_left(mh_ref[...], ml_ref[...], xin)
        pr_ref[0, :, q, :] = res[:n2c]
        pi_ref[0, :, q, :] = res[n2c:]


def _fnet_dft(fb, fb_ctx, m1, m3, mc):
    nb, seq, w = fb.shape
    ctx_len = fb_ctx.shape[1]
    n2c = DFT_INNER
    n1c = seq // n2c
    sub = DFT_BLOCK
    f4 = fb.reshape(nb, n1c, n2c, w)
    blk1 = pl.BlockSpec((1, n1c, sub, w), lambda b, i: (b, 0, i, 0))
    tab1 = pl.BlockSpec((sub, 2 * n1c, n1c), lambda b, i: (i, 0, 0))
    b4 = jax.ShapeDtypeStruct((nb, n1c, n2c, w), F32)
    br, bi = pl.pallas_call(
        _fnet_stage1_kernel,
        grid=(nb, n2c // sub),
        in_specs=[blk1, tab1, tab1],
        out_specs=[blk1, blk1],
        out_shape=[b4, b4],
        compiler_params=pltpu.CompilerParams(vmem_limit_bytes=VMEM_LIMIT),
        name="fnet_stage1",
    )(f4, *m1)
    blk2 = pl.BlockSpec((1, sub, n2c, w), lambda b, i: (b, i, 0, 0))
    oblk = pl.BlockSpec((1, n2c, sub, w), lambda b, i: (b, 0, i, 0))
    cblk = pl.BlockSpec((1, ctx_len, w), lambda b, i: (b, 0, 0))
    const = lambda a: pl.BlockSpec(a.shape, lambda b, i: (0,) * a.ndim)
    p4 = jax.ShapeDtypeStruct((nb, n2c, n1c, w), F32)
    pc = jax.ShapeDtypeStruct((nb, ctx_len, w), F32)
    pr, pi, prc, pic = pl.pallas_call(
        _fnet_stage2_kernel,
        grid=(nb, n1c // sub),
        in_specs=[blk2, blk2, cblk, const(m3[0]), const(m3[1]), const(mc[0]), const(mc[1])],
        out_specs=[oblk, oblk, cblk, cblk],
        out_shape=[p4, p4, pc, pc],
        compiler_params=pltpu.CompilerParams(vmem_limit_bytes=VMEM_LIMIT),
        name="fnet_stage2",
    )(br, bi, fb_ctx, *m3, *mc)
    return pr.reshape(nb, seq, w), pi.reshape(nb, seq, w), prc, pic


def _merge_kernel(x_ref, mod_ref, nw_ref, ya_ref, yd_ref, pr_ref, pi_ref, prc_ref, pic_ref,
                  zb_ref, yc_ref, chh_ref, chl_ref, wf_ref, wbr_ref, wm_ref, bm_ref, wo_ref, o_ref,
                  *, n_batch, tile_off):
    b, j = pl.program_id(0), pl.program_id(1)
    d = x_ref.shape[-1]
    x = x_ref[0]
    is_ctx = j + tile_off == 0
    row = jnp.where(is_ctx, n_batch, b)
    h, gate = _modulated_norm(x, mod_ref, nw_ref, row, d)
    pc = jnp.concatenate([jnp.where(is_ctx, prc_ref[0], pr_ref[0]),
                          jnp.where(is_ctx, pic_ref[0], pi_ref[0])], axis=1)
    yb = _dot(_dot3_right(pc, chh_ref[...], chl_ref[...]).astype(BF16), wf_ref[...]) * zb_ref[0]
    branches = (ya_ref[0], yd_ref[0], yb.astype(BF16), yc_ref[0])
    hb = h.astype(BF16)
    acc = jnp.zeros((x.shape[0], d), F32)
    for r in range(N_BRANCH):
        g = jax.nn.sigmoid(_dot(hb, wm_ref[:, r * d:(r + 1) * d]) + bm_ref[:, r * d:(r + 1) * d])
        acc = acc + g * _dot(branches[r], wbr_ref[r])
    o_ref[0] = x + gate * _dot(acc.astype(BF16), wo_ref[...])


def _merge(x_all, mod, nw, ya, yd, pr, pi, prc, pic, zb, yc, chan, wf_bd, w_br, w_merge, b_merge,
           w_out, skip_ctx):
    nb, t, d = x_all.shape
    tm = TOKEN_TILE
    off = 1 if skip_ctx else 0
    nt = t // tm - off
    tok_spec = lambda w: pl.BlockSpec((1, tm, w), lambda b, j: (b, j + off, 0))
    lat_spec = lambda w: pl.BlockSpec((1, tm, w), lambda b, j: (b, jnp.maximum(j + off - 1, 0), 0))
    ctx_spec = lambda a: pl.BlockSpec((1,) + a.shape[1:], lambda b, j: (b, 0, 0))
    const = lambda a: pl.BlockSpec(a.shape, lambda b, j: (0,) * a.ndim)
    return pl.pallas_call(
        functools.partial(_merge_kernel, n_batch=nb, tile_off=off),
        grid=(nb, nt),
        in_specs=[tok_spec(d), const(mod), const(nw), tok_spec(BRANCH_W), tok_spec(BRANCH_W),
                  lat_spec(BRANCH_W), lat_spec(BRANCH_W), ctx_spec(prc), ctx_spec(pic),
                  tok_spec(BRANCH_W), tok_spec(BRANCH_W), const(chan[0]), const(chan[1]),
                  const(wf_bd), const(w_br), const(w_merge), const(b_merge), const(w_out)],
        out_specs=pl.BlockSpec((1, tm, d), lambda b, j: (b, j, 0)),
        out_shape=jax.ShapeDtypeStruct((nb, nt * tm, d), F32),
        compiler_params=pltpu.CompilerParams(vmem_limit_bytes=VMEM_LIMIT),
        name="merge",
    )(x_all, mod, nw, ya, yd, pr, pi, prc, pic, zb, yc, *chan, wf_bd, w_br, w_merge, b_merge,
      w_out)


def _rope_tables(seq_len, ctx_len):
    t = jnp.arange(seq_len, dtype=jnp.int32)
    r = (t // GRID_W).astype(F32)
    col = (t % GRID_W).astype(F32)
    nf = HEAD_DIM // 4
    inv = ROPE_BASE ** (-jnp.arange(nf, dtype=F32) / nf)
    ar = r[:, None] * inv[None, :]
    ac = col[:, None] * inv[None, :]
    cr, sr, cc, sc = jnp.cos(ar), jnp.sin(ar), jnp.cos(ac), jnp.sin(ac)
    cos_h = jnp.concatenate([cr, cr, cc, cc], axis=1)
    sin_h = jnp.concatenate([-sr, sr, -sc, sc], axis=1)
    cos_t = jnp.concatenate([jnp.ones((ctx_len, HEAD_DIM), F32), cos_h], axis=0)
    sin_t = jnp.concatenate([jnp.zeros((ctx_len, HEAD_DIM), F32), sin_h], axis=0)
    return jnp.tile(cos_t, (1, 2)), jnp.tile(sin_t, (1, 2))


def kernel(x, c, ctx, c_ctx, norm_w, w_ada, b_ada, w_in, qn_a, kn_a, qn_d, kn_d, sink_d,
           w_fnet, w_sp, b_sp, w_br, w_merge, b_merge, w_out):
    nb, seq, d = x.shape
    ctx_len = ctx.shape[1]
    depth = norm_w.shape[0]
    assert ctx_len == TOKEN_TILE and seq % (2 * ATTN_A_KEYS) == 0 and nb < 8

    cvecs = jnp.zeros((8, d), F32).at[:nb].set(c).at[nb].set(c_ctx)
    mods = _modulation(cvecs, w_ada, b_ada)

    cos_t, sin_t = _rope_tables(seq, ctx_len)
    m1, m3, mc, chan = _dft_tables(seq, ctx_len)
    ones_bd = jnp.asarray(np.kron(np.eye(2), np.full((HEAD_DIM, HEAD_DIM), 1.0 / HEAD_DIM)), BF16)
    pair = lambda w: jnp.tile(w, 2).reshape(1, LANES)
    eye_g = jnp.eye(BRANCH_W // GROUP_W, dtype=F32)

    x_all = jnp.concatenate([ctx, x], axis=1)
    for l in range(depth):
        nw = norm_w[l].reshape(1, d)
        wf_bd = jnp.einsum('gh,gcd->gchd', eye_g, w_fnet[l]).reshape(BRANCH_W, BRANCH_W)
        b_sp_t = jnp.repeat(b_sp[l].T, GROUP_W, axis=1)
        (qat, ka, vat, za, qdt, kd, vdt, zd, fb, fbc, zb, yc) = _projection(
            x_all, mods[l], nw, w_in[l].astype(BF16), ones_bd, cos_t, sin_t,
            pair(qn_a[l]), pair(kn_a[l]), pair(qn_d[l]), pair(kn_d[l]),
            w_sp[l].astype(BF16), b_sp_t)
        ya = _attention_a(qat, ka, vat, za, ctx_len)
        yd = _attention_d(sink_d[l], qdt, kd, vdt, zd, ctx_len)
        pr, pi, prc, pic = _fnet_dft(fb, fbc, m1, m3, mc)
        x_all = _merge(x_all, mods[l], nw, ya, yd, pr, pi, prc, pic, zb, yc, chan,
                       wf_bd.astype(BF16),
                       w_br[l].astype(BF16), w_merge[l].astype(BF16),
                       b_merge[l].reshape(1, -1), w_out[l].astype(BF16),
                       skip_ctx=(l == depth - 1))
    return x_all
```

```python
import functools

import numpy as np
import jax
import jax.numpy as jnp
from jax import lax
from jax.experimental import pallas as pl
from jax.experimental.pallas import tpu as pltpu

F32 = jnp.float32
BF16 = jnp.bfloat16

GRID_W = 64
HEAD_DIM = 64
BRANCH_W = 256
KV_W = 128
N_BRANCH = 4
GROUP_W = 64
CHUNK = 128
Q_BLOCK = 128
WINDOW = 128
ROPE_BASE = 10000.0
EPS = 1e-6
LOG2_E = float(np.log2(np.e))
LANES = 128
TOKEN_TILE = 256
V_ROWS = 80
ATTN_A_KEYS = 512
ATTN_A_UNROLL = 16
DFT_BLOCK = 8
DFT_INNER = 64
VMEM_LIMIT = 56 * 1024 * 1024

_IN_SIZES = (BRANCH_W, KV_W, KV_W, BRANCH_W, BRANCH_W, KV_W, KV_W, BRANCH_W,
             BRANCH_W, BRANCH_W, BRANCH_W, BRANCH_W, BRANCH_W)
_IN_OFF = tuple(int(v) for v in np.cumsum((0,) + _IN_SIZES))
IN_W = _IN_OFF[-1]
(_AQ, _AK, _AV, _AZ, _DQ, _DK, _DV, _DZ, _BF, _BZ, _CU, _CV, _CZ) = _IN_OFF[:-1]


def _silu(z):
    return z * jax.nn.sigmoid(z)


def _gelu(x):
    return 0.5 * x * (1.0 + lax.erf(x * np.float32(np.sqrt(0.5))))


def _dot(a, b):
    return jnp.dot(a, b, preferred_element_type=F32)


def _lane_lo(width=LANES):
    lane = lax.broadcasted_iota(jnp.int32, (1, width), 1)
    return (lane % LANES) < HEAD_DIM


def _mod_kernel(cv_ref, w_ref, b_ref, o_ref):
    s = _silu(cv_ref[...])
    o_ref[0] = _dot(s.astype(BF16), w_ref[0].astype(BF16)) + b_ref[0]


def _modulation(cvecs, w_ada, b_ada):
    depth, d, d3 = w_ada.shape
    nblk = d3 // d
    return pl.pallas_call(
        _mod_kernel,
        grid=(depth, nblk),
        in_specs=[pl.BlockSpec((8, d), lambda l, n: (0, 0)),
                  pl.BlockSpec((1, d, d), lambda l, n: (l, 0, n)),
                  pl.BlockSpec((1, 1, d), lambda l, n: (l, 0, n))],
        out_specs=pl.BlockSpec((1, 8, d), lambda l, n: (l, 0, n)),
        out_shape=jax.ShapeDtypeStruct((depth, 8, d3), F32),
        name="modulation",
    )(cvecs, w_ada, b_ada.reshape(depth, 1, d3))


def _modulated_norm(x, mod_ref, nw_ref, row, d):
    m = mod_ref[pl.ds(row, 1), :]
    sh, sc = m[:, :d], m[:, d:2 * d]
    ms = jnp.mean(x * x, axis=-1, keepdims=True)
    xn = x * lax.rsqrt(ms + EPS) * nw_ref[...]
    return xn * (1.0 + sc) + sh, m[:, 2 * d:]


def _head_norm_rope(xs, ones_ref, wn, cos, sin, scale):
    sq = xs * xs
    hi = sq.astype(BF16)
    lo = (sq - hi.astype(F32)).astype(BF16)
    ms = _dot(hi, ones_ref[...]) + _dot(lo, ones_ref[...])
    y = xs * lax.rsqrt(ms + EPS) * wn
    lane = lax.broadcasted_iota(jnp.int32, (1, LANES), 1)
    first = (lane % 32) < 16
    sw = jnp.where(first, pltpu.roll(y, LANES - 16, axis=1), pltpu.roll(y, 16, axis=1))
    y = y * cos + sw * sin
    return y * scale if scale != 1.0 else y


def _proj_kernel(x_ref, mod_ref, nw_ref, win_ref, ones_ref, cos_ref, sin_ref,
                 qna_ref, kna_ref, qnd_ref, knd_ref, wsp_ref, bsp_ref,
                 qat_ref, ka_ref, vat_ref, za_ref, qdt_ref, kd_ref, vdt_ref, zd_ref,
                 fb_ref, fbc_ref, zb_ref, yc_ref, *, n_batch):
    b, j = pl.program_id(0), pl.program_id(1)
    d = x_ref.shape[-1]
    row = jnp.where(j == 0, n_batch, b)
    h, _ = _modulated_norm(x_ref[0], mod_ref, nw_ref, row, d)
    p = _dot(h.astype(BF16), win_ref[...])
    cos, sin = cos_ref[...], sin_ref[...]
    q_scale = HEAD_DIM ** -0.5 * LOG2_E

    def attn_branch(q0, k0, v0, z0, qn_ref, kn_ref, qt_ref, k_ref, vt_ref, z_ref):
        q = [_head_norm_rope(p[:, q0 + s * LANES:q0 + (s + 1) * LANES], ones_ref, qn_ref[...],
                             cos, sin, q_scale) for s in range(BRANCH_W // LANES)]
        qt_ref[0] = jnp.concatenate(q, axis=1).T.astype(BF16)
        k_ref[0] = _head_norm_rope(p[:, k0:k0 + KV_W], ones_ref, kn_ref[...], cos, sin,
                                   1.0).astype(BF16)
        vt = p[:, v0:v0 + KV_W].T
        ones = jnp.ones((V_ROWS - HEAD_DIM, vt.shape[1]), F32)
        vt_ref[0] = jnp.concatenate([vt[:HEAD_DIM], ones, vt[HEAD_DIM:], ones],
                                    axis=0).astype(BF16)
        z_ref[0] = _silu(p[:, z0:z0 + BRANCH_W])

    attn_branch(_AQ, _AK, _AV, _AZ, qna_ref, kna_ref, qat_ref, ka_ref, vat_ref, za_ref)
    attn_branch(_DQ, _DK, _DV, _DZ, qnd_ref, knd_ref, qdt_ref, kd_ref, vdt_ref, zd_ref)

    f = p[:, _BF:_BF + BRANCH_W]
    sub = DFT_BLOCK
    for hi in range(DFT_INNER // sub):
        rows = jnp.concatenate([f[n1 * DFT_INNER + hi * sub:n1 * DFT_INNER + (hi + 1) * sub]
                                for n1 in range(f.shape[0] // DFT_INNER)], axis=0)
        for half in range(BRANCH_W // LANES):
            fb_ref[0, half, hi] = rows[:, half * LANES:(half + 1) * LANES]

    @pl.when(j == 0)
    def _():
        fbc_ref[0] = f

    zb_ref[0] = _silu(p[:, _BZ:_BZ + BRANCH_W])

    u = _gelu(p[:, _CU:_CU + BRANCH_W])
    v = _gelu(p[:, _CV:_CV + BRANCH_W]).astype(BF16)
    zc = _silu(p[:, _CZ:_CZ + BRANCH_W])
    group = lax.broadcasted_iota(jnp.int32, (1, BRANCH_W), 1) // GROUP_W
    for c in range(x_ref.shape[1] // CHUNK):
        rows = slice(c * CHUNK, (c + 1) * CHUNK)
        sp = bsp_ref[...]
        for g in range(BRANCH_W // GROUP_W):
            sp = sp + jnp.where(group == g, _dot(wsp_ref[g], v[rows]), 0.0)
        yc_ref[0, rows, :] = (u[rows] * sp * zc[rows]).astype(BF16)


def _projection(x_all, mod, nw, w_in, ones_bd, cos_t, sin_t, qna, kna, qnd, knd, w_sp, b_sp_t):
    nb, t, d = x_all.shape
    tm = TOKEN_TILE
    nt = t // tm
    tok = lambda w, dt: jax.ShapeDtypeStruct((nb, t, w), dt)
    tok_spec = lambda w: pl.BlockSpec((1, tm, w), lambda b, j: (b, j, 0))
    tr = lambda w: jax.ShapeDtypeStruct((nb, w, t), BF16)
    tr_spec = lambda w: pl.BlockSpec((1, w, tm), lambda b, j: (b, 0, j))
    const = lambda a: pl.BlockSpec(a.shape, lambda b, j: (0,) * a.ndim)
    attn_specs = [tr_spec(BRANCH_W), tok_spec(KV_W), tr_spec(2 * V_ROWS), tok_spec(BRANCH_W)]
    attn_shapes = [tr(BRANCH_W), tok(KV_W, BF16), tr(2 * V_ROWS), tok(BRANCH_W, F32)]
    n_hi = DFT_INNER // DFT_BLOCK
    return pl.pallas_call(
        functools.partial(_proj_kernel, n_batch=nb),
        grid=(nb, nt),
        in_specs=[tok_spec(d), const(mod), const(nw), const(w_in), const(ones_bd),
                  pl.BlockSpec((tm, LANES), lambda b, j: (j, 0)),
                  pl.BlockSpec((tm, LANES), lambda b, j: (j, 0)),
                  const(qna), const(kna), const(qnd), const(knd), const(w_sp), const(b_sp_t)],
        out_specs=attn_specs * 2 + [
            pl.BlockSpec((1, BRANCH_W // LANES, n_hi, tm // n_hi, LANES),
                         lambda b, j: (b, 0, 0, jnp.maximum(j - 1, 0), 0)),
            pl.BlockSpec((1, tm, BRANCH_W), lambda b, j: (b, 0, 0)),
            tok_spec(BRANCH_W), tok_spec(BRANCH_W)],
        out_shape=attn_shapes * 2 + [
            jax.ShapeDtypeStruct((nb, BRANCH_W // LANES, n_hi, (t - tm) // n_hi, LANES), F32),
            jax.ShapeDtypeStruct((nb, tm, BRANCH_W), F32),
            tok(BRANCH_W, F32), tok(BRANCH_W, BF16)],
        compiler_params=pltpu.CompilerParams(vmem_limit_bytes=VMEM_LIMIT),
        name="projection",
    )(x_all, mod, nw, w_in, ones_bd, cos_t, sin_t, qna, kna, qnd, knd, w_sp, b_sp_t)


def _group_queries(qt_ref, g):
    r0 = g * LANES
    qrow = jnp.concatenate([qt_ref[0, r0:r0 + HEAD_DIM, :], qt_ref[0, r0 + HEAD_DIM:r0 + LANES, :]],
                           axis=1)
    zeros = jnp.zeros_like(qrow)
    return jnp.concatenate([qrow, zeros] if g == 0 else [zeros, qrow], axis=0)


def _store_heads(o_ref, z_ref, g, o_t):
    tq = o_t.shape[1] // 2
    o = jnp.concatenate([o_t[:, :tq], o_t[:, tq:]], axis=0).T
    cols = slice(g * LANES, (g + 1) * LANES)
    o_ref[0, :, cols] = (o * z_ref[0, :, cols]).astype(BF16)


def _attn_a_kernel(qt_ref, k_ref, vt_ref, z_ref, o_ref, s_ref, smax_ref, m_ref, acc_ref,
                   *, tk, ctx_len):
    j = pl.program_id(1)
    tq = qt_ref.shape[2]
    t = k_ref.shape[1]
    n_lat = (t - ctx_len) // tk
    groups = range(2)
    qt_g = [_group_queries(qt_ref, g) for g in groups]

    def scores(g, start, size):
        return _dot(k_ref[0, pl.ds(start, size), :], qt_g[g])

    def values(g, start, size):
        return vt_ref[0, g * V_ROWS:(g + 1) * V_ROWS, pl.ds(start, size)]

    def lat_start(c):
        return pl.multiple_of(ctx_len + c * tk, LANES)

    def produce(g, slot, c):
        s = scores(g, lat_start(c), tk)
        smax_ref[slot, g] = jnp.max(s, axis=0, keepdims=True)
        s_ref[slot, g] = s

    s_ctx = [scores(g, 0, ctx_len) for g in groups]
    for g in groups:
        produce(g, 0, 0)
    for g in groups:
        m = jnp.max(s_ctx[g], axis=0, keepdims=True)
        m_ref[g] = m
        acc_ref[g] = _dot(values(g, 0, ctx_len), jnp.exp2(s_ctx[g] - m).astype(BF16))

    def consume(g, slot, c):
        m = m_ref[g]
        m_new = jnp.maximum(m, smax_ref[slot, g])
        pr = jnp.exp2(s_ref[slot, g] - m_new).astype(BF16)
        acc_ref[g] = jnp.exp2(m - m_new) * acc_ref[g] + _dot(values(g, lat_start(c), tk), pr)
        m_ref[g] = m_new

    @pl.when(j > 0)
    def _():
        unroll = ATTN_A_UNROLL

        def body(i, carry):
            for k in range(unroll):
                c = unroll * i + k
                for g in groups:
                    produce(g, (k + 1) % 2, c + 1)
                    consume(g, k % 2, c)
            return carry

        n_loop = (n_lat - 1) // unroll
        if n_loop:
            lax.fori_loop(0, n_loop, body, 0)
        for c in range(n_loop * unroll, n_lat):
            for g in groups:
                if c + 1 < n_lat:
                    produce(g, (c + 1) % 2, c + 1)
                consume(g, c % 2, c)

    for g in groups:
        acc = acc_ref[g]
        _store_heads(o_ref, z_ref, g, acc[:HEAD_DIM] / acc[HEAD_DIM:HEAD_DIM + 1])


def _attention_a(qt, k, vt, z, ctx_len):
    nb, t, _ = k.shape
    tq = TOKEN_TILE
    tk = ATTN_A_KEYS
    return pl.pallas_call(
        functools.partial(_attn_a_kernel, tk=tk, ctx_len=ctx_len),
        grid=(nb, t // tq),
        in_specs=[pl.BlockSpec((1, BRANCH_W, tq), lambda b, j: (b, 0, j)),
                  pl.BlockSpec((1, t, KV_W), lambda b, j: (b, 0, 0)),
                  pl.BlockSpec((1, 2 * V_ROWS, t), lambda b, j: (b, 0, 0)),
                  pl.BlockSpec((1, tq, BRANCH_W), lambda b, j: (b, j, 0))],
        out_specs=pl.BlockSpec((1, tq, BRANCH_W), lambda b, j: (b, j, 0)),
        out_shape=jax.ShapeDtypeStruct((nb, t, BRANCH_W), BF16),
        scratch_shapes=[pltpu.VMEM((2, 2, tk, 2 * tq), F32),
                        pltpu.VMEM((2, 2, 1, 2 * tq), F32),
                        pltpu.VMEM((2, 1, 2 * tq), F32),
                        pltpu.VMEM((2, V_ROWS, 2 * tq), F32)],
        compiler_params=pltpu.CompilerParams(vmem_limit_bytes=VMEM_LIMIT),
        name="attention_a",
    )(qt, k, vt, z)


def _attn_d_kernel(sink_ref, qt_ref, k_ref, vt_ref, z_ref, o_ref, *, ctx_len):
    j = pl.program_id(1)
    tq = qt_ref.shape[2]
    t = k_ref.shape[1]
    span = tq + 2 * WINDOW
    q0 = j * tq
    start = pl.multiple_of(jnp.clip(q0 - WINDOW, ctx_len, t - span), LANES)
    kpos = start + lax.broadcasted_iota(jnp.int32, (span, 2 * tq), 0)
    qpos = q0 + lax.broadcasted_iota(jnp.int32, (span, 2 * tq), 1) % tq
    mask = (jnp.abs(kpos - qpos) <= WINDOW) & (j > 0)
    first = lax.broadcasted_iota(jnp.int32, (1, 2 * tq), 1) < tq
    scores = []
    for g in range(2):
        qt_g = _group_queries(qt_ref, g)
        scores.append((_dot(k_ref[0, :ctx_len, :], qt_g),
                       _dot(k_ref[0, pl.ds(start, span), :], qt_g)))
    for g in range(2):
        vrows = slice(g * V_ROWS, (g + 1) * V_ROWS)
        s_ctx = scores[g][0]
        s_loc = jnp.where(mask, scores[g][1], -jnp.inf)
        sink = jnp.where(first, sink_ref[2 * g], sink_ref[2 * g + 1]) * LOG2_E
        m = jnp.maximum(jnp.maximum(jnp.max(s_ctx, axis=0, keepdims=True),
                                    jnp.max(s_loc, axis=0, keepdims=True)), sink)
        acc = (_dot(vt_ref[0, vrows, :ctx_len], jnp.exp2(s_ctx - m).astype(BF16))
               + _dot(vt_ref[0, vrows, pl.ds(start, span)], jnp.exp2(s_loc - m).astype(BF16)))
        l = acc[HEAD_DIM:HEAD_DIM + 1] + jnp.exp2(sink - m)
        _store_heads(o_ref, z_ref, g, acc[:HEAD_DIM] / l)


def _attention_d(sink, qt, k, vt, z, ctx_len):
    nb, t, _ = k.shape
    tq = TOKEN_TILE
    return pl.pallas_call(
        functools.partial(_attn_d_kernel, ctx_len=ctx_len),
        grid=(nb, t // tq),
        in_specs=[pl.BlockSpec(memory_space=pltpu.SMEM),
                  pl.BlockSpec((1, BRANCH_W, tq), lambda b, j: (b, 0, j)),
                  pl.BlockSpec((1, t, KV_W), lambda b, j: (b, 0, 0)),
                  pl.BlockSpec((1, 2 * V_ROWS, t), lambda b, j: (b, 0, 0)),
                  pl.BlockSpec((1, tq, BRANCH_W), lambda b, j: (b, j, 0))],
        out_specs=pl.BlockSpec((1, tq, BRANCH_W), lambda b, j: (b, j, 0)),
        out_shape=jax.ShapeDtypeStruct((nb, t, BRANCH_W), BF16),
        compiler_params=pltpu.CompilerParams(vmem_limit_bytes=VMEM_LIMIT),
        name="attention_d",
    )(sink, qt, k, vt, z)


def _dft_tables(seq_len, ctx_len):
    n2c = DFT_INNER
    n1c = seq_len // n2c
    scale = 1.0 / np.sqrt(float(seq_len) * GROUP_W)
    k1 = np.arange(n1c)[:, None]
    n1 = np.arange(n1c)[None, :]
    m1 = np.zeros((n2c, 2 * n1c, n1c), np.float64)
    for n2 in range(n2c):
        ang = -2.0 * np.pi * (k1 * n1 / n1c + n2 * k1 / seq_len)
        m1[n2, :n1c] = np.cos(ang) * scale
        m1[n2, n1c:] = np.sin(ang) * scale
    ang3 = 2.0 * np.pi * np.outer(np.arange(n2c), np.arange(n2c)) / n2c
    c3, s3 = np.cos(ang3), np.sin(ang3)
    m3 = np.block([[c3, s3], [-s3, c3]])
    cscale = 1.0 / np.sqrt(float(ctx_len) * GROUP_W)
    angc = 2.0 * np.pi * np.outer(np.arange(ctx_len), np.arange(ctx_len)) / ctx_len
    mc = np.concatenate([np.cos(angc), -np.sin(angc)], axis=0) * cscale
    angg = 2.0 * np.pi * np.outer(np.arange(GROUP_W), np.arange(GROUP_W)) / GROUP_W
    eye = np.eye(BRANCH_W // GROUP_W)
    chan = np.concatenate([np.kron(eye, np.cos(angg)), np.kron(eye, np.sin(angg))], axis=0)
    return _split(m1), _split(m3), _split(mc), _split(chan)


def _split(a):
    a32 = jnp.asarray(a.astype(np.float32))
    hi = a32.astype(BF16)
    return hi, (a32 - hi.astype(F32)).astype(BF16)


def _dot3_left(m_hi, m_lo, x):
    x_hi = x.astype(BF16)
    x_lo = (x - x_hi.astype(F32)).astype(BF16)
    return _dot(m_hi, x_hi) + _dot(m_lo, x_hi) + _dot(m_hi, x_lo)


def _dot3_right(x, m_hi, m_lo):
    x_hi = x.astype(BF16)
    x_lo = (x - x_hi.astype(F32)).astype(BF16)
    return _dot(x_hi, m_hi) + _dot(x_hi, m_lo) + _dot(x_lo, m_hi)


def _fnet_stage1_kernel(f_ref, mh_ref, ml_ref, br_ref, bi_ref):
    sub = DFT_BLOCK
    halves = range(f_ref.shape[1])
    n1c = f_ref.shape[3] // sub
    for q in range(sub):
        rows = pl.ds(q, n1c, stride=sub)
        x = jnp.concatenate([f_ref[0, h, 0, rows, :] for h in halves], axis=1)
        res = _dot3_left(mh_ref[q], ml_ref[q], x)
        for h in halves:
            br_ref[0, h, 0, rows, :] = res[:n1c, h * LANES:(h + 1) * LANES]
            bi_ref[0, h, 0, rows, :] = res[n1c:, h * LANES:(h + 1) * LANES]


def _fnet_stage2_kernel(br_ref, bi_ref, fc_ref, mh_ref, ml_ref, mch_ref, mcl_ref,
                        pr_ref, pi_ref, prc_ref, pic_ref):
    n2c = DFT_INNER
    sub = DFT_BLOCK
    ctx_len = fc_ref.shape[1]

    @pl.when(pl.program_id(1) == 0)
    def _():
        res = _dot3_left(mch_ref[...], mcl_ref[...], fc_ref[0])
        prc_ref[0] = res[:ctx_len]
        pic_ref[0] = res[ctx_len:]

    halves = range(br_ref.shape[1])
    for q in range(sub):
        xin = jnp.concatenate(
            [jnp.concatenate([ref[0, h, hi, q * sub:(q + 1) * sub, :] for h in halves], axis=1)
             for ref in (br_ref, bi_ref) for hi in range(n2c // sub)], axis=0)
        res = _dot3_left(mh_ref[...], ml_ref[...], xin)
        rows = pl.ds(q, n2c, stride=sub)
        for h in halves:
            pr_ref[0, h, 0, rows, :] = res[:n2c, h * LANES:(h + 1) * LANES]
            pi_ref[0, h, 0, rows, :] = res[n2c:, h * LANES:(h + 1) * LANES]


def _fnet_dft(fb, fb_ctx, m1, m3, mc):
    nb, nh, n_hi, rows, w = fb.shape
    ctx_len = fb_ctx.shape[1]
    n2c = DFT_INNER
    sub = DFT_BLOCK
    n1c = rows // sub
    blk1 = pl.BlockSpec((1, nh, 1, rows, w), lambda b, i: (b, 0, i, 0, 0))
    tab1 = pl.BlockSpec((sub, 2 * n1c, n1c), lambda b, i: (i, 0, 0))
    b4 = jax.ShapeDtypeStruct((nb, nh, n_hi, rows, w), F32)
    br, bi = pl.pallas_call(
        _fnet_stage1_kernel,
        grid=(nb, n_hi),
        in_specs=[blk1, tab1, tab1],
        out_specs=[blk1, blk1],
        out_shape=[b4, b4],
        compiler_params=pltpu.CompilerParams(vmem_limit_bytes=VMEM_LIMIT),
        name="fnet_stage1",
    )(fb, *m1)
    blk2 = pl.BlockSpec((1, nh, n_hi, sub * sub, w), lambda b, i: (b, 0, 0, i, 0))
    oblk = pl.BlockSpec((1, nh, 1, n2c * sub, w), lambda b, i: (b, 0, i, 0, 0))
    cblk = pl.BlockSpec((1, ctx_len, nh * w), lambda b, i: (b, 0, 0))
    const = lambda a: pl.BlockSpec(a.shape, lambda b, i: (0,) * a.ndim)
    p4 = jax.ShapeDtypeStruct((nb, nh, n1c // sub, n2c * sub, w), F32)
    pc = jax.ShapeDtypeStruct((nb, ctx_len, nh * w), F32)
    return pl.pallas_call(
        _fnet_stage2_kernel,
        grid=(nb, n1c // sub),
        in_specs=[blk2, blk2, cblk, const(m3[0]), const(m3[1]), const(mc[0]), const(mc[1])],
        out_specs=[oblk, oblk, cblk, cblk],
        out_shape=[p4, p4, pc, pc],
        compiler_params=pltpu.CompilerParams(vmem_limit_bytes=VMEM_LIMIT),
        name="fnet_stage2",
    )(br, bi, fb_ctx, *m3, *mc)


def _merge_kernel(x_ref, mod_ref, nw_ref, ya_ref, yd_ref, pr_ref, pi_ref, prc_ref, pic_ref,
                  zb_ref, yc_ref, chh_ref, chl_ref, wf_ref, wbr_ref, wm_ref, bm_ref, wo_ref, o_ref,
                  *, n_batch, tile_off):
    b, j = pl.program_id(0), pl.program_id(1)
    d = x_ref.shape[-1]
    x = x_ref[0]
    is_ctx = j + tile_off == 0
    row = jnp.where(is_ctx, n_batch, b)
    h, gate = _modulated_norm(x, mod_ref, nw_ref, row, d)
    def latent_rows(ref):
        sub = DFT_BLOCK
        return jnp.concatenate(
            [jnp.concatenate([ref[0, h, kb, k2 * sub:(k2 + 1) * sub, :]
                              for k2 in range(ref.shape[3] // sub)
                              for kb in range(ref.shape[2])], axis=0)
             for h in range(ref.shape[1])], axis=1)

    pc = jnp.concatenate([jnp.where(is_ctx, prc_ref[0], latent_rows(pr_ref)),
                          jnp.where(is_ctx, pic_ref[0], latent_rows(pi_ref))], axis=1)
    yb = _dot(_dot3_right(pc, chh_ref[...], chl_ref[...]).astype(BF16), wf_ref[...]) * zb_ref[0]
    branches = (ya_ref[0], yd_ref[0], yb.astype(BF16), yc_ref[0])
    hb = h.astype(BF16)
    acc = jnp.zeros((x.shape[0], d), F32)
    for r in range(N_BRANCH):
        g = jax.nn.sigmoid(_dot(hb, wm_ref[:, r * d:(r + 1) * d]) + bm_ref[:, r * d:(r + 1) * d])
        acc = acc + g * _dot(branches[r], wbr_ref[r])
    o_ref[0] = x + gate * _dot(acc.astype(BF16), wo_ref[...])


def _merge(x_all, mod, nw, ya, yd, pr, pi, prc, pic, zb, yc, chan, wf_bd, w_br, w_merge, b_merge,
           w_out, skip_ctx):
    nb, t, d = x_all.shape
    tm = TOKEN_TILE
    off = 1 if skip_ctx else 0
    nt = t // tm - off
    tok_spec = lambda w: pl.BlockSpec((1, tm, w), lambda b, j: (b, j + off, 0))
    lat_spec = lambda a: pl.BlockSpec((1, a.shape[1], a.shape[2], tm // a.shape[2], a.shape[4]),
                                      lambda b, j: (b, 0, 0, jnp.maximum(j + off - 1, 0), 0))
    ctx_spec = lambda a: pl.BlockSpec((1,) + a.shape[1:], lambda b, j: (b, 0, 0))
    const = lambda a: pl.BlockSpec(a.shape, lambda b, j: (0,) * a.ndim)
    return pl.pallas_call(
        functools.partial(_merge_kernel, n_batch=nb, tile_off=off),
        grid=(nb, nt),
        in_specs=[tok_spec(d), const(mod), const(nw), tok_spec(BRANCH_W), tok_spec(BRANCH_W),
                  lat_spec(pr), lat_spec(pi), ctx_spec(prc), ctx_spec(pic),
                  tok_spec(BRANCH_W), tok_spec(BRANCH_W), const(chan[0]), const(chan[1]),
                  const(wf_bd), const(w_br), const(w_merge), const(b_merge), const(w_out)],
        out_specs=pl.BlockSpec((1, tm, d), lambda b, j: (b, j, 0)),
        out_shape=jax.ShapeDtypeStruct((nb, nt * tm, d), F32),
        compiler_params=pltpu.CompilerParams(vmem_limit_bytes=VMEM_LIMIT),
        name="merge",
    )(x_all, mod, nw, ya, yd, pr, pi, prc, pic, zb, yc, *chan, wf_bd, w_br, w_merge, b_merge,
      w_out)


def _rope_tables(seq_len, ctx_len):
    t = jnp.arange(seq_len, dtype=jnp.int32)
    r = (t // GRID_W).astype(F32)
    col = (t % GRID_W).astype(F32)
    nf = HEAD_DIM // 4
    inv = ROPE_BASE ** (-jnp.arange(nf, dtype=F32) / nf)
    ar = r[:, None] * inv[None, :]
    ac = col[:, None] * inv[None, :]
    cr, sr, cc, sc = jnp.cos(ar), jnp.sin(ar), jnp.cos(ac), jnp.sin(ac)
    cos_h = jnp.concatenate([cr, cr, cc, cc], axis=1)
    sin_h = jnp.concatenate([-sr, sr, -sc, sc], axis=1)
    cos_t = jnp.concatenate([jnp.ones((ctx_len, HEAD_DIM), F32), cos_h], axis=0)
    sin_t = jnp.concatenate([jnp.zeros((ctx_len, HEAD_DIM), F32), sin_h], axis=0)
    return jnp.tile(cos_t, (1, 2)), jnp.tile(sin_t, (1, 2))


def kernel(x, c, ctx, c_ctx, norm_w, w_ada, b_ada, w_in, qn_a, kn_a, qn_d, kn_d, sink_d,
           w_fnet, w_sp, b_sp, w_br, w_merge, b_merge, w_out):
    nb, seq, d = x.shape
    ctx_len = ctx.shape[1]
    depth = norm_w.shape[0]
    assert ctx_len == TOKEN_TILE and seq % (2 * ATTN_A_KEYS) == 0 and nb < 8

    cvecs = jnp.zeros((8, d), F32).at[:nb].set(c).at[nb].set(c_ctx)
    mods = _modulation(cvecs, w_ada, b_ada)

    cos_t, sin_t = _rope_tables(seq, ctx_len)
    m1, m3, mc, chan = _dft_tables(seq, ctx_len)
    ones_bd = jnp.asarray(np.kron(np.eye(2), np.full((HEAD_DIM, HEAD_DIM), 1.0 / HEAD_DIM)), BF16)
    pair = lambda w: jnp.tile(w, 2).reshape(1, LANES)
    eye_g = jnp.eye(BRANCH_W // GROUP_W, dtype=F32)

    x_all = jnp.concatenate([ctx, x], axis=1)
    for l in range(depth):
        nw = norm_w[l].reshape(1, d)
        wf_bd = jnp.einsum('gh,gcd->gchd', eye_g, w_fnet[l]).reshape(BRANCH_W, BRANCH_W)
        b_sp_t = jnp.repeat(b_sp[l].T, GROUP_W, axis=1)
        (qat, ka, vat, za, qdt, kd, vdt, zd, fb, fbc, zb, yc) = _projection(
            x_all, mods[l], nw, w_in[l].astype(BF16), ones_bd, cos_t, sin_t,
            pair(qn_a[l]), pair(kn_a[l]), pair(qn_d[l]), pair(kn_d[l]),
            w_sp[l].astype(BF16), b_sp_t)
        ya = _attention_a(qat, ka, vat, za, ctx_len)
        yd = _attention_d(sink_d[l], qdt, kd, vdt, zd, ctx_len)
        pr, pi, prc, pic = _fnet_dft(fb, fbc, m1, m3, mc)
        x_all = _merge(x_all, mods[l], nw, ya, yd, pr, pi, prc, pic, zb, yc, chan,
                       wf_bd.astype(BF16),
                       w_br[l].astype(BF16), w_merge[l].astype(BF16),
                       b_merge[l].reshape(1, -1), w_out[l].astype(BF16),
                       skip_ctx=(l == depth - 1))
    return x_all
```

```python
import functools

import numpy as np
import jax
import jax.numpy as jnp
from jax import lax
from jax.experimental import pallas as pl
from jax.experimental.pallas import tpu as pltpu

F32 = jnp.float32
BF16 = jnp.bfloat16

GRID_W = 64
HEAD_DIM = 64
BRANCH_W = 256
KV_W = 128
N_BRANCH = 4
GROUP_W = 64
CHUNK = 128
Q_BLOCK = 128
WINDOW = 128
ROPE_BASE = 10000.0
EPS = 1e-6
LOG2_E = float(np.log2(np.e))
LANES = 128
TOKEN_TILE = 256
V_ROWS = 80
ATTN_A_KEYS = 512
ATTN_A_UNROLL = 16
DFT_BLOCK = 8
DFT_INNER = 64
VMEM_LIMIT = 56 * 1024 * 1024

_IN_SIZES = (BRANCH_W, KV_W, KV_W, BRANCH_W, BRANCH_W, KV_W, KV_W, BRANCH_W,
             BRANCH_W, BRANCH_W, BRANCH_W, BRANCH_W, BRANCH_W)
_IN_OFF = tuple(int(v) for v in np.cumsum((0,) + _IN_SIZES))
IN_W = _IN_OFF[-1]
(_AQ, _AK, _AV, _AZ, _DQ, _DK, _DV, _DZ, _BF, _BZ, _CU, _CV, _CZ) = _IN_OFF[:-1]


def _silu(z):
    return z * jax.nn.sigmoid(z)


def _gelu(x):
    return 0.5 * x * (1.0 + lax.erf(x * np.float32(np.sqrt(0.5))))


def _dot(a, b):
    return jnp.dot(a, b, preferred_element_type=F32)


def _lane_lo(width=LANES):
    lane = lax.broadcasted_iota(jnp.int32, (1, width), 1)
    return (lane % LANES) < HEAD_DIM


def _mod_kernel(cv_ref, w_ref, b_ref, o_ref):
    s = _silu(cv_ref[...])
    o_ref[0] = _dot(s.astype(BF16), w_ref[0].astype(BF16)) + b_ref[0]


def _modulation(cvecs, w_ada, b_ada):
    depth, d, d3 = w_ada.shape
    nblk = d3 // d
    return pl.pallas_call(
        _mod_kernel,
        grid=(depth, nblk),
        in_specs=[pl.BlockSpec((8, d), lambda l, n: (0, 0)),
                  pl.BlockSpec((1, d, d), lambda l, n: (l, 0, n)),
                  pl.BlockSpec((1, 1, d), lambda l, n: (l, 0, n))],
        out_specs=pl.BlockSpec((1, 8, d), lambda l, n: (l, 0, n)),
        out_shape=jax.ShapeDtypeStruct((depth, 8, d3), F32),
        name="modulation",
    )(cvecs, w_ada, b_ada.reshape(depth, 1, d3))


def _modulated_norm(x, mod_ref, nw_ref, row, d):
    m = mod_ref[pl.ds(row, 1), :]
    sh, sc = m[:, :d], m[:, d:2 * d]
    ms = jnp.mean(x * x, axis=-1, keepdims=True)
    xn = x * lax.rsqrt(ms + EPS) * nw_ref[...]
    return xn * (1.0 + sc) + sh, m[:, 2 * d:]


def _head_norm_rope(xs, wn, cos, sin, scale):
    sq = xs * xs
    lo = _lane_lo()
    ms = jnp.where(lo, jnp.sum(jnp.where(lo, sq, 0.0), axis=1, keepdims=True),
                   jnp.sum(jnp.where(lo, 0.0, sq), axis=1, keepdims=True)) * (1.0 / HEAD_DIM)
    y = xs * lax.rsqrt(ms + EPS) * wn
    lane = lax.broadcasted_iota(jnp.int32, (1, LANES), 1)
    first = (lane % 32) < 16
    sw = jnp.where(first, pltpu.roll(y, LANES - 16, axis=1), pltpu.roll(y, 16, axis=1))
    y = y * cos + sw * sin
    return y * scale if scale != 1.0 else y


def _proj_kernel(xc_ref, xl_ref, mod_ref, nw_ref, win_ref, cos_ref, sin_ref,
                 qna_ref, kna_ref, qnd_ref, knd_ref, wsp_ref, bsp_ref,
                 qat_ref, ka_ref, vat_ref, za_ref, qdt_ref, kd_ref, vdt_ref, zd_ref,
                 fb_ref, fbc_ref, zb_ref, yc_ref, *, n_batch):
    b, j = pl.program_id(0), pl.program_id(1)
    d = xl_ref.shape[-1]
    x = jnp.where(j == 0, xc_ref[0], xl_ref[0])
    row = jnp.where(j == 0, n_batch, b)
    h, _ = _modulated_norm(x, mod_ref, nw_ref, row, d)
    hb = h.astype(BF16)
    pc = _dot(hb, win_ref[:, _CU:])
    p = _dot(hb, win_ref[:, :_CU])

    u = _gelu(pc[:, :BRANCH_W])
    v = _gelu(pc[:, BRANCH_W:2 * BRANCH_W]).astype(BF16)
    zc = _silu(pc[:, 2 * BRANCH_W:])
    group = lax.broadcasted_iota(jnp.int32, (1, BRANCH_W), 1) // GROUP_W
    for c in range(x.shape[0] // CHUNK):
        rows = slice(c * CHUNK, (c + 1) * CHUNK)
        sp = bsp_ref[...]
        for g in range(BRANCH_W // GROUP_W):
            sp = sp + jnp.where(group == g, _dot(wsp_ref[g], v[rows]), 0.0)
        yc_ref[0, rows, :] = (u[rows] * sp * zc[rows]).astype(BF16)

    cos, sin = cos_ref[...], sin_ref[...]
    q_scale = HEAD_DIM ** -0.5 * LOG2_E

    def attn_branch(q0, k0, v0, z0, qn_ref, kn_ref, qt_ref, k_ref, vt_ref, z_ref):
        q = [_head_norm_rope(p[:, q0 + s * LANES:q0 + (s + 1) * LANES], qn_ref[...],
                             cos, sin, q_scale) for s in range(BRANCH_W // LANES)]
        qt_ref[0] = jnp.concatenate(q, axis=1).T.astype(BF16)
        k_ref[0] = _head_norm_rope(p[:, k0:k0 + KV_W], kn_ref[...], cos, sin,
                                   1.0).astype(BF16)
        vt = p[:, v0:v0 + KV_W].T
        ones = jnp.ones((V_ROWS - HEAD_DIM, vt.shape[1]), F32)
        vt_ref[0] = jnp.concatenate([vt[:HEAD_DIM], ones, vt[HEAD_DIM:], ones],
                                    axis=0).astype(BF16)
        z_ref[0] = _silu(p[:, z0:z0 + BRANCH_W])

    attn_branch(_AQ, _AK, _AV, _AZ, qna_ref, kna_ref, qat_ref, ka_ref, vat_ref, za_ref)
    attn_branch(_DQ, _DK, _DV, _DZ, qnd_ref, knd_ref, qdt_ref, kd_ref, vdt_ref, zd_ref)

    f = p[:, _BF:_BF + BRANCH_W]
    sub = DFT_BLOCK
    for hi in range(DFT_INNER // sub):
        rows = jnp.concatenate([f[n1 * DFT_INNER + hi * sub:n1 * DFT_INNER + (hi + 1) * sub]
                                for n1 in range(f.shape[0] // DFT_INNER)], axis=0)
        for half in range(BRANCH_W // LANES):
            fb_ref[0, half, hi] = rows[:, half * LANES:(half + 1) * LANES]

    @pl.when(j == 0)
    def _():
        fbc_ref[0] = f

    zb_ref[0] = _silu(p[:, _BZ:_BZ + BRANCH_W])


def _projection(xc, xl, mod, nw, w_in, cos_t, sin_t, qna, kna, qnd, knd, w_sp, b_sp_t):
    nb, seq, d = xl.shape
    tm = TOKEN_TILE
    t = seq + xc.shape[1]
    nt = t // tm
    lat_tile = lambda b, j: (b, jnp.maximum(j - 1, 0), 0)
    tok = lambda w, dt: jax.ShapeDtypeStruct((nb, t, w), dt)
    tok_spec = lambda w: pl.BlockSpec((1, tm, w), lambda b, j: (b, j, 0))
    tr = lambda w: jax.ShapeDtypeStruct((nb, w, t), BF16)
    tr_spec = lambda w: pl.BlockSpec((1, w, tm), lambda b, j: (b, 0, j))
    const = lambda a: pl.BlockSpec(a.shape, lambda b, j: (0,) * a.ndim)
    attn_specs = [tr_spec(BRANCH_W), tok_spec(KV_W), tr_spec(2 * V_ROWS), tok_spec(BRANCH_W)]
    attn_shapes = [tr(BRANCH_W), tok(KV_W, BF16), tr(2 * V_ROWS), tok(BRANCH_W, F32)]
    n_hi = DFT_INNER // DFT_BLOCK
    return pl.pallas_call(
        functools.partial(_proj_kernel, n_batch=nb),
        grid=(nb, nt),
        in_specs=[pl.BlockSpec((1, tm, d), lambda b, j: (b, 0, 0)),
                  pl.BlockSpec((1, tm, d), lat_tile), const(mod), const(nw), const(w_in),
                  pl.BlockSpec((tm, LANES), lambda b, j: (j, 0)),
                  pl.BlockSpec((tm, LANES), lambda b, j: (j, 0)),
                  const(qna), const(kna), const(qnd), const(knd), const(w_sp), const(b_sp_t)],
        out_specs=attn_specs * 2 + [
            pl.BlockSpec((1, BRANCH_W // LANES, n_hi, tm // n_hi, LANES),
                         lambda b, j: (b, 0, 0, jnp.maximum(j - 1, 0), 0)),
            pl.BlockSpec((1, tm, BRANCH_W), lambda b, j: (b, 0, 0)),
            tok_spec(BRANCH_W), tok_spec(BRANCH_W)],
        out_shape=attn_shapes * 2 + [
            jax.ShapeDtypeStruct((nb, BRANCH_W // LANES, n_hi, (t - tm) // n_hi, LANES), F32),
            jax.ShapeDtypeStruct((nb, tm, BRANCH_W), F32),
            tok(BRANCH_W, F32), tok(BRANCH_W, BF16)],
        compiler_params=pltpu.CompilerParams(vmem_limit_bytes=VMEM_LIMIT),
        name="projection",
    )(xc, xl, mod, nw, w_in, cos_t, sin_t, qna, kna, qnd, knd, w_sp, b_sp_t)


def _group_queries(qt_ref, g):
    r0 = g * LANES
    qrow = jnp.concatenate([qt_ref[0, r0:r0 + HEAD_DIM, :], qt_ref[0, r0 + HEAD_DIM:r0 + LANES, :]],
                           axis=1)
    zeros = jnp.zeros_like(qrow)
    return jnp.concatenate([qrow, zeros] if g == 0 else [zeros, qrow], axis=0)


def _store_heads(o_ref, z_ref, g, o_t):
    tq = o_t.shape[1] // 2
    o = jnp.concatenate([o_t[:, :tq], o_t[:, tq:]], axis=0).T
    cols = slice(g * LANES, (g + 1) * LANES)
    o_ref[0, :, cols] = (o * z_ref[0, :, cols]).astype(BF16)


def _attn_a_kernel(qt_ref, k_ref, vt_ref, z_ref, o_ref, s_ref, smax_ref, m_ref, acc_ref,
                   *, tk, ctx_len):
    j = pl.program_id(1)
    tq = qt_ref.shape[2]
    t = k_ref.shape[1]
    n_lat = (t - ctx_len) // tk
    groups = range(2)
    qt_g = [_group_queries(qt_ref, g) for g in groups]

    def scores(g, start, size):
        return _dot(k_ref[0, pl.ds(start, size), :], qt_g[g])

    def values(g, start, size):
        return vt_ref[0, g * V_ROWS:(g + 1) * V_ROWS, pl.ds(start, size)]

    def lat_start(c):
        return pl.multiple_of(ctx_len + c * tk, LANES)

    def produce(g, slot, c):
        s = scores(g, lat_start(c), tk)
        smax_ref[slot, g] = jnp.max(s, axis=0, keepdims=True)
        s_ref[slot, g] = s

    s_ctx = [scores(g, 0, ctx_len) for g in groups]
    for g in groups:
        produce(g, 0, 0)
    for g in groups:
        m = jnp.max(s_ctx[g], axis=0, keepdims=True)
        m_ref[g] = m
        acc_ref[g] = _dot(values(g, 0, ctx_len), jnp.exp2(s_ctx[g] - m).astype(BF16))

    def consume(g, slot, c):
        m = m_ref[g]
        m_new = jnp.maximum(m, smax_ref[slot, g])
        pr = jnp.exp2(s_ref[slot, g] - m_new).astype(BF16)
        acc_ref[g] = jnp.exp2(m - m_new) * acc_ref[g] + _dot(values(g, lat_start(c), tk), pr)
        m_ref[g] = m_new

    @pl.when(j > 0)
    def _():
        unroll = ATTN_A_UNROLL

        def body(i, carry):
            for k in range(unroll):
                c = unroll * i + k
                for g in groups:
                    produce(g, (k + 1) % 2, c + 1)
                    consume(g, k % 2, c)
            return carry

        n_loop = (n_lat - 1) // unroll
        if n_loop:
            lax.fori_loop(0, n_loop, body, 0)
        for c in range(n_loop * unroll, n_lat):
            for g in groups:
                if c + 1 < n_lat:
                    produce(g, (c + 1) % 2, c + 1)
                consume(g, c % 2, c)

    for g in groups:
        acc = acc_ref[g]
        _store_heads(o_ref, z_ref, g, acc[:HEAD_DIM] / acc[HEAD_DIM:HEAD_DIM + 1])


def _attention_a(qt, k, vt, z, ctx_len):
    nb, t, _ = k.shape
    tq = TOKEN_TILE
    tk = ATTN_A_KEYS
    return pl.pallas_call(
        functools.partial(_attn_a_kernel, tk=tk, ctx_len=ctx_len),
        grid=(nb, t // tq),
        in_specs=[pl.BlockSpec((1, BRANCH_W, tq), lambda b, j: (b, 0, j)),
                  pl.BlockSpec((1, t, KV_W), lambda b, j: (b, 0, 0)),
                  pl.BlockSpec((1, 2 * V_ROWS, t), lambda b, j: (b, 0, 0)),
                  pl.BlockSpec((1, tq, BRANCH_W), lambda b, j: (b, j, 0))],
        out_specs=pl.BlockSpec((1, tq, BRANCH_W), lambda b, j: (b, j, 0)),
        out_shape=jax.ShapeDtypeStruct((nb, t, BRANCH_W), BF16),
        scratch_shapes=[pltpu.VMEM((2, 2, tk, 2 * tq), F32),
                        pltpu.VMEM((2, 2, 1, 2 * tq), F32),
                        pltpu.VMEM((2, 1, 2 * tq), F32),
                        pltpu.VMEM((2, V_ROWS, 2 * tq), F32)],
        compiler_params=pltpu.CompilerParams(vmem_limit_bytes=VMEM_LIMIT),
        name="attention_a",
    )(qt, k, vt, z)


def _attn_d_kernel(sink_ref, qt_ref, k_ref, vt_ref, z_ref, o_ref, *, ctx_len):
    j = pl.program_id(1)
    tq = qt_ref.shape[2]
    t = k_ref.shape[1]
    span = tq + 2 * WINDOW
    q0 = j * tq
    start = pl.multiple_of(jnp.clip(q0 - WINDOW, ctx_len, t - span), LANES)
    kpos = start + lax.broadcasted_iota(jnp.int32, (span, 2 * tq), 0)
    qpos = q0 + lax.broadcasted_iota(jnp.int32, (span, 2 * tq), 1) % tq
    mask = (jnp.abs(kpos - qpos) <= WINDOW) & (j > 0)
    first = lax.broadcasted_iota(jnp.int32, (1, 2 * tq), 1) < tq
    scores = []
    for g in range(2):
        qt_g = _group_queries(qt_ref, g)
        scores.append((_dot(k_ref[0, :ctx_len, :], qt_g),
                       _dot(k_ref[0, pl.ds(start, span), :], qt_g)))
    for g in range(2):
        vrows = slice(g * V_ROWS, (g + 1) * V_ROWS)
        s_ctx = scores[g][0]
        s_loc = jnp.where(mask, scores[g][1], -jnp.inf)
        sink = jnp.where(first, sink_ref[2 * g], sink_ref[2 * g + 1]) * LOG2_E
        m = jnp.maximum(jnp.maximum(jnp.max(s_ctx, axis=0, keepdims=True),
                                    jnp.max(s_loc, axis=0, keepdims=True)), sink)
        acc = (_dot(vt_ref[0, vrows, :ctx_len], jnp.exp2(s_ctx - m).astype(BF16))
               + _dot(vt_ref[0, vrows, pl.ds(start, span)], jnp.exp2(s_loc - m).astype(BF16)))
        l = acc[HEAD_DIM:HEAD_DIM + 1] + jnp.exp2(sink - m)
        _store_heads(o_ref, z_ref, g, acc[:HEAD_DIM] / l)


def _attention_d(sink, qt, k, vt, z, ctx_len):
    nb, t, _ = k.shape
    tq = TOKEN_TILE
    return pl.pallas_call(
        functools.partial(_attn_d_kernel, ctx_len=ctx_len),
        grid=(nb, t // tq),
        in_specs=[pl.BlockSpec(memory_space=pltpu.SMEM),
                  pl.BlockSpec((1, BRANCH_W, tq), lambda b, j: (b, 0, j)),
                  pl.BlockSpec((1, t, KV_W), lambda b, j: (b, 0, 0)),
                  pl.BlockSpec((1, 2 * V_ROWS, t), lambda b, j: (b, 0, 0)),
                  pl.BlockSpec((1, tq, BRANCH_W), lambda b, j: (b, j, 0))],
        out_specs=pl.BlockSpec((1, tq, BRANCH_W), lambda b, j: (b, j, 0)),
        out_shape=jax.ShapeDtypeStruct((nb, t, BRANCH_W), BF16),
        compiler_params=pltpu.CompilerParams(vmem_limit_bytes=VMEM_LIMIT),
        name="attention_d",
    )(sink, qt, k, vt, z)


def _dft_tables(seq_len, ctx_len):
    n2c = DFT_INNER
    n1c = seq_len // n2c
    scale = 1.0 / np.sqrt(float(seq_len) * GROUP_W)
    k1 = np.arange(n1c)[:, None]
    n1 = np.arange(n1c)[None, :]
    m1 = np.zeros((n2c, 2 * n1c, n1c), np.float64)
    for n2 in range(n2c):
        ang = -2.0 * np.pi * (k1 * n1 / n1c + n2 * k1 / seq_len)
        m1[n2, :n1c] = np.cos(ang) * scale
        m1[n2, n1c:] = np.sin(ang) * scale
    ang3 = 2.0 * np.pi * np.outer(np.arange(n2c), np.arange(n2c)) / n2c
    c3, s3 = np.cos(ang3), np.sin(ang3)
    m3 = np.block([[c3, s3], [-s3, c3]])
    cscale = 1.0 / np.sqrt(float(ctx_len) * GROUP_W)
    angc = 2.0 * np.pi * np.outer(np.arange(ctx_len), np.arange(ctx_len)) / ctx_len
    mc = np.concatenate([np.cos(angc), -np.sin(angc)], axis=0) * cscale
    angg = 2.0 * np.pi * np.outer(np.arange(GROUP_W), np.arange(GROUP_W)) / GROUP_W
    eye = np.eye(BRANCH_W // GROUP_W)
    chan = np.concatenate([np.kron(eye, np.cos(angg)), np.kron(eye, np.sin(angg))], axis=0)
    return _split(m1), _split(m3), _split(mc), _split(chan)


def _split(a):
    a32 = jnp.asarray(a.astype(np.float32))
    hi = a32.astype(BF16)
    return hi, (a32 - hi.astype(F32)).astype(BF16)


def _dot3_left(m_hi, m_lo, x):
    x_hi = x.astype(BF16)
    x_lo = (x - x_hi.astype(F32)).astype(BF16)
    return _dot(m_hi, x_hi) + _dot(m_lo, x_hi) + _dot(m_hi, x_lo)


def _dot3_right(x, m_hi, m_lo):
    x_hi = x.astype(BF16)
    x_lo = (x - x_hi.astype(F32)).astype(BF16)
    return _dot(x_hi, m_hi) + _dot(x_hi, m_lo) + _dot(x_lo, m_hi)


def _fnet_stage1_kernel(f_ref, mh_ref, ml_ref, br_ref, bi_ref):
    sub = DFT_BLOCK
    halves = range(f_ref.shape[1])
    n1c = f_ref.shape[3] // sub
    for q in range(sub):
        rows = pl.ds(q, n1c, stride=sub)
        x = jnp.concatenate([f_ref[0, h, 0, rows, :] for h in halves], axis=1)
        res = _dot3_left(mh_ref[q], ml_ref[q], x)
        for h in halves:
            br_ref[0, h, 0, rows, :] = res[:n1c, h * LANES:(h + 1) * LANES]
            bi_ref[0, h, 0, rows, :] = res[n1c:, h * LANES:(h + 1) * LANES]


def _fnet_stage2_kernel(br_ref, bi_ref, fc_ref, mh_ref, ml_ref, mch_ref, mcl_ref,
                        pr_ref, pi_ref, prc_ref, pic_ref):
    n2c = DFT_INNER
    sub = DFT_BLOCK
    ctx_len = fc_ref.shape[1]

    @pl.when(pl.program_id(1) == 0)
    def _():
        res = _dot3_left(mch_ref[...], mcl_ref[...], fc_ref[0])
        prc_ref[0] = res[:ctx_len]
        pic_ref[0] = res[ctx_len:]

    halves = range(br_ref.shape[1])
    for q in range(sub):
        xin = jnp.concatenate(
            [jnp.concatenate([ref[0, h, hi, q * sub:(q + 1) * sub, :] for h in halves], axis=1)
             for ref in (br_ref, bi_ref) for hi in range(n2c // sub)], axis=0)
        res = _dot3_left(mh_ref[...], ml_ref[...], xin)
        rows = pl.ds(q, n2c, stride=sub)
        for h in halves:
            pr_ref[0, h, 0, rows, :] = res[:n2c, h * LANES:(h + 1) * LANES]
            pi_ref[0, h, 0, rows, :] = res[n2c:, h * LANES:(h + 1) * LANES]


def _fnet_dft(fb, fb_ctx, m1, m3, mc):
    nb, nh, n_hi, rows, w = fb.shape
    ctx_len = fb_ctx.shape[1]
    n2c = DFT_INNER
    sub = DFT_BLOCK
    n1c = rows // sub
    blk1 = pl.BlockSpec((1, nh, 1, rows, w), lambda b, i: (b, 0, i, 0, 0))
    tab1 = pl.BlockSpec((sub, 2 * n1c, n1c), lambda b, i: (i, 0, 0))
    b4 = jax.ShapeDtypeStruct((nb, nh, n_hi, rows, w), F32)
    br, bi = pl.pallas_call(
        _fnet_stage1_kernel,
        grid=(nb, n_hi),
        in_specs=[blk1, tab1, tab1],
        out_specs=[blk1, blk1],
        out_shape=[b4, b4],
        compiler_params=pltpu.CompilerParams(vmem_limit_bytes=VMEM_LIMIT),
        name="fnet_stage1",
    )(fb, *m1)
    blk2 = pl.BlockSpec((1, nh, n_hi, sub * sub, w), lambda b, i: (b, 0, 0, i, 0))
    oblk = pl.BlockSpec((1, nh, 1, n2c * sub, w), lambda b, i: (b, 0, i, 0, 0))
    cblk = pl.BlockSpec((1, ctx_len, nh * w), lambda b, i: (b, 0, 0))
    const = lambda a: pl.BlockSpec(a.shape, lambda b, i: (0,) * a.ndim)
    p4 = jax.ShapeDtypeStruct((nb, nh, n1c // sub, n2c * sub, w), F32)
    pc = jax.ShapeDtypeStruct((nb, ctx_len, nh * w), F32)
    return pl.pallas_call(
        _fnet_stage2_kernel,
        grid=(nb, n1c // sub),
        in_specs=[blk2, blk2, cblk, const(m3[0]), const(m3[1]), const(mc[0]), const(mc[1])],
        out_specs=[oblk, oblk, cblk, cblk],
        out_shape=[p4, p4, pc, pc],
        compiler_params=pltpu.CompilerParams(vmem_limit_bytes=VMEM_LIMIT),
        name="fnet_stage2",
    )(br, bi, fb_ctx, *m3, *mc)


def _merge_kernel(xc_ref, xl_ref, mod_ref, nw_ref, ya_ref, yd_ref, pr_ref, pi_ref, prc_ref,
                  pic_ref, zb_ref, yc_ref, chh_ref, chl_ref, wf_ref, wbr_ref, wm_ref, bm_ref,
                  wo_ref, *out_refs, n_batch, tile_off):
    b, j = pl.program_id(0), pl.program_id(1)
    d = xl_ref.shape[-1]
    is_ctx = j + tile_off == 0
    x = jnp.where(is_ctx, xc_ref[0], xl_ref[0]) if tile_off == 0 else xl_ref[0]
    row = jnp.where(is_ctx, n_batch, b)
    h, gate = _modulated_norm(x, mod_ref, nw_ref, row, d)
    def latent_rows(ref):
        sub = DFT_BLOCK
        return jnp.concatenate(
            [jnp.concatenate([ref[0, h, kb, k2 * sub:(k2 + 1) * sub, :]
                              for k2 in range(ref.shape[3] // sub)
                              for kb in range(ref.shape[2])], axis=0)
             for h in range(ref.shape[1])], axis=1)

    pc = jnp.concatenate([jnp.where(is_ctx, prc_ref[0], latent_rows(pr_ref)),
                          jnp.where(is_ctx, pic_ref[0], latent_rows(pi_ref))], axis=1)
    yb = _dot(_dot3_right(pc, chh_ref[...], chl_ref[...]).astype(BF16), wf_ref[...]) * zb_ref[0]
    branches = (ya_ref[0], yd_ref[0], yb.astype(BF16), yc_ref[0])
    hb = h.astype(BF16)
    acc = jnp.zeros((x.shape[0], d), F32)
    for r in range(N_BRANCH):
        g = jax.nn.sigmoid(_dot(hb, wm_ref[:, r * d:(r + 1) * d]) + bm_ref[:, r * d:(r + 1) * d])
        acc = acc + g * _dot(branches[r], wbr_ref[r])
    out = x + gate * _dot(acc.astype(BF16), wo_ref[...])
    out_refs[-1][0] = out
    if tile_off == 0:
        @pl.when(is_ctx)
        def _():
            out_refs[0][0] = out


def _merge(xc, xl, mod, nw, ya, yd, pr, pi, prc, pic, zb, yc, chan, wf_bd, w_br, w_merge, b_merge,
           w_out, skip_ctx):
    nb, seq, d = xl.shape
    tm = TOKEN_TILE
    off = 1 if skip_ctx else 0
    nt = (seq + xc.shape[1]) // tm - off
    tok_spec = lambda w: pl.BlockSpec((1, tm, w), lambda b, j: (b, j + off, 0))
    ctx_tile = pl.BlockSpec((1, tm, d), lambda b, j: (b, 0, 0))
    lat_tile = pl.BlockSpec((1, tm, d), lambda b, j: (b, jnp.maximum(j + off - 1, 0), 0))
    xc_shape = jax.ShapeDtypeStruct(xc.shape, F32)
    xl_shape = jax.ShapeDtypeStruct(xl.shape, F32)
    lat_spec = lambda a: pl.BlockSpec((1, a.shape[1], a.shape[2], tm // a.shape[2], a.shape[4]),
                                      lambda b, j: (b, 0, 0, jnp.maximum(j + off - 1, 0), 0))
    ctx_spec = lambda a: pl.BlockSpec((1,) + a.shape[1:], lambda b, j: (b, 0, 0))
    const = lambda a: pl.BlockSpec(a.shape, lambda b, j: (0,) * a.ndim)
    return pl.pallas_call(
        functools.partial(_merge_kernel, n_batch=nb, tile_off=off),
        grid=(nb, nt),
        in_specs=[ctx_tile, lat_tile, const(mod), const(nw), tok_spec(BRANCH_W),
                  tok_spec(BRANCH_W), lat_spec(pr), lat_spec(pi), ctx_spec(prc), ctx_spec(pic),
                  tok_spec(BRANCH_W), tok_spec(BRANCH_W), const(chan[0]), const(chan[1]),
                  const(wf_bd), const(w_br), const(w_merge), const(b_merge), const(w_out)],
        out_specs=[lat_tile] if skip_ctx else [ctx_tile, lat_tile],
        out_shape=[xl_shape] if skip_ctx else [xc_shape, xl_shape],
        compiler_params=pltpu.CompilerParams(vmem_limit_bytes=VMEM_LIMIT),
        name="merge",
    )(xc, xl, mod, nw, ya, yd, pr, pi, prc, pic, zb, yc, *chan, wf_bd, w_br, w_merge, b_merge,
      w_out)


def _rope_tables(seq_len, ctx_len):
    t = jnp.arange(seq_len, dtype=jnp.int32)
    r = (t // GRID_W).astype(F32)
    col = (t % GRID_W).astype(F32)
    nf = HEAD_DIM // 4
    inv = ROPE_BASE ** (-jnp.arange(nf, dtype=F32) / nf)
    ar = r[:, None] * inv[None, :]
    ac = col[:, None] * inv[None, :]
    cr, sr, cc, sc = jnp.cos(ar), jnp.sin(ar), jnp.cos(ac), jnp.sin(ac)
    cos_h = jnp.concatenate([cr, cr, cc, cc], axis=1)
    sin_h = jnp.concatenate([-sr, sr, -sc, sc], axis=1)
    cos_t = jnp.concatenate([jnp.ones((ctx_len, HEAD_DIM), F32), cos_h], axis=0)
    sin_t = jnp.concatenate([jnp.zeros((ctx_len, HEAD_DIM), F32), sin_h], axis=0)
    return jnp.tile(cos_t, (1, 2)), jnp.tile(sin_t, (1, 2))


def kernel(x, c, ctx, c_ctx, norm_w, w_ada, b_ada, w_in, qn_a, kn_a, qn_d, kn_d, sink_d,
           w_fnet, w_sp, b_sp, w_br, w_merge, b_merge, w_out):
    nb, seq, d = x.shape
    ctx_len = ctx.shape[1]
    depth = norm_w.shape[0]
    assert ctx_len == TOKEN_TILE and seq % (2 * ATTN_A_KEYS) == 0 and nb < 8

    cvecs = jnp.zeros((8, d), F32).at[:nb].set(c).at[nb].set(c_ctx)
    mods = _modulation(cvecs, w_ada, b_ada)

    cos_t, sin_t = _rope_tables(seq, ctx_len)
    m1, m3, mc, chan = _dft_tables(seq, ctx_len)
    pair = lambda w: jnp.tile(w, 2).reshape(1, LANES)
    eye_g = jnp.eye(BRANCH_W // GROUP_W, dtype=F32)

    xc, xl = ctx, x
    for l in range(depth):
        nw = norm_w[l].reshape(1, d)
        wf_bd = jnp.einsum('gh,gcd->gchd', eye_g, w_fnet[l]).reshape(BRANCH_W, BRANCH_W)
        b_sp_t = jnp.repeat(b_sp[l].T, GROUP_W, axis=1)
        (qat, ka, vat, za, qdt, kd, vdt, zd, fb, fbc, zb, yc) = _projection(
            xc, xl, mods[l], nw, w_in[l].astype(BF16), cos_t, sin_t,
            pair(qn_a[l]), pair(kn_a[l]), pair(qn_d[l]), pair(kn_d[l]),
            w_sp[l].astype(BF16), b_sp_t)
        ya = _attention_a(qat, ka, vat, za, ctx_len)
        yd = _attention_d(sink_d[l], qdt, kd, vdt, zd, ctx_len)
        pr, pi, prc, pic = _fnet_dft(fb, fbc, m1, m3, mc)
        *xc_new, xl = _merge(xc, xl, mods[l], nw, ya, yd, pr, pi, prc, pic, zb, yc, chan,
                             wf_bd.astype(BF16), w_br[l].astype(BF16), w_merge[l].astype(BF16),
                             b_merge[l].reshape(1, -1), w_out[l].astype(BF16),
                             skip_ctx=(l == depth - 1))
        xc = xc_new[0] if xc_new else None
    return xl
```

```python
import functools

import numpy as np
import jax
import jax.numpy as jnp
from jax import lax
from jax.experimental import pallas as pl
from jax.experimental.pallas import tpu as pltpu

F32 = jnp.float32
BF16 = jnp.bfloat16

GRID_W = 64
HEAD_DIM = 64
BRANCH_W = 256
KV_W = 128
N_BRANCH = 4
GROUP_W = 64
CHUNK = 128
Q_BLOCK = 128
WINDOW = 128
ROPE_BASE = 10000.0
EPS = 1e-6
LOG2_E = float(np.log2(np.e))
LANES = 128
TOKEN_TILE = 256
V_ROWS = 80
ATTN_A_KEYS = 512
ATTN_A_UNROLL = 16
ATTN_D_TILES = 3
DFT_BLOCK = 8
DFT_INNER = 64
VMEM_LIMIT = 56 * 1024 * 1024

_IN_SIZES = (BRANCH_W, KV_W, KV_W, BRANCH_W, BRANCH_W, KV_W, KV_W, BRANCH_W,
             BRANCH_W, BRANCH_W, BRANCH_W, BRANCH_W, BRANCH_W)
_IN_OFF = tuple(int(v) for v in np.cumsum((0,) + _IN_SIZES))
IN_W = _IN_OFF[-1]
(_AQ, _AK, _AV, _AZ, _DQ, _DK, _DV, _DZ, _BF, _BZ, _CU, _CV, _CZ) = _IN_OFF[:-1]


def _silu(z):
    return z * jax.nn.sigmoid(z)


def _gelu(x):
    return 0.5 * x * (1.0 + lax.erf(x * np.float32(np.sqrt(0.5))))


def _dot(a, b):
    return jnp.dot(a, b, preferred_element_type=F32)


def _lane_lo(width=LANES):
    lane = lax.broadcasted_iota(jnp.int32, (1, width), 1)
    return (lane % LANES) < HEAD_DIM


def _mod_kernel(cv_ref, w_ref, b_ref, o_ref):
    s = _silu(cv_ref[...])
    o_ref[0] = _dot(s.astype(BF16), w_ref[0].astype(BF16)) + b_ref[0]


def _modulation(cvecs, w_ada, b_ada):
    depth, d, d3 = w_ada.shape
    nblk = d3 // d
    return pl.pallas_call(
        _mod_kernel,
        grid=(depth, nblk),
        in_specs=[pl.BlockSpec((8, d), lambda l, n: (0, 0)),
                  pl.BlockSpec((1, d, d), lambda l, n: (l, 0, n)),
                  pl.BlockSpec((1, 1, d), lambda l, n: (l, 0, n))],
        out_specs=pl.BlockSpec((1, 8, d), lambda l, n: (l, 0, n)),
        out_shape=jax.ShapeDtypeStruct((depth, 8, d3), F32),
        name="modulation",
    )(cvecs, w_ada, b_ada.reshape(depth, 1, d3))


def _modulated_norm(x, mod_ref, nw_ref, row, d):
    m = mod_ref[pl.ds(row, 1), :]
    sh, sc = m[:, :d], m[:, d:2 * d]
    ms = jnp.mean(x * x, axis=-1, keepdims=True)
    xn = x * lax.rsqrt(ms + EPS) * nw_ref[...]
    return xn * (1.0 + sc) + sh, m[:, 2 * d:]


def _head_norm_rope(xs, wn, cos, sin, scale):
    sq = xs * xs
    lo = _lane_lo()
    ms = jnp.where(lo, jnp.sum(jnp.where(lo, sq, 0.0), axis=1, keepdims=True),
                   jnp.sum(jnp.where(lo, 0.0, sq), axis=1, keepdims=True)) * (1.0 / HEAD_DIM)
    y = xs * lax.rsqrt(ms + EPS) * wn
    lane = lax.broadcasted_iota(jnp.int32, (1, LANES), 1)
    first = (lane % 32) < 16
    sw = jnp.where(first, pltpu.roll(y, LANES - 16, axis=1), pltpu.roll(y, 16, axis=1))
    y = y * cos + sw * sin
    return y * scale if scale != 1.0 else y


def _proj_kernel(xc_ref, xl_ref, mod_ref, nw_ref, win_ref, cos_ref, sin_ref,
                 qna_ref, kna_ref, qnd_ref, knd_ref, wsp_ref, bsp_ref,
                 qat_ref, ka_ref, vat_ref, za_ref, qdt_ref, kd_ref, vdt_ref, zd_ref,
                 fb_ref, fbc_ref, zb_ref, yc_ref, *, n_batch):
    b, j = pl.program_id(0), pl.program_id(1)
    d = xl_ref.shape[-1]
    x = jnp.where(j == 0, xc_ref[0], xl_ref[0])
    row = jnp.where(j == 0, n_batch, b)
    h, _ = _modulated_norm(x, mod_ref, nw_ref, row, d)
    hb = h.astype(BF16)
    pc = _dot(hb, win_ref[:, _CU:])
    p = _dot(hb, win_ref[:, :_CU])

    u = _gelu(pc[:, :BRANCH_W])
    v = _gelu(pc[:, BRANCH_W:2 * BRANCH_W]).astype(BF16)
    zc = _silu(pc[:, 2 * BRANCH_W:])
    group = lax.broadcasted_iota(jnp.int32, (1, BRANCH_W), 1) // GROUP_W
    for c in range(x.shape[0] // CHUNK):
        rows = slice(c * CHUNK, (c + 1) * CHUNK)
        sp = bsp_ref[...]
        for g in range(BRANCH_W // GROUP_W):
            sp = sp + jnp.where(group == g, _dot(wsp_ref[g], v[rows]), 0.0)
        yc_ref[0, rows, :] = (u[rows] * sp * zc[rows]).astype(BF16)

    cos, sin = cos_ref[...], sin_ref[...]
    q_scale = HEAD_DIM ** -0.5 * LOG2_E

    def attn_branch(q0, k0, v0, z0, qn_ref, kn_ref, qt_ref, k_ref, vt_ref, z_ref):
        q = [_head_norm_rope(p[:, q0 + s * LANES:q0 + (s + 1) * LANES], qn_ref[...],
                             cos, sin, q_scale) for s in range(BRANCH_W // LANES)]
        qt_ref[0] = jnp.concatenate(q, axis=1).T.astype(BF16)
        k_ref[0] = _head_norm_rope(p[:, k0:k0 + KV_W], kn_ref[...], cos, sin,
                                   1.0).astype(BF16)
        vt = p[:, v0:v0 + KV_W].T
        ones = jnp.ones((V_ROWS - HEAD_DIM, vt.shape[1]), F32)
        vt_ref[0] = jnp.concatenate([vt[:HEAD_DIM], ones, vt[HEAD_DIM:], ones],
                                    axis=0).astype(BF16)
        z_ref[0] = _silu(p[:, z0:z0 + BRANCH_W])

    attn_branch(_AQ, _AK, _AV, _AZ, qna_ref, kna_ref, qat_ref, ka_ref, vat_ref, za_ref)
    attn_branch(_DQ, _DK, _DV, _DZ, qnd_ref, knd_ref, qdt_ref, kd_ref, vdt_ref, zd_ref)

    f = p[:, _BF:_BF + BRANCH_W]
    sub = DFT_BLOCK
    for hi in range(DFT_INNER // sub):
        rows = jnp.concatenate([f[n1 * DFT_INNER + hi * sub:n1 * DFT_INNER + (hi + 1) * sub]
                                for n1 in range(f.shape[0] // DFT_INNER)], axis=0)
        for half in range(BRANCH_W // LANES):
            fb_ref[0, half, hi] = rows[:, half * LANES:(half + 1) * LANES]

    @pl.when(j == 0)
    def _():
        fbc_ref[0] = f

    zb_ref[0] = _silu(p[:, _BZ:_BZ + BRANCH_W])


def _projection(layer, xc, xl, mod, nw, w_in, cos_t, sin_t, qna, kna, qnd, knd, w_sp, b_sp_t):
    nb, seq, d = xl.shape
    tm = TOKEN_TILE
    t = seq + xc.shape[1]
    nt = t // tm
    lat_tile = lambda b, j: (b, jnp.maximum(j - 1, 0), 0)
    tok = lambda w, dt: jax.ShapeDtypeStruct((nb, t, w), dt)
    tok_spec = lambda w: pl.BlockSpec((1, tm, w), lambda b, j: (b, j, 0))
    tr = lambda w: jax.ShapeDtypeStruct((nb, w, t), BF16)
    tr_spec = lambda w: pl.BlockSpec((1, w, tm), lambda b, j: (b, 0, j))
    const = lambda a: pl.BlockSpec(a.shape, lambda b, j: (0,) * a.ndim)
    of_layer = lambda a: pl.BlockSpec((None,) + a.shape[1:],
                                      lambda b, j: (layer,) + (0,) * (a.ndim - 1))
    attn_specs = [tr_spec(BRANCH_W), tok_spec(KV_W), tr_spec(2 * V_ROWS), tok_spec(BRANCH_W)]
    attn_shapes = [tr(BRANCH_W), tok(KV_W, BF16), tr(2 * V_ROWS), tok(BRANCH_W, F32)]
    n_hi = DFT_INNER // DFT_BLOCK
    return pl.pallas_call(
        functools.partial(_proj_kernel, n_batch=nb),
        grid=(nb, nt),
        in_specs=[pl.BlockSpec((1, tm, d), lambda b, j: (b, 0, 0)),
                  pl.BlockSpec((1, tm, d), lat_tile), of_layer(mod), const(nw), of_layer(w_in),
                  pl.BlockSpec((tm, LANES), lambda b, j: (j, 0)),
                  pl.BlockSpec((tm, LANES), lambda b, j: (j, 0)),
                  const(qna), const(kna), const(qnd), const(knd), of_layer(w_sp), const(b_sp_t)],
        out_specs=attn_specs * 2 + [
            pl.BlockSpec((1, BRANCH_W // LANES, n_hi, tm // n_hi, LANES),
                         lambda b, j: (b, 0, 0, jnp.maximum(j - 1, 0), 0)),
            pl.BlockSpec((1, tm, BRANCH_W), lambda b, j: (b, 0, 0)),
            tok_spec(BRANCH_W), tok_spec(BRANCH_W)],
        out_shape=attn_shapes * 2 + [
            jax.ShapeDtypeStruct((nb, BRANCH_W // LANES, n_hi, (t - tm) // n_hi, LANES), F32),
            jax.ShapeDtypeStruct((nb, tm, BRANCH_W), F32),
            tok(BRANCH_W, F32), tok(BRANCH_W, BF16)],
        compiler_params=pltpu.CompilerParams(vmem_limit_bytes=VMEM_LIMIT),
        name="projection",
    )(xc, xl, mod, nw, w_in, cos_t, sin_t, qna, kna, qnd, knd, w_sp, b_sp_t)


def _group_queries(qt_ref, g, tokens=slice(None)):
    r0 = g * LANES
    qrow = jnp.concatenate([qt_ref[0, r0:r0 + HEAD_DIM, tokens],
                            qt_ref[0, r0 + HEAD_DIM:r0 + LANES, tokens]], axis=1)
    zeros = jnp.zeros_like(qrow)
    return jnp.concatenate([qrow, zeros] if g == 0 else [zeros, qrow], axis=0)


def _store_heads(o_ref, z_ref, g, o_t, tokens=slice(None)):
    tq = o_t.shape[1] // 2
    o = jnp.concatenate([o_t[:, :tq], o_t[:, tq:]], axis=0).T
    cols = slice(g * LANES, (g + 1) * LANES)
    o_ref[0, tokens, cols] = (o * z_ref[0, tokens, cols]).astype(BF16)


def _attn_a_kernel(qt_ref, k_ref, vt_ref, z_ref, o_ref, s_ref, smax_ref, m_ref, acc_ref,
                   *, tk, ctx_len):
    j = pl.program_id(1)
    tq = qt_ref.shape[2]
    t = k_ref.shape[1]
    n_lat = (t - ctx_len) // tk
    groups = range(2)
    qt_g = [_group_queries(qt_ref, g) for g in groups]

    def scores(g, start, size):
        return _dot(k_ref[0, pl.ds(start, size), :], qt_g[g])

    def values(g, start, size):
        return vt_ref[0, g * V_ROWS:(g + 1) * V_ROWS, pl.ds(start, size)]

    def lat_start(c):
        return pl.multiple_of(ctx_len + c * tk, LANES)

    def produce(g, slot, c):
        s = scores(g, lat_start(c), tk)
        smax_ref[slot, g] = jnp.max(s, axis=0, keepdims=True)
        s_ref[slot, g] = s

    s_ctx = [scores(g, 0, ctx_len) for g in groups]
    for g in groups:
        produce(g, 0, 0)
    for g in groups:
        m = jnp.max(s_ctx[g], axis=0, keepdims=True)
        m_ref[g] = m
        acc_ref[g] = _dot(values(g, 0, ctx_len), jnp.exp2(s_ctx[g] - m).astype(BF16))

    def consume(g, slot, c):
        m = m_ref[g]
        m_new = jnp.maximum(m, smax_ref[slot, g])
        pr = jnp.exp2(s_ref[slot, g] - m_new).astype(BF16)
        acc_ref[g] = jnp.exp2(m - m_new) * acc_ref[g] + _dot(values(g, lat_start(c), tk), pr)
        m_ref[g] = m_new

    @pl.when(j > 0)
    def _():
        unroll = ATTN_A_UNROLL

        def body(i, carry):
            for k in range(unroll):
                c = unroll * i + k
                for g in groups:
                    produce(g, (k + 1) % 2, c + 1)
                    consume(g, k % 2, c)
            return carry

        n_loop = (n_lat - 1) // unroll
        if n_loop:
            lax.fori_loop(0, n_loop, body, 0)
        for c in range(n_loop * unroll, n_lat):
            for g in groups:
                if c + 1 < n_lat:
                    produce(g, (c + 1) % 2, c + 1)
                consume(g, c % 2, c)

    for g in groups:
        acc = acc_ref[g]
        _store_heads(o_ref, z_ref, g, acc[:HEAD_DIM] / acc[HEAD_DIM:HEAD_DIM + 1])


def _attention_a(qt, k, vt, z, ctx_len):
    nb, t, _ = k.shape
    tq = TOKEN_TILE
    tk = ATTN_A_KEYS
    return pl.pallas_call(
        functools.partial(_attn_a_kernel, tk=tk, ctx_len=ctx_len),
        grid=(nb, t // tq),
        in_specs=[pl.BlockSpec((1, BRANCH_W, tq), lambda b, j: (b, 0, j)),
                  pl.BlockSpec((1, t, KV_W), lambda b, j: (b, 0, 0)),
                  pl.BlockSpec((1, 2 * V_ROWS, t), lambda b, j: (b, 0, 0)),
                  pl.BlockSpec((1, tq, BRANCH_W), lambda b, j: (b, j, 0))],
        out_specs=pl.BlockSpec((1, tq, BRANCH_W), lambda b, j: (b, j, 0)),
        out_shape=jax.ShapeDtypeStruct((nb, t, BRANCH_W), BF16),
        scratch_shapes=[pltpu.VMEM((2, 2, tk, 2 * tq), F32),
                        pltpu.VMEM((2, 2, 1, 2 * tq), F32),
                        pltpu.VMEM((2, 1, 2 * tq), F32),
                        pltpu.VMEM((2, V_ROWS, 2 * tq), F32)],
        compiler_params=pltpu.CompilerParams(vmem_limit_bytes=VMEM_LIMIT),
        name="attention_a",
    )(qt, k, vt, z)


def _attn_d_kernel(sink_ref, qt_ref, k_ref, vt_ref, z_ref, o_ref, *, ctx_len, tq):
    tiles = qt_ref.shape[2] // tq
    t = k_ref.shape[1]
    span = tq + 2 * WINDOW
    first = lax.broadcasted_iota(jnp.int32, (1, 2 * tq), 1) < tq
    rel = (lax.broadcasted_iota(jnp.int32, (span, 2 * tq), 0)
           - lax.broadcasted_iota(jnp.int32, (span, 2 * tq), 1) % tq)

    def scores(u):
        tile = pl.program_id(1) * tiles + u
        q0 = tile * tq
        start = pl.multiple_of(jnp.clip(q0 - WINDOW, ctx_len, t - span), LANES)
        mask = (jnp.abs(rel + (start - q0)) <= WINDOW) & (tile > 0)
        sc = []
        for g in range(2):
            qt_g = _group_queries(qt_ref, g, slice(u * tq, (u + 1) * tq))
            sc.append((_dot(k_ref[0, :ctx_len, :], qt_g),
                       _dot(k_ref[0, pl.ds(start, span), :], qt_g)))
        return start, mask, sc

    def finish(u, start, mask, sc):
        for g in range(2):
            vrows = slice(g * V_ROWS, (g + 1) * V_ROWS)
            s_ctx = sc[g][0]
            s_loc = jnp.where(mask, sc[g][1], -jnp.inf)
            sink = jnp.where(first, sink_ref[2 * g], sink_ref[2 * g + 1]) * LOG2_E
            m = jnp.maximum(jnp.maximum(jnp.max(s_ctx, axis=0, keepdims=True),
                                        jnp.max(s_loc, axis=0, keepdims=True)), sink)
            acc = (_dot(vt_ref[0, vrows, :ctx_len], jnp.exp2(s_ctx - m).astype(BF16))
                   + _dot(vt_ref[0, vrows, pl.ds(start, span)], jnp.exp2(s_loc - m).astype(BF16)))
            l = acc[HEAD_DIM:HEAD_DIM + 1] + jnp.exp2(sink - m)
            _store_heads(o_ref, z_ref, g, acc[:HEAD_DIM] / l, slice(u * tq, (u + 1) * tq))

    pending = scores(0)
    for u in range(tiles):
        ready, pending = pending, (scores(u + 1) if u + 1 < tiles else None)
        finish(u, *ready)


def _attention_d(sink, qt, k, vt, z, ctx_len):
    nb, t, _ = k.shape
    tq = TOKEN_TILE
    step = ATTN_D_TILES * tq
    return pl.pallas_call(
        functools.partial(_attn_d_kernel, ctx_len=ctx_len, tq=tq),
        grid=(nb, t // step),
        in_specs=[pl.BlockSpec(memory_space=pltpu.SMEM),
                  pl.BlockSpec((1, BRANCH_W, step), lambda b, j: (b, 0, j)),
                  pl.BlockSpec((1, t, KV_W), lambda b, j: (b, 0, 0)),
                  pl.BlockSpec((1, 2 * V_ROWS, t), lambda b, j: (b, 0, 0)),
                  pl.BlockSpec((1, step, BRANCH_W), lambda b, j: (b, j, 0))],
        out_specs=pl.BlockSpec((1, step, BRANCH_W), lambda b, j: (b, j, 0)),
        out_shape=jax.ShapeDtypeStruct((nb, t, BRANCH_W), BF16),
        compiler_params=pltpu.CompilerParams(vmem_limit_bytes=VMEM_LIMIT),
        name="attention_d",
    )(sink, qt, k, vt, z)


def _dft_tables(seq_len, ctx_len):
    n2c = DFT_INNER
    n1c = seq_len // n2c
    scale = 1.0 / np.sqrt(float(seq_len) * GROUP_W)
    k1 = np.arange(n1c)[:, None]
    n1 = np.arange(n1c)[None, :]
    m1 = np.zeros((n2c, 2 * n1c, n1c), np.float64)
    for n2 in range(n2c):
        ang = -2.0 * np.pi * (k1 * n1 / n1c + n2 * k1 / seq_len)
        m1[n2, :n1c] = np.cos(ang) * scale
        m1[n2, n1c:] = np.sin(ang) * scale
    ang3 = 2.0 * np.pi * np.outer(np.arange(n2c), np.arange(n2c)) / n2c
    c3, s3 = np.cos(ang3), np.sin(ang3)
    m3 = np.block([[c3, s3], [-s3, c3]])
    cscale = 1.0 / np.sqrt(float(ctx_len) * GROUP_W)
    angc = 2.0 * np.pi * np.outer(np.arange(ctx_len), np.arange(ctx_len)) / ctx_len
    mc = np.concatenate([np.cos(angc), -np.sin(angc)], axis=0) * cscale
    angg = 2.0 * np.pi * np.outer(np.arange(GROUP_W), np.arange(GROUP_W)) / GROUP_W
    eye = np.eye(BRANCH_W // GROUP_W)
    chan = np.concatenate([np.kron(eye, np.cos(angg)), np.kron(eye, np.sin(angg))], axis=0)
    return _stacked(m1), _stacked(m3), _split(mc), _split(chan)


def _stacked(a):
    hi, lo = _split(a)
    return jnp.concatenate([hi, lo, hi], axis=-1)


def _dot3_stacked(m3, x):
    x_hi = x.astype(BF16)
    x_lo = (x - x_hi.astype(F32)).astype(BF16)
    return _dot(m3, jnp.concatenate([x_hi, x_hi, x_lo], axis=0))


def _split(a):
    a32 = jnp.asarray(a.astype(np.float32))
    hi = a32.astype(BF16)
    return hi, (a32 - hi.astype(F32)).astype(BF16)


def _dot3_left(m_hi, m_lo, x):
    x_hi = x.astype(BF16)
    x_lo = (x - x_hi.astype(F32)).astype(BF16)
    return _dot(m_hi, x_hi) + _dot(m_lo, x_hi) + _dot(m_hi, x_lo)


def _dot3_right(x, m_hi, m_lo):
    x_hi = x.astype(BF16)
    x_lo = (x - x_hi.astype(F32)).astype(BF16)
    return _dot(x_hi, m_hi) + _dot(x_hi, m_lo) + _dot(x_lo, m_hi)


def _fnet_stage1_kernel(f_ref, m_ref, br_ref, bi_ref):
    sub = DFT_BLOCK
    halves = range(f_ref.shape[1])
    n1c = f_ref.shape[3] // sub
    for q in range(sub):
        rows = pl.ds(q, n1c, stride=sub)
        x = jnp.concatenate([f_ref[0, h, 0, rows, :] for h in halves], axis=1)
        res = _dot3_stacked(m_ref[q], x)
        for h in halves:
            br_ref[0, h, 0, rows, :] = res[:n1c, h * LANES:(h + 1) * LANES]
            bi_ref[0, h, 0, rows, :] = res[n1c:, h * LANES:(h + 1) * LANES]


def _fnet_stage2_kernel(br_ref, bi_ref, fc_ref, m_ref, mch_ref, mcl_ref,
                        pr_ref, pi_ref, prc_ref, pic_ref):
    n2c = DFT_INNER
    sub = DFT_BLOCK
    ctx_len = fc_ref.shape[1]

    @pl.when(pl.program_id(1) == 0)
    def _():
        res = _dot3_left(mch_ref[...], mcl_ref[...], fc_ref[0])
        prc_ref[0] = res[:ctx_len]
        pic_ref[0] = res[ctx_len:]

    halves = range(br_ref.shape[1])
    for q in range(sub):
        xin = jnp.concatenate(
            [jnp.concatenate([ref[0, h, hi, q * sub:(q + 1) * sub, :] for h in halves], axis=1)
             for ref in (br_ref, bi_ref) for hi in range(n2c // sub)], axis=0)
        res = _dot3_stacked(m_ref[...], xin)
        rows = pl.ds(q, n2c, stride=sub)
        for h in halves:
            pr_ref[0, h, 0, rows, :] = res[:n2c, h * LANES:(h + 1) * LANES]
            pi_ref[0, h, 0, rows, :] = res[n2c:, h * LANES:(h + 1) * LANES]


def _fnet_dft(fb, fb_ctx, m1, m3, mc):
    nb, nh, n_hi, rows, w = fb.shape
    ctx_len = fb_ctx.shape[1]
    n2c = DFT_INNER
    sub = DFT_BLOCK
    n1c = rows // sub
    blk1 = pl.BlockSpec((1, nh, 1, rows, w), lambda b, i: (b, 0, i, 0, 0))
    tab1 = pl.BlockSpec((sub, 2 * n1c, 3 * n1c), lambda b, i: (i, 0, 0))
    b4 = jax.ShapeDtypeStruct((nb, nh, n_hi, rows, w), F32)
    br, bi = pl.pallas_call(
        _fnet_stage1_kernel,
        grid=(nb, n_hi),
        in_specs=[blk1, tab1],
        out_specs=[blk1, blk1],
        out_shape=[b4, b4],
        compiler_params=pltpu.CompilerParams(vmem_limit_bytes=VMEM_LIMIT),
        name="fnet_stage1",
    )(fb, m1)
    blk2 = pl.BlockSpec((1, nh, n_hi, sub * sub, w), lambda b, i: (b, 0, 0, i, 0))
    oblk = pl.BlockSpec((1, nh, 1, n2c * sub, w), lambda b, i: (b, 0, i, 0, 0))
    cblk = pl.BlockSpec((1, ctx_len, nh * w), lambda b, i: (b, 0, 0))
    const = lambda a: pl.BlockSpec(a.shape, lambda b, i: (0,) * a.ndim)
    p4 = jax.ShapeDtypeStruct((nb, nh, n1c // sub, n2c * sub, w), F32)
    pc = jax.ShapeDtypeStruct((nb, ctx_len, nh * w), F32)
    return pl.pallas_call(
        _fnet_stage2_kernel,
        grid=(nb, n1c // sub),
        in_specs=[blk2, blk2, cblk, const(m3), const(mc[0]), const(mc[1])],
        out_specs=[oblk, oblk, cblk, cblk],
        out_shape=[p4, p4, pc, pc],
        compiler_params=pltpu.CompilerParams(vmem_limit_bytes=VMEM_LIMIT),
        name="fnet_stage2",
    )(br, bi, fb_ctx, m3, *mc)


def _merge_kernel(xc_ref, xl_ref, mod_ref, nw_ref, ya_ref, yd_ref, pr_ref, pi_ref, prc_ref,
                  pic_ref, zb_ref, yc_ref, chh_ref, chl_ref, wf_ref, wbr_ref, wm_ref, bm_ref,
                  wo_ref, *out_refs, n_batch, tile_off):
    b, j = pl.program_id(0), pl.program_id(1)
    d = xl_ref.shape[-1]
    is_ctx = j + tile_off == 0
    x = jnp.where(is_ctx, xc_ref[0], xl_ref[0]) if tile_off == 0 else xl_ref[0]
    row = jnp.where(is_ctx, n_batch, b)
    h, gate = _modulated_norm(x, mod_ref, nw_ref, row, d)
    def latent_rows(ref):
        sub = DFT_BLOCK
        return jnp.concatenate(
            [jnp.concatenate([ref[0, h, kb, k2 * sub:(k2 + 1) * sub, :]
                              for k2 in range(ref.shape[3] // sub)
                              for kb in range(ref.shape[2])], axis=0)
             for h in range(ref.shape[1])], axis=1)

    pc = jnp.concatenate([jnp.where(is_ctx, prc_ref[0], latent_rows(pr_ref)),
                          jnp.where(is_ctx, pic_ref[0], latent_rows(pi_ref))], axis=1)
    yb = _dot(_dot3_right(pc, chh_ref[...], chl_ref[...]).astype(BF16), wf_ref[...]) * zb_ref[0]
    branches = (ya_ref[0], yd_ref[0], yb.astype(BF16), yc_ref[0])
    hb = h.astype(BF16)
    acc = jnp.zeros((x.shape[0], d), F32)
    for r in range(N_BRANCH):
        g = jax.nn.sigmoid(_dot(hb, wm_ref[:, r * d:(r + 1) * d]) + bm_ref[:, r * d:(r + 1) * d])
        acc = acc + g * _dot(branches[r], wbr_ref[r])
    out = x + gate * _dot(acc.astype(BF16), wo_ref[...])
    out_refs[-1][0] = out
    if tile_off == 0:
        @pl.when(is_ctx)
        def _():
            out_refs[0][0] = out


def _merge(layer, xc, xl, mod, nw, ya, yd, pr, pi, prc, pic, zb, yc, chan, wf_bd, w_br, w_merge,
           b_merge, w_out, skip_ctx):
    nb, seq, d = xl.shape
    tm = TOKEN_TILE
    off = 1 if skip_ctx else 0
    nt = (seq + xc.shape[1]) // tm - off
    tok_spec = lambda w: pl.BlockSpec((1, tm, w), lambda b, j: (b, j + off, 0))
    ctx_tile = pl.BlockSpec((1, tm, d), lambda b, j: (b, 0, 0))
    lat_tile = pl.BlockSpec((1, tm, d), lambda b, j: (b, jnp.maximum(j + off - 1, 0), 0))
    xc_shape = jax.ShapeDtypeStruct(xc.shape, F32)
    xl_shape = jax.ShapeDtypeStruct(xl.shape, F32)
    lat_spec = lambda a: pl.BlockSpec((1, a.shape[1], a.shape[2], tm // a.shape[2], a.shape[4]),
                                      lambda b, j: (b, 0, 0, jnp.maximum(j + off - 1, 0), 0))
    ctx_spec = lambda a: pl.BlockSpec((1,) + a.shape[1:], lambda b, j: (b, 0, 0))
    const = lambda a: pl.BlockSpec(a.shape, lambda b, j: (0,) * a.ndim)
    of_layer = lambda a: pl.BlockSpec((None,) + a.shape[1:],
                                      lambda b, j: (layer,) + (0,) * (a.ndim - 1))
    return pl.pallas_call(
        functools.partial(_merge_kernel, n_batch=nb, tile_off=off),
        grid=(nb, nt),
        in_specs=[ctx_tile, lat_tile, of_layer(mod), const(nw), tok_spec(BRANCH_W),
                  tok_spec(BRANCH_W), lat_spec(pr), lat_spec(pi), ctx_spec(prc), ctx_spec(pic),
                  tok_spec(BRANCH_W), tok_spec(BRANCH_W), const(chan[0]), const(chan[1]),
                  const(wf_bd), of_layer(w_br), of_layer(w_merge), const(b_merge),
                  of_layer(w_out)],
        out_specs=[lat_tile] if skip_ctx else [ctx_tile, lat_tile],
        out_shape=[xl_shape] if skip_ctx else [xc_shape, xl_shape],
        compiler_params=pltpu.CompilerParams(vmem_limit_bytes=VMEM_LIMIT),
        name="merge",
    )(xc, xl, mod, nw, ya, yd, pr, pi, prc, pic, zb, yc, *chan, wf_bd, w_br, w_merge, b_merge,
      w_out)


def _rope_tables(seq_len, ctx_len):
    t = jnp.arange(seq_len, dtype=jnp.int32)
    r = (t // GRID_W).astype(F32)
    col = (t % GRID_W).astype(F32)
    nf = HEAD_DIM // 4
    inv = ROPE_BASE ** (-jnp.arange(nf, dtype=F32) / nf)
    ar = r[:, None] * inv[None, :]
    ac = col[:, None] * inv[None, :]
    cr, sr, cc, sc = jnp.cos(ar), jnp.sin(ar), jnp.cos(ac), jnp.sin(ac)
    cos_h = jnp.concatenate([cr, cr, cc, cc], axis=1)
    sin_h = jnp.concatenate([-sr, sr, -sc, sc], axis=1)
    cos_t = jnp.concatenate([jnp.ones((ctx_len, HEAD_DIM), F32), cos_h], axis=0)
    sin_t = jnp.concatenate([jnp.zeros((ctx_len, HEAD_DIM), F32), sin_h], axis=0)
    return jnp.tile(cos_t, (1, 2)), jnp.tile(sin_t, (1, 2))


def kernel(x, c, ctx, c_ctx, norm_w, w_ada, b_ada, w_in, qn_a, kn_a, qn_d, kn_d, sink_d,
           w_fnet, w_sp, b_sp, w_br, w_merge, b_merge, w_out):
    nb, seq, d = x.shape
    ctx_len = ctx.shape[1]
    depth = norm_w.shape[0]
    assert ctx_len == TOKEN_TILE and seq % (2 * ATTN_A_KEYS) == 0 and nb < 8
    assert (seq + ctx_len) % (ATTN_D_TILES * TOKEN_TILE) == 0

    cvecs = jnp.zeros((8, d), F32).at[:nb].set(c).at[nb].set(c_ctx)
    mods = _modulation(cvecs, w_ada, b_ada)

    cos_t, sin_t = _rope_tables(seq, ctx_len)
    m1, m3, mc, chan = _dft_tables(seq, ctx_len)
    pair = lambda w: jnp.tile(w, 2).reshape(1, LANES)
    eye_g = jnp.eye(BRANCH_W // GROUP_W, dtype=F32)

    w_in, w_sp, w_br, w_merge, w_out = (w.astype(BF16) for w in (w_in, w_sp, w_br, w_merge, w_out))
    xc, xl = ctx, x
    for l in range(depth):
        nw = norm_w[l].reshape(1, d)
        wf_bd = jnp.einsum('gh,gcd->gchd', eye_g, w_fnet[l]).reshape(BRANCH_W, BRANCH_W)
        b_sp_t = jnp.repeat(b_sp[l].T, GROUP_W, axis=1)
        (qat, ka, vat, za, qdt, kd, vdt, zd, fb, fbc, zb, yc) = _projection(
            l, xc, xl, mods, nw, w_in, cos_t, sin_t,
            pair(qn_a[l]), pair(kn_a[l]), pair(qn_d[l]), pair(kn_d[l]),
            w_sp, b_sp_t)
        ya = _attention_a(qat, ka, vat, za, ctx_len)
        yd = _attention_d(sink_d[l], qdt, kd, vdt, zd, ctx_len)
        pr, pi, prc, pic = _fnet_dft(fb, fbc, m1, m3, mc)
        *xc_new, xl = _merge(l, xc, xl, mods, nw, ya, yd, pr, pi, prc, pic, zb, yc, chan,
                             wf_bd.astype(BF16), w_br, w_merge, b_merge[l].reshape(1, -1), w_out,
                             skip_ctx=(l == depth - 1))
        xc = xc_new[0] if xc_new else None
    return xl
```

```python
import functools

import numpy as np
import jax
import jax.numpy as jnp
from jax import lax
from jax.experimental import pallas as pl
from jax.experimental.pallas import tpu as pltpu

F32 = jnp.float32
BF16 = jnp.bfloat16

GRID_W = 64
HEAD_DIM = 64
BRANCH_W = 256
KV_W = 128
N_BRANCH = 4
GROUP_W = 64
CHUNK = 128
Q_BLOCK = 128
WINDOW = 128
ROPE_BASE = 10000.0
EPS = 1e-6
LOG2_E = float(np.log2(np.e))
LANES = 128
TOKEN_TILE = 256
V_ROWS = 80
ATTN_A_KEYS = 512
ATTN_A_UNROLL = 16
ATTN_D_TILES = 3
PROJ_TILES = 3
DFT_BLOCK = 8
DFT_INNER = 64
VMEM_LIMIT = 56 * 1024 * 1024

_IN_SIZES = (BRANCH_W, KV_W, KV_W, BRANCH_W, BRANCH_W, KV_W, KV_W, BRANCH_W,
             BRANCH_W, BRANCH_W, BRANCH_W, BRANCH_W, BRANCH_W)
_IN_OFF = tuple(int(v) for v in np.cumsum((0,) + _IN_SIZES))
IN_W = _IN_OFF[-1]
(_AQ, _AK, _AV, _AZ, _DQ, _DK, _DV, _DZ, _BF, _BZ, _CU, _CV, _CZ) = _IN_OFF[:-1]


def _silu(z):
    return z * jax.nn.sigmoid(z)


def _gelu(x):
    return 0.5 * x * (1.0 + lax.erf(x * np.float32(np.sqrt(0.5))))


def _dot(a, b):
    return jnp.dot(a, b, preferred_element_type=F32)


def _lane_lo(width=LANES):
    lane = lax.broadcasted_iota(jnp.int32, (1, width), 1)
    return (lane % LANES) < HEAD_DIM


def _mod_kernel(cv_ref, w_ref, b_ref, o_ref):
    s = _silu(cv_ref[...])
    o_ref[0] = _dot(s.astype(BF16), w_ref[0].astype(BF16)) + b_ref[0]


def _modulation(cvecs, w_ada, b_ada):
    depth, d, d3 = w_ada.shape
    nblk = d3 // d
    return pl.pallas_call(
        _mod_kernel,
        grid=(depth, nblk),
        in_specs=[pl.BlockSpec((8, d), lambda l, n: (0, 0)),
                  pl.BlockSpec((1, d, d), lambda l, n: (l, 0, n)),
                  pl.BlockSpec((1, 1, d), lambda l, n: (l, 0, n))],
        out_specs=pl.BlockSpec((1, 8, d), lambda l, n: (l, 0, n)),
        out_shape=jax.ShapeDtypeStruct((depth, 8, d3), F32),
        name="modulation",
    )(cvecs, w_ada, b_ada.reshape(depth, 1, d3))


def _modulated_norm(x, mod_ref, nw_ref, row, d):
    m = mod_ref[pl.ds(row, 1), :]
    sh, sc = m[:, :d], m[:, d:2 * d]
    ms = jnp.mean(x * x, axis=-1, keepdims=True)
    xn = x * lax.rsqrt(ms + EPS) * nw_ref[...]
    return xn * (1.0 + sc) + sh, m[:, 2 * d:]


def _head_norm_rope(xs, wn, cos, sin, scale):
    sq = xs * xs
    lo = _lane_lo()
    ms = jnp.where(lo, jnp.sum(jnp.where(lo, sq, 0.0), axis=1, keepdims=True),
                   jnp.sum(jnp.where(lo, 0.0, sq), axis=1, keepdims=True)) * (1.0 / HEAD_DIM)
    y = xs * lax.rsqrt(ms + EPS) * wn
    lane = lax.broadcasted_iota(jnp.int32, (1, LANES), 1)
    first = (lane % 32) < 16
    sw = jnp.where(first, pltpu.roll(y, LANES - 16, axis=1), pltpu.roll(y, 16, axis=1))
    y = y * cos + sw * sin
    return y * scale if scale != 1.0 else y


def _proj_kernel(xc_ref, *refs, n_batch, tiles):
    xl_refs, refs = refs[:tiles], refs[tiles:]
    (mod_ref, nw_ref, win_ref, cos_ref, sin_ref, qna_ref, kna_ref, qnd_ref, knd_ref, wsp_ref,
     bsp_ref, qat_ref, ka_ref, vat_ref, za_ref, qdt_ref, kd_ref, vdt_ref, zd_ref,
     fb_ref, fbc_ref, zb_ref, yc_ref) = refs
    b, j = pl.program_id(0), pl.program_id(1)
    d = xc_ref.shape[-1]
    tm = xc_ref.shape[1]
    q_scale = HEAD_DIM ** -0.5 * LOG2_E
    group = lax.broadcasted_iota(jnp.int32, (1, BRANCH_W), 1) // GROUP_W
    sub = DFT_BLOCK

    for u in range(tiles):
        tok = slice(u * tm, (u + 1) * tm)
        if u == 0:
            x = jnp.where(j == 0, xc_ref[0], xl_refs[0][0])
            row = jnp.where(j == 0, n_batch, b)
        else:
            x, row = xl_refs[u][0], b
        h, _ = _modulated_norm(x, mod_ref, nw_ref, row, d)
        hb = h.astype(BF16)
        pc = _dot(hb, win_ref[:, _CU:])
        p = _dot(hb, win_ref[:, :_CU])

        gu = _gelu(pc[:, :BRANCH_W])
        gv = _gelu(pc[:, BRANCH_W:2 * BRANCH_W]).astype(BF16)
        zc = _silu(pc[:, 2 * BRANCH_W:])
        for c in range(tm // CHUNK):
            rows = slice(c * CHUNK, (c + 1) * CHUNK)
            sp = bsp_ref[...]
            for g in range(BRANCH_W // GROUP_W):
                sp = sp + jnp.where(group == g, _dot(wsp_ref[g], gv[rows]), 0.0)
            yc_ref[0, u * tm + c * CHUNK:u * tm + (c + 1) * CHUNK, :] = (
                gu[rows] * sp * zc[rows]).astype(BF16)

        cos, sin = cos_ref[tok, :], sin_ref[tok, :]

        def attn_branch(q0, k0, v0, z0, qn_ref, kn_ref, qt_ref, k_ref, vt_ref, z_ref):
            q = [_head_norm_rope(p[:, q0 + s * LANES:q0 + (s + 1) * LANES], qn_ref[...],
                                 cos, sin, q_scale) for s in range(BRANCH_W // LANES)]
            qt_ref[0, :, tok] = jnp.concatenate(q, axis=1).T.astype(BF16)
            k_ref[0, tok, :] = _head_norm_rope(p[:, k0:k0 + KV_W], kn_ref[...], cos, sin,
                                               1.0).astype(BF16)
            vt = p[:, v0:v0 + KV_W].T
            ones = jnp.ones((V_ROWS - HEAD_DIM, tm), F32)
            vt_ref[0, :, tok] = jnp.concatenate([vt[:HEAD_DIM], ones, vt[HEAD_DIM:], ones],
                                                axis=0).astype(BF16)
            z_ref[0, tok, :] = _silu(p[:, z0:z0 + BRANCH_W])

        attn_branch(_AQ, _AK, _AV, _AZ, qna_ref, kna_ref, qat_ref, ka_ref, vat_ref, za_ref)
        attn_branch(_DQ, _DK, _DV, _DZ, qnd_ref, knd_ref, qdt_ref, kd_ref, vdt_ref, zd_ref)

        f = p[:, _BF:_BF + BRANCH_W]
        per_hi = tm // (DFT_INNER // sub)
        for hi in range(DFT_INNER // sub):
            rows = jnp.concatenate([f[n1 * DFT_INNER + hi * sub:n1 * DFT_INNER + (hi + 1) * sub]
                                    for n1 in range(tm // DFT_INNER)], axis=0)
            for half in range(BRANCH_W // LANES):
                fb_ref[0, half, hi, u * per_hi:(u + 1) * per_hi, :] = (
                    rows[:, half * LANES:(half + 1) * LANES])
        if u == 0:
            @pl.when(j == 0)
            def _():
                fbc_ref[0] = f

        zb_ref[0, tok, :] = _silu(p[:, _BZ:_BZ + BRANCH_W])


def _projection(layer, xc, xl, mod, nw, w_in, cos_t, sin_t, qna, kna, qnd, knd, w_sp, b_sp_t):
    nb, seq, d = xl.shape
    tm = TOKEN_TILE
    tiles = PROJ_TILES
    step = tiles * tm
    t = seq + xc.shape[1]
    tok = lambda w, dt: jax.ShapeDtypeStruct((nb, t, w), dt)
    tok_spec = lambda w: pl.BlockSpec((1, step, w), lambda b, j: (b, j, 0))
    tr = lambda w: jax.ShapeDtypeStruct((nb, w, t), BF16)
    tr_spec = lambda w: pl.BlockSpec((1, w, step), lambda b, j: (b, 0, j))
    const = lambda a: pl.BlockSpec(a.shape, lambda b, j: (0,) * a.ndim)
    of_layer = lambda a: pl.BlockSpec((None,) + a.shape[1:],
                                      lambda b, j: (layer,) + (0,) * (a.ndim - 1))
    lat_tile = lambda u: pl.BlockSpec(
        (1, tm, d), lambda b, j: (b, jnp.clip(tiles * j + u - 1, 0, seq // tm - 1), 0))
    attn_specs = [tr_spec(BRANCH_W), tok_spec(KV_W), tr_spec(2 * V_ROWS), tok_spec(BRANCH_W)]
    attn_shapes = [tr(BRANCH_W), tok(KV_W, BF16), tr(2 * V_ROWS), tok(BRANCH_W, F32)]
    n_hi = DFT_INNER // DFT_BLOCK
    return pl.pallas_call(
        functools.partial(_proj_kernel, n_batch=nb, tiles=tiles),
        grid=(nb, t // step),
        in_specs=[pl.BlockSpec((1, tm, d), lambda b, j: (b, 0, 0))]
                 + [lat_tile(u) for u in range(tiles)]
                 + [of_layer(mod), const(nw), of_layer(w_in),
                    pl.BlockSpec((step, LANES), lambda b, j: (j, 0)),
                    pl.BlockSpec((step, LANES), lambda b, j: (j, 0)),
                    const(qna), const(kna), const(qnd), const(knd), of_layer(w_sp),
                    const(b_sp_t)],
        out_specs=attn_specs * 2 + [
            pl.BlockSpec((1, BRANCH_W // LANES, n_hi, step // n_hi, LANES),
                         lambda b, j: (b, 0, 0, j, 0)),
            pl.BlockSpec((1, tm, BRANCH_W), lambda b, j: (b, 0, 0)),
            tok_spec(BRANCH_W), tok_spec(BRANCH_W)],
        out_shape=attn_shapes * 2 + [
            jax.ShapeDtypeStruct((nb, BRANCH_W // LANES, n_hi, t // n_hi, LANES), F32),
            jax.ShapeDtypeStruct((nb, tm, BRANCH_W), F32),
            tok(BRANCH_W, F32), tok(BRANCH_W, BF16)],
        compiler_params=pltpu.CompilerParams(vmem_limit_bytes=VMEM_LIMIT),
        name="projection",
    )(xc, *([xl] * tiles), mod, nw, w_in, cos_t, sin_t, qna, kna, qnd, knd, w_sp, b_sp_t)


def _group_queries(qt_ref, g, tokens=slice(None)):
    r0 = g * LANES
    qrow = jnp.concatenate([qt_ref[0, r0:r0 + HEAD_DIM, tokens],
                            qt_ref[0, r0 + HEAD_DIM:r0 + LANES, tokens]], axis=1)
    zeros = jnp.zeros_like(qrow)
    return jnp.concatenate([qrow, zeros] if g == 0 else [zeros, qrow], axis=0)


def _store_heads(o_ref, z_ref, g, o_t, tokens=slice(None)):
    tq = o_t.shape[1] // 2
    o = jnp.concatenate([o_t[:, :tq], o_t[:, tq:]], axis=0).T
    cols = slice(g * LANES, (g + 1) * LANES)
    o_ref[0, tokens, cols] = (o * z_ref[0, tokens, cols]).astype(BF16)


def _attn_a_kernel(qt_ref, k_ref, vt_ref, z_ref, o_ref, s_ref, smax_ref, m_ref, acc_ref,
                   *, tk, ctx_len):
    j = pl.program_id(1)
    tq = qt_ref.shape[2]
    t = k_ref.shape[1]
    n_lat = (t - ctx_len) // tk
    groups = range(2)
    qt_g = [_group_queries(qt_ref, g) for g in groups]

    def scores(g, start, size):
        return _dot(k_ref[0, pl.ds(start, size), :], qt_g[g])

    def values(g, start, size):
        return vt_ref[0, g * V_ROWS:(g + 1) * V_ROWS, pl.ds(start, size)]

    def lat_start(c):
        return pl.multiple_of(ctx_len + c * tk, LANES)

    def produce(g, slot, c):
        s = scores(g, lat_start(c), tk)
        smax_ref[slot, g] = jnp.max(s, axis=0, keepdims=True)
        s_ref[slot, g] = s

    s_ctx = [scores(g, 0, ctx_len) for g in groups]
    for g in groups:
        produce(g, 0, 0)
    for g in groups:
        m = jnp.max(s_ctx[g], axis=0, keepdims=True)
        m_ref[g] = m
        acc_ref[g] = _dot(values(g, 0, ctx_len), jnp.exp2(s_ctx[g] - m).astype(BF16))

    def consume(g, slot, c):
        m = m_ref[g]
        m_new = jnp.maximum(m, smax_ref[slot, g])
        pr = jnp.exp2(s_ref[slot, g] - m_new).astype(BF16)
        acc_ref[g] = jnp.exp2(m - m_new) * acc_ref[g] + _dot(values(g, lat_start(c), tk), pr)
        m_ref[g] = m_new

    @pl.when(j > 0)
    def _():
        unroll = ATTN_A_UNROLL

        def body(i, carry):
            for k in range(unroll):
                c = unroll * i + k
                for g in groups:
                    produce(g, (k + 1) % 2, c + 1)
                    consume(g, k % 2, c)
            return carry

        n_loop = (n_lat - 1) // unroll
        if n_loop:
            lax.fori_loop(0, n_loop, body, 0)
        for c in range(n_loop * unroll, n_lat):
            for g in groups:
                if c + 1 < n_lat:
                    produce(g, (c + 1) % 2, c + 1)
                consume(g, c % 2, c)

    for g in groups:
        acc = acc_ref[g]
        _store_heads(o_ref, z_ref, g, acc[:HEAD_DIM] / acc[HEAD_DIM:HEAD_DIM + 1])


def _attention_a(qt, k, vt, z, ctx_len):
    nb, t, _ = k.shape
    tq = TOKEN_TILE
    tk = ATTN_A_KEYS
    return pl.pallas_call(
        functools.partial(_attn_a_kernel, tk=tk, ctx_len=ctx_len),
        grid=(nb, t // tq),
        in_specs=[pl.BlockSpec((1, BRANCH_W, tq), lambda b, j: (b, 0, j)),
                  pl.BlockSpec((1, t, KV_W), lambda b, j: (b, 0, 0)),
                  pl.BlockSpec((1, 2 * V_ROWS, t), lambda b, j: (b, 0, 0)),
                  pl.BlockSpec((1, tq, BRANCH_W), lambda b, j: (b, j, 0))],
        out_specs=pl.BlockSpec((1, tq, BRANCH_W), lambda b, j: (b, j, 0)),
        out_shape=jax.ShapeDtypeStruct((nb, t, BRANCH_W), BF16),
        scratch_shapes=[pltpu.VMEM((2, 2, tk, 2 * tq), F32),
                        pltpu.VMEM((2, 2, 1, 2 * tq), F32),
                        pltpu.VMEM((2, 1, 2 * tq), F32),
                        pltpu.VMEM((2, V_ROWS, 2 * tq), F32)],
        compiler_params=pltpu.CompilerParams(vmem_limit_bytes=VMEM_LIMIT),
        name="attention_a",
    )(qt, k, vt, z)


def _attn_d_kernel(sink_ref, qt_ref, k_ref, vt_ref, z_ref, o_ref, *, ctx_len, tq):
    tiles = qt_ref.shape[2] // tq
    t = k_ref.shape[1]
    span = tq + 2 * WINDOW
    first = lax.broadcasted_iota(jnp.int32, (1, 2 * tq), 1) < tq
    rel = (lax.broadcasted_iota(jnp.int32, (span, 2 * tq), 0)
           - lax.broadcasted_iota(jnp.int32, (span, 2 * tq), 1) % tq)

    def scores(u):
        tile = pl.program_id(1) * tiles + u
        q0 = tile * tq
        start = pl.multiple_of(jnp.clip(q0 - WINDOW, ctx_len, t - span), LANES)
        mask = (jnp.abs(rel + (start - q0)) <= WINDOW) & (tile > 0)
        sc = []
        for g in range(2):
            qt_g = _group_queries(qt_ref, g, slice(u * tq, (u + 1) * tq))
            sc.append((_dot(k_ref[0, :ctx_len, :], qt_g),
                       _dot(k_ref[0, pl.ds(start, span), :], qt_g)))
        return start, mask, sc

    def finish(u, start, mask, sc):
        for g in range(2):
            vrows = slice(g * V_ROWS, (g + 1) * V_ROWS)
            s_ctx = sc[g][0]
            s_loc = jnp.where(mask, sc[g][1], -jnp.inf)
            sink = jnp.where(first, sink_ref[2 * g], sink_ref[2 * g + 1]) * LOG2_E
            m = jnp.maximum(jnp.maximum(jnp.max(s_ctx, axis=0, keepdims=True),
                                        jnp.max(s_loc, axis=0, keepdims=True)), sink)
            acc = (_dot(vt_ref[0, vrows, :ctx_len], jnp.exp2(s_ctx - m).astype(BF16))
                   + _dot(vt_ref[0, vrows, pl.ds(start, span)], jnp.exp2(s_loc - m).astype(BF16)))
            l = acc[HEAD_DIM:HEAD_DIM + 1] + jnp.exp2(sink - m)
            _store_heads(o_ref, z_ref, g, acc[:HEAD_DIM] / l, slice(u * tq, (u + 1) * tq))

    pending = scores(0)
    for u in range(tiles):
        ready, pending = pending, (scores(u + 1) if u + 1 < tiles else None)
        finish(u, *ready)


def _attention_d(sink, qt, k, vt, z, ctx_len):
    nb, t, _ = k.shape
    tq = TOKEN_TILE
    step = ATTN_D_TILES * tq
    return pl.pallas_call(
        functools.partial(_attn_d_kernel, ctx_len=ctx_len, tq=tq),
        grid=(nb, t // step),
        in_specs=[pl.BlockSpec(memory_space=pltpu.SMEM),
                  pl.BlockSpec((1, BRANCH_W, step), lambda b, j: (b, 0, j)),
                  pl.BlockSpec((1, t, KV_W), lambda b, j: (b, 0, 0)),
                  pl.BlockSpec((1, 2 * V_ROWS, t), lambda b, j: (b, 0, 0)),
                  pl.BlockSpec((1, step, BRANCH_W), lambda b, j: (b, j, 0))],
        out_specs=pl.BlockSpec((1, step, BRANCH_W), lambda b, j: (b, j, 0)),
        out_shape=jax.ShapeDtypeStruct((nb, t, BRANCH_W), BF16),
        compiler_params=pltpu.CompilerParams(vmem_limit_bytes=VMEM_LIMIT),
        name="attention_d",
    )(sink, qt, k, vt, z)


def _dft_tables(seq_len, ctx_len):
    n2c = DFT_INNER
    n1c = seq_len // n2c
    scale = 1.0 / np.sqrt(float(seq_len) * GROUP_W)
    k1 = np.arange(n1c)[:, None]
    n1 = np.arange(n1c)[None, :]
    m1 = np.zeros((n2c, 2 * n1c, n1c), np.float64)
    for n2 in range(n2c):
        ang = -2.0 * np.pi * (k1 * n1 / n1c + n2 * k1 / seq_len)
        m1[n2, :n1c] = np.cos(ang) * scale
        m1[n2, n1c:] = np.sin(ang) * scale
    ang3 = 2.0 * np.pi * np.outer(np.arange(n2c), np.arange(n2c)) / n2c
    c3, s3 = np.cos(ang3), np.sin(ang3)
    m3 = np.block([[c3, s3], [-s3, c3]])
    cscale = 1.0 / np.sqrt(float(ctx_len) * GROUP_W)
    angc = 2.0 * np.pi * np.outer(np.arange(ctx_len), np.arange(ctx_len)) / ctx_len
    mc = np.concatenate([np.cos(angc), -np.sin(angc)], axis=0) * cscale
    angg = 2.0 * np.pi * np.outer(np.arange(GROUP_W), np.arange(GROUP_W)) / GROUP_W
    eye = np.eye(BRANCH_W // GROUP_W)
    chan = np.concatenate([np.kron(eye, np.cos(angg)), np.kron(eye, np.sin(angg))], axis=0)
    return _stacked(m1), _stacked(m3), _split(mc), _split(chan)


def _stacked(a):
    hi, lo = _split(a)
    return jnp.concatenate([hi, lo, hi], axis=-1)


def _dot3_stacked(m3, x):
    x_hi = x.astype(BF16)
    x_lo = (x - x_hi.astype(F32)).astype(BF16)
    return _dot(m3, jnp.concatenate([x_hi, x_hi, x_lo], axis=0))


def _split(a):
    a32 = jnp.asarray(a.astype(np.float32))
    hi = a32.astype(BF16)
    return hi, (a32 - hi.astype(F32)).astype(BF16)


def _dot3_left(m_hi, m_lo, x):
    x_hi = x.astype(BF16)
    x_lo = (x - x_hi.astype(F32)).astype(BF16)
    return _dot(m_hi, x_hi) + _dot(m_lo, x_hi) + _dot(m_hi, x_lo)


def _dot3_right(x, m_hi, m_lo):
    x_hi = x.astype(BF16)
    x_lo = (x - x_hi.astype(F32)).astype(BF16)
    return _dot(x_hi, m_hi) + _dot(x_hi, m_lo) + _dot(x_lo, m_hi)


def _fnet_stage1_kernel(f_ref, m_ref, br_ref, bi_ref):
    sub = DFT_BLOCK
    halves = range(f_ref.shape[1])
    n1c = br_ref.shape[3] // sub
    skip = f_ref.shape[3] - br_ref.shape[3]
    for q in range(sub):
        rows = pl.ds(q, n1c, stride=sub)
        x = jnp.concatenate([f_ref[0, h, 0, pl.ds(skip + q, n1c, stride=sub), :] for h in halves],
                            axis=1)
        res = _dot3_stacked(m_ref[q], x)
        for h in halves:
            br_ref[0, h, 0, rows, :] = res[:n1c, h * LANES:(h + 1) * LANES]
            bi_ref[0, h, 0, rows, :] = res[n1c:, h * LANES:(h + 1) * LANES]


def _fnet_stage2_kernel(br_ref, bi_ref, fc_ref, m_ref, mch_ref, mcl_ref,
                        pr_ref, pi_ref, prc_ref, pic_ref):
    n2c = DFT_INNER
    sub = DFT_BLOCK
    ctx_len = fc_ref.shape[1]

    @pl.when(pl.program_id(1) == 0)
    def _():
        res = _dot3_left(mch_ref[...], mcl_ref[...], fc_ref[0])
        prc_ref[0] = res[:ctx_len]
        pic_ref[0] = res[ctx_len:]

    halves = range(br_ref.shape[1])
    for q in range(sub):
        xin = jnp.concatenate(
            [jnp.concatenate([ref[0, h, hi, q * sub:(q + 1) * sub, :] for h in halves], axis=1)
             for ref in (br_ref, bi_ref) for hi in range(n2c // sub)], axis=0)
        res = _dot3_stacked(m_ref[...], xin)
        rows = pl.ds(q, n2c, stride=sub)
        for h in halves:
            pr_ref[0, h, 0, rows, :] = res[:n2c, h * LANES:(h + 1) * LANES]
            pi_ref[0, h, 0, rows, :] = res[n2c:, h * LANES:(h + 1) * LANES]


def _fnet_dft(fb, fb_ctx, m1, m3, mc):
    nb, nh, n_hi, rows_in, w = fb.shape
    ctx_len = fb_ctx.shape[1]
    n2c = DFT_INNER
    sub = DFT_BLOCK
    rows = rows_in - ctx_len // n_hi
    n1c = rows // sub
    blk0 = pl.BlockSpec((1, nh, 1, rows_in, w), lambda b, i: (b, 0, i, 0, 0))
    blk1 = pl.BlockSpec((1, nh, 1, rows, w), lambda b, i: (b, 0, i, 0, 0))
    tab1 = pl.BlockSpec((sub, 2 * n1c, 3 * n1c), lambda b, i: (i, 0, 0))
    b4 = jax.ShapeDtypeStruct((nb, nh, n_hi, rows, w), F32)
    br, bi = pl.pallas_call(
        _fnet_stage1_kernel,
        grid=(nb, n_hi),
        in_specs=[blk0, tab1],
        out_specs=[blk1, blk1],
        out_shape=[b4, b4],
        compiler_params=pltpu.CompilerParams(vmem_limit_bytes=VMEM_LIMIT),
        name="fnet_stage1",
    )(fb, m1)
    blk2 = pl.BlockSpec((1, nh, n_hi, sub * sub, w), lambda b, i: (b, 0, 0, i, 0))
    oblk = pl.BlockSpec((1, nh, 1, n2c * sub, w), lambda b, i: (b, 0, i, 0, 0))
    cblk = pl.BlockSpec((1, ctx_len, nh * w), lambda b, i: (b, 0, 0))
    const = lambda a: pl.BlockSpec(a.shape, lambda b, i: (0,) * a.ndim)
    p4 = jax.ShapeDtypeStruct((nb, nh, n1c // sub, n2c * sub, w), F32)
    pc = jax.ShapeDtypeStruct((nb, ctx_len, nh * w), F32)
    return pl.pallas_call(
        _fnet_stage2_kernel,
        grid=(nb, n1c // sub),
        in_specs=[blk2, blk2, cblk, const(m3), const(mc[0]), const(mc[1])],
        out_specs=[oblk, oblk, cblk, cblk],
        out_shape=[p4, p4, pc, pc],
        compiler_params=pltpu.CompilerParams(vmem_limit_bytes=VMEM_LIMIT),
        name="fnet_stage2",
    )(br, bi, fb_ctx, m3, *mc)


def _merge_kernel(xc_ref, xl_ref, mod_ref, nw_ref, ya_ref, yd_ref, pr_ref, pi_ref, prc_ref,
                  pic_ref, zb_ref, yc_ref, chh_ref, chl_ref, wf_ref, wbr_ref, wm_ref, bm_ref,
                  wo_ref, *out_refs, n_batch, tile_off):
    b, j = pl.program_id(0), pl.program_id(1)
    d = xl_ref.shape[-1]
    is_ctx = j + tile_off == 0
    x = jnp.where(is_ctx, xc_ref[0], xl_ref[0]) if tile_off == 0 else xl_ref[0]
    row = jnp.where(is_ctx, n_batch, b)
    h, gate = _modulated_norm(x, mod_ref, nw_ref, row, d)
    def latent_rows(ref):
        sub = DFT_BLOCK
        return jnp.concatenate(
            [jnp.concatenate([ref[0, h, kb, k2 * sub:(k2 + 1) * sub, :]
                              for k2 in range(ref.shape[3] // sub)
                              for kb in range(ref.shape[2])], axis=0)
             for h in range(ref.shape[1])], axis=1)

    pc = jnp.concatenate([jnp.where(is_ctx, prc_ref[0], latent_rows(pr_ref)),
                          jnp.where(is_ctx, pic_ref[0], latent_rows(pi_ref))], axis=1)
    yb = _dot(_dot3_right(pc, chh_ref[...], chl_ref[...]).astype(BF16), wf_ref[...]) * zb_ref[0]
    branches = (ya_ref[0], yd_ref[0], yb.astype(BF16), yc_ref[0])
    hb = h.astype(BF16)
    acc = jnp.zeros((x.shape[0], d), F32)
    for r in range(N_BRANCH):
        g = jax.nn.sigmoid(_dot(hb, wm_ref[:, r * d:(r + 1) * d]) + bm_ref[:, r * d:(r + 1) * d])
        acc = acc + g * _dot(branches[r], wbr_ref[r])
    out = x + gate * _dot(acc.astype(BF16), wo_ref[...])
    out_refs[-1][0] = out
    if tile_off == 0:
        @pl.when(is_ctx)
        def _():
            out_refs[0][0] = out


def _merge(layer, xc, xl, mod, nw, ya, yd, pr, pi, prc, pic, zb, yc, chan, wf_bd, w_br, w_merge,
           b_merge, w_out, skip_ctx):
    nb, seq, d = xl.shape
    tm = TOKEN_TILE
    off = 1 if skip_ctx else 0
    nt = (seq + xc.shape[1]) // tm - off
    tok_spec = lambda w: pl.BlockSpec((1, tm, w), lambda b, j: (b, j + off, 0))
    ctx_tile = pl.BlockSpec((1, tm, d), lambda b, j: (b, 0, 0))
    lat_tile = pl.BlockSpec((1, tm, d), lambda b, j: (b, jnp.maximum(j + off - 1, 0), 0))
    xc_shape = jax.ShapeDtypeStruct(xc.shape, F32)
    xl_shape = jax.ShapeDtypeStruct(xl.shape, F32)
    lat_spec = lambda a: pl.BlockSpec((1, a.shape[1], a.shape[2], tm // a.shape[2], a.shape[4]),
                                      lambda b, j: (b, 0, 0, jnp.maximum(j + off - 1, 0), 0))
    ctx_spec = lambda a: pl.BlockSpec((1,) + a.shape[1:], lambda b, j: (b, 0, 0))
    const = lambda a: pl.BlockSpec(a.shape, lambda b, j: (0,) * a.ndim)
    of_layer = lambda a: pl.BlockSpec((None,) + a.shape[1:],
                                      lambda b, j: (layer,) + (0,) * (a.ndim - 1))
    return pl.pallas_call(
        functools.partial(_merge_kernel, n_batch=nb, tile_off=off),
        grid=(nb, nt),
        in_specs=[ctx_tile, lat_tile, of_layer(mod), const(nw), tok_spec(BRANCH_W),
                  tok_spec(BRANCH_W), lat_spec(pr), lat_spec(pi), ctx_spec(prc), ctx_spec(pic),
                  tok_spec(BRANCH_W), tok_spec(BRANCH_W), const(chan[0]), const(chan[1]),
                  const(wf_bd), of_layer(w_br), of_layer(w_merge), const(b_merge),
                  of_layer(w_out)],
        out_specs=[lat_tile] if skip_ctx else [ctx_tile, lat_tile],
        out_shape=[xl_shape] if skip_ctx else [xc_shape, xl_shape],
        compiler_params=pltpu.CompilerParams(vmem_limit_bytes=VMEM_LIMIT),
        name="merge",
    )(xc, xl, mod, nw, ya, yd, pr, pi, prc, pic, zb, yc, *chan, wf_bd, w_br, w_merge, b_merge,
      w_out)


def _rope_tables(seq_len, ctx_len):
    t = jnp.arange(seq_len, dtype=jnp.int32)
    r = (t // GRID_W).astype(F32)
    col = (t % GRID_W).astype(F32)
    nf = HEAD_DIM // 4
    inv = ROPE_BASE ** (-jnp.arange(nf, dtype=F32) / nf)
    ar = r[:, None] * inv[None, :]
    ac = col[:, None] * inv[None, :]
    cr, sr, cc, sc = jnp.cos(ar), jnp.sin(ar), jnp.cos(ac), jnp.sin(ac)
    cos_h = jnp.concatenate([cr, cr, cc, cc], axis=1)
    sin_h = jnp.concatenate([-sr, sr, -sc, sc], axis=1)
    cos_t = jnp.concatenate([jnp.ones((ctx_len, HEAD_DIM), F32), cos_h], axis=0)
    sin_t = jnp.concatenate([jnp.zeros((ctx_len, HEAD_DIM), F32), sin_h], axis=0)
    return jnp.tile(cos_t, (1, 2)), jnp.tile(sin_t, (1, 2))


def kernel(x, c, ctx, c_ctx, norm_w, w_ada, b_ada, w_in, qn_a, kn_a, qn_d, kn_d, sink_d,
           w_fnet, w_sp, b_sp, w_br, w_merge, b_merge, w_out):
    nb, seq, d = x.shape
    ctx_len = ctx.shape[1]
    depth = norm_w.shape[0]
    assert ctx_len == TOKEN_TILE and seq % (2 * ATTN_A_KEYS) == 0 and nb < 8
    assert (seq + ctx_len) % (ATTN_D_TILES * TOKEN_TILE) == 0
    assert (seq + ctx_len) % (PROJ_TILES * TOKEN_TILE) == 0

    cvecs = jnp.zeros((8, d), F32).at[:nb].set(c).at[nb].set(c_ctx)
    mods = _modulation(cvecs, w_ada, b_ada)

    cos_t, sin_t = _rope_tables(seq, ctx_len)
    m1, m3, mc, chan = _dft_tables(seq, ctx_len)
    pair = lambda w: jnp.tile(w, 2).reshape(1, LANES)
    eye_g = jnp.eye(BRANCH_W // GROUP_W, dtype=F32)

    w_in, w_sp, w_br, w_merge, w_out = (w.astype(BF16) for w in (w_in, w_sp, w_br, w_merge, w_out))
    xc, xl = ctx, x
    for l in range(depth):
        nw = norm_w[l].reshape(1, d)
        wf_bd = jnp.einsum('gh,gcd->gchd', eye_g, w_fnet[l]).reshape(BRANCH_W, BRANCH_W)
        b_sp_t = jnp.repeat(b_sp[l].T, GROUP_W, axis=1)
        (qat, ka, vat, za, qdt, kd, vdt, zd, fb, fbc, zb, yc) = _projection(
            l, xc, xl, mods, nw, w_in, cos_t, sin_t,
            pair(qn_a[l]), pair(kn_a[l]), pair(qn_d[l]), pair(kn_d[l]),
            w_sp, b_sp_t)
        ya = _attention_a(qat, ka, vat, za, ctx_len)
        yd = _attention_d(sink_d[l], qdt, kd, vdt, zd, ctx_len)
        pr, pi, prc, pic = _fnet_dft(fb, fbc, m1, m3, mc)
        *xc_new, xl = _merge(l, xc, xl, mods, nw, ya, yd, pr, pi, prc, pic, zb, yc, chan,
                             wf_bd.astype(BF16), w_br, w_merge, b_merge[l].reshape(1, -1), w_out,
                             skip_ctx=(l == depth - 1))
        xc = xc_new[0] if xc_new else None
    return xl
```

```python
import functools

import numpy as np
import jax
import jax.numpy as jnp
from jax import lax
from jax.experimental import pallas as pl
from jax.experimental.pallas import tpu as pltpu

F32 = jnp.float32
BF16 = jnp.bfloat16

GRID_W = 64
HEAD_DIM = 64
BRANCH_W = 256
KV_W = 128
N_BRANCH = 4
GROUP_W = 64
CHUNK = 128
Q_BLOCK = 128
WINDOW = 128
ROPE_BASE = 10000.0
EPS = 1e-6
LOG2_E = float(np.log2(np.e))
LANES = 128
TOKEN_TILE = 256
V_ROWS = 80
ATTN_A_KEYS = 512
ATTN_A_TILES = 3
ATTN_D_TILES = 3
PROJ_TILES = 3
DFT_BLOCK = 8
DFT_INNER = 64
VMEM_LIMIT = 56 * 1024 * 1024

_IN_SIZES = (BRANCH_W, KV_W, KV_W, BRANCH_W, BRANCH_W, KV_W, KV_W, BRANCH_W,
             BRANCH_W, BRANCH_W, BRANCH_W, BRANCH_W, BRANCH_W)
_IN_OFF = tuple(int(v) for v in np.cumsum((0,) + _IN_SIZES))
IN_W = _IN_OFF[-1]
(_AQ, _AK, _AV, _AZ, _DQ, _DK, _DV, _DZ, _BF, _BZ, _CU, _CV, _CZ) = _IN_OFF[:-1]


def _silu(z):
    return z * jax.nn.sigmoid(z)


def _gelu(x):
    return 0.5 * x * (1.0 + lax.erf(x * np.float32(np.sqrt(0.5))))


def _dot(a, b):
    return jnp.dot(a, b, preferred_element_type=F32)


def _lane_lo(width=LANES):
    lane = lax.broadcasted_iota(jnp.int32, (1, width), 1)
    return (lane % LANES) < HEAD_DIM


def _mod_kernel(cv_ref, w_ref, b_ref, o_ref):
    s = _silu(cv_ref[...])
    o_ref[0] = _dot(s.astype(BF16), w_ref[0].astype(BF16)) + b_ref[0]


def _modulation(cvecs, w_ada, b_ada):
    depth, d, d3 = w_ada.shape
    nblk = d3 // d
    return pl.pallas_call(
        _mod_kernel,
        grid=(depth, nblk),
        in_specs=[pl.BlockSpec((8, d), lambda l, n: (0, 0)),
                  pl.BlockSpec((1, d, d), lambda l, n: (l, 0, n)),
                  pl.BlockSpec((1, 1, d), lambda l, n: (l, 0, n))],
        out_specs=pl.BlockSpec((1, 8, d), lambda l, n: (l, 0, n)),
        out_shape=jax.ShapeDtypeStruct((depth, 8, d3), F32),
        name="modulation",
    )(cvecs, w_ada, b_ada.reshape(depth, 1, d3))


def _modulated_norm(x, mod_ref, nw_ref, row, d):
    m = mod_ref[pl.ds(row, 1), :]
    sh, sc = m[:, :d], m[:, d:2 * d]
    ms = jnp.mean(x * x, axis=-1, keepdims=True)
    xn = x * lax.rsqrt(ms + EPS) * nw_ref[...]
    return xn * (1.0 + sc) + sh, m[:, 2 * d:]


def _head_norm_rope(xs, wn, cos, sin, scale):
    sq = xs * xs
    lo = _lane_lo()
    ms = jnp.where(lo, jnp.sum(jnp.where(lo, sq, 0.0), axis=1, keepdims=True),
                   jnp.sum(jnp.where(lo, 0.0, sq), axis=1, keepdims=True)) * (1.0 / HEAD_DIM)
    y = xs * lax.rsqrt(ms + EPS) * wn
    lane = lax.broadcasted_iota(jnp.int32, (1, LANES), 1)
    first = (lane % 32) < 16
    sw = jnp.where(first, pltpu.roll(y, LANES - 16, axis=1), pltpu.roll(y, 16, axis=1))
    y = y * cos + sw * sin
    return y * scale if scale != 1.0 else y


def _proj_kernel(xc_ref, *refs, n_batch, tiles):
    xl_refs, refs = refs[:tiles], refs[tiles:]
    (mod_ref, nw_ref, win_ref, cos_ref, sin_ref, qna_ref, kna_ref, qnd_ref, knd_ref, wsp_ref,
     bsp_ref, qat_ref, ka_ref, vat_ref, za_ref, qdt_ref, kd_ref, vdt_ref, zd_ref,
     fb_ref, fbc_ref, zb_ref, yc_ref) = refs
    b, j = pl.program_id(0), pl.program_id(1)
    d = xc_ref.shape[-1]
    tm = xc_ref.shape[1]
    q_scale = HEAD_DIM ** -0.5 * LOG2_E
    group = lax.broadcasted_iota(jnp.int32, (1, BRANCH_W), 1) // GROUP_W
    sub = DFT_BLOCK

    for u in range(tiles):
        tok = slice(u * tm, (u + 1) * tm)
        if u == 0:
            x = jnp.where(j == 0, xc_ref[0], xl_refs[0][0])
            row = jnp.where(j == 0, n_batch, b)
        else:
            x, row = xl_refs[u][0], b
        h, _ = _modulated_norm(x, mod_ref, nw_ref, row, d)
        hb = h.astype(BF16)
        pc = _dot(hb, win_ref[:, _CU:])
        p = _dot(hb, win_ref[:, :_CU])

        gu = _gelu(pc[:, :BRANCH_W])
        gv = _gelu(pc[:, BRANCH_W:2 * BRANCH_W]).astype(BF16)
        zc = _silu(pc[:, 2 * BRANCH_W:])
        for c in range(tm // CHUNK):
            rows = slice(c * CHUNK, (c + 1) * CHUNK)
            sp = bsp_ref[...]
            for g in range(BRANCH_W // GROUP_W):
                sp = sp + jnp.where(group == g, _dot(wsp_ref[g], gv[rows]), 0.0)
            yc_ref[0, u * tm + c * CHUNK:u * tm + (c + 1) * CHUNK, :] = (
                gu[rows] * sp * zc[rows]).astype(BF16)

        cos, sin = cos_ref[tok, :], sin_ref[tok, :]

        def attn_branch(q0, k0, v0, z0, qn_ref, kn_ref, qt_ref, k_ref, vt_ref, z_ref):
            q = [_head_norm_rope(p[:, q0 + s * LANES:q0 + (s + 1) * LANES], qn_ref[...],
                                 cos, sin, q_scale) for s in range(BRANCH_W // LANES)]
            qt_ref[0, :, tok] = jnp.concatenate(q, axis=1).T.astype(BF16)
            k_ref[0, tok, :] = _head_norm_rope(p[:, k0:k0 + KV_W], kn_ref[...], cos, sin,
                                               1.0).astype(BF16)
            vt = p[:, v0:v0 + KV_W].T
            ones = jnp.ones((V_ROWS - HEAD_DIM, tm), F32)
            vt_ref[0, :, tok] = jnp.concatenate([vt[:HEAD_DIM], ones, vt[HEAD_DIM:], ones],
                                                axis=0).astype(BF16)
            z_ref[0, tok, :] = _silu(p[:, z0:z0 + BRANCH_W])

        attn_branch(_AQ, _AK, _AV, _AZ, qna_ref, kna_ref, qat_ref, ka_ref, vat_ref, za_ref)
        attn_branch(_DQ, _DK, _DV, _DZ, qnd_ref, knd_ref, qdt_ref, kd_ref, vdt_ref, zd_ref)

        f = p[:, _BF:_BF + BRANCH_W]
        per_hi = tm // (DFT_INNER // sub)
        for hi in range(DFT_INNER // sub):
            rows = jnp.concatenate([f[n1 * DFT_INNER + hi * sub:n1 * DFT_INNER + (hi + 1) * sub]
                                    for n1 in range(tm // DFT_INNER)], axis=0)
            for half in range(BRANCH_W // LANES):
                fb_ref[0, half, hi, u * per_hi:(u + 1) * per_hi, :] = (
                    rows[:, half * LANES:(half + 1) * LANES])
        if u == 0:
            @pl.when(j == 0)
            def _():
                fbc_ref[0] = f

        zb_ref[0, tok, :] = _silu(p[:, _BZ:_BZ + BRANCH_W])


def _projection(layer, xc, xl, mod, nw, w_in, cos_t, sin_t, qna, kna, qnd, knd, w_sp, b_sp_t):
    nb, seq, d = xl.shape
    tm = TOKEN_TILE
    tiles = PROJ_TILES
    step = tiles * tm
    t = seq + xc.shape[1]
    tok = lambda w, dt: jax.ShapeDtypeStruct((nb, t, w), dt)
    tok_spec = lambda w: pl.BlockSpec((1, step, w), lambda b, j: (b, j, 0))
    tr = lambda w: jax.ShapeDtypeStruct((nb, w, t), BF16)
    tr_spec = lambda w: pl.BlockSpec((1, w, step), lambda b, j: (b, 0, j))
    const = lambda a: pl.BlockSpec(a.shape, lambda b, j: (0,) * a.ndim)
    of_layer = lambda a: pl.BlockSpec((None,) + a.shape[1:],
                                      lambda b, j: (layer,) + (0,) * (a.ndim - 1))
    lat_tile = lambda u: pl.BlockSpec(
        (1, tm, d), lambda b, j: (b, jnp.clip(tiles * j + u - 1, 0, seq // tm - 1), 0))
    attn_specs = [tr_spec(BRANCH_W), tok_spec(KV_W), tr_spec(2 * V_ROWS), tok_spec(BRANCH_W)]
    attn_shapes = [tr(BRANCH_W), tok(KV_W, BF16), tr(2 * V_ROWS), tok(BRANCH_W, F32)]
    n_hi = DFT_INNER // DFT_BLOCK
    return pl.pallas_call(
        functools.partial(_proj_kernel, n_batch=nb, tiles=tiles),
        grid=(nb, t // step),
        in_specs=[pl.BlockSpec((1, tm, d), lambda b, j: (b, 0, 0))]
                 + [lat_tile(u) for u in range(tiles)]
                 + [of_layer(mod), const(nw), of_layer(w_in),
                    pl.BlockSpec((step, LANES), lambda b, j: (j, 0)),
                    pl.BlockSpec((step, LANES), lambda b, j: (j, 0)),
                    const(qna), const(kna), const(qnd), const(knd), of_layer(w_sp),
                    const(b_sp_t)],
        out_specs=attn_specs * 2 + [
            pl.BlockSpec((1, BRANCH_W // LANES, n_hi, step // n_hi, LANES),
                         lambda b, j: (b, 0, 0, j, 0)),
            pl.BlockSpec((1, tm, BRANCH_W), lambda b, j: (b, 0, 0)),
            tok_spec(BRANCH_W), tok_spec(BRANCH_W)],
        out_shape=attn_shapes * 2 + [
            jax.ShapeDtypeStruct((nb, BRANCH_W // LANES, n_hi, t // n_hi, LANES), F32),
            jax.ShapeDtypeStruct((nb, tm, BRANCH_W), F32),
            tok(BRANCH_W, F32), tok(BRANCH_W, BF16)],
        compiler_params=pltpu.CompilerParams(vmem_limit_bytes=VMEM_LIMIT),
        name="projection",
    )(xc, *([xl] * tiles), mod, nw, w_in, cos_t, sin_t, qna, kna, qnd, knd, w_sp, b_sp_t)


def _group_queries(qt_ref, g, tokens=slice(None)):
    r0 = g * LANES
    qrow = jnp.concatenate([qt_ref[0, r0:r0 + HEAD_DIM, tokens],
                            qt_ref[0, r0 + HEAD_DIM:r0 + LANES, tokens]], axis=1)
    zeros = jnp.zeros_like(qrow)
    return jnp.concatenate([qrow, zeros] if g == 0 else [zeros, qrow], axis=0)


def _store_heads(o_ref, z_ref, g, o_t, tokens=slice(None)):
    tq = o_t.shape[1] // 2
    o = jnp.concatenate([o_t[:, :tq], o_t[:, tq:]], axis=0).T
    cols = slice(g * LANES, (g + 1) * LANES)
    o_ref[0, tokens, cols] = (o * z_ref[0, tokens, cols]).astype(BF16)


def _attn_a_kernel(qt_ref, k_ref, vt_ref, z_ref, o_ref, s_ref, smax_ref, m_ref, acc_ref,
                   *, tk, ctx_len, tq):
    tiles = qt_ref.shape[2] // tq
    t = k_ref.shape[1]
    n_lat = (t - ctx_len) // tk
    groups = range(2)

    def values(g, start, size):
        return vt_ref[0, g * V_ROWS:(g + 1) * V_ROWS, start:start + size]

    class Tile:
        def __init__(self, u):
            self.tok = slice(u * tq, (u + 1) * tq)
            self.par = u % 2

        def scores(self, g, start, size):
            return _dot(k_ref[0, start:start + size, :], self.qt_g[g])

        def produce(self, g, c):
            s = self.scores(g, ctx_len + c * tk, tk)
            smax_ref[c % 2, g] = jnp.max(s, axis=0, keepdims=True)
            s_ref[c % 2, g] = s

        def consume(self, g, c):
            m = m_ref[self.par, g]
            m_new = jnp.maximum(m, smax_ref[c % 2, g])
            pr = jnp.exp2(s_ref[c % 2, g] - m_new).astype(BF16)
            acc_ref[self.par, g] = (jnp.exp2(m - m_new) * acc_ref[self.par, g]
                                    + _dot(values(g, ctx_len + c * tk, tk), pr))
            m_ref[self.par, g] = m_new

        def head_scores(self):
            self.qt_g = [_group_queries(qt_ref, g, self.tok) for g in groups]
            self.s_ctx = [self.scores(g, 0, ctx_len) for g in groups]
            for g in groups:
                self.produce(g, 0)

        def head_finish(self):
            for g in groups:
                m = jnp.max(self.s_ctx[g], axis=0, keepdims=True)
                m_ref[self.par, g] = m
                acc_ref[self.par, g] = _dot(values(g, 0, ctx_len),
                                            jnp.exp2(self.s_ctx[g] - m).astype(BF16))

        def body(self):
            for c in range(n_lat - 1):
                for g in groups:
                    self.produce(g, c + 1)
                    self.consume(g, c)

        def last(self):
            for g in groups:
                self.consume(g, n_lat - 1)

        def store(self):
            for g in groups:
                acc = acc_ref[self.par, g]
                _store_heads(o_ref, z_ref, g, acc[:HEAD_DIM] / acc[HEAD_DIM:HEAD_DIM + 1],
                             self.tok)

    tile = Tile(0)
    tile.head_scores()
    tile.head_finish()

    @pl.when(pl.program_id(1) > 0)
    def _():
        tile.body()
        tile.last()

    for u in range(1, tiles):
        nxt = Tile(u)
        nxt.head_scores()
        if u > 1:
            tile.last()
        tile.store()
        nxt.head_finish()
        nxt.body()
        tile = nxt
    if tiles > 1:
        tile.last()
    tile.store()


def _attention_a(qt, k, vt, z, ctx_len):
    nb, t, _ = k.shape
    tq = TOKEN_TILE
    tk = ATTN_A_KEYS
    step = ATTN_A_TILES * tq
    return pl.pallas_call(
        functools.partial(_attn_a_kernel, tk=tk, ctx_len=ctx_len, tq=tq),
        grid=(nb, t // step),
        in_specs=[pl.BlockSpec((1, BRANCH_W, step), lambda b, j: (b, 0, j)),
                  pl.BlockSpec((1, t, KV_W), lambda b, j: (b, 0, 0)),
                  pl.BlockSpec((1, 2 * V_ROWS, t), lambda b, j: (b, 0, 0)),
                  pl.BlockSpec((1, step, BRANCH_W), lambda b, j: (b, j, 0))],
        out_specs=pl.BlockSpec((1, step, BRANCH_W), lambda b, j: (b, j, 0)),
        out_shape=jax.ShapeDtypeStruct((nb, t, BRANCH_W), BF16),
        scratch_shapes=[pltpu.VMEM((2, 2, tk, 2 * tq), F32),
                        pltpu.VMEM((2, 2, 1, 2 * tq), F32),
                        pltpu.VMEM((2, 2, 1, 2 * tq), F32),
                        pltpu.VMEM((2, 2, V_ROWS, 2 * tq), F32)],
        compiler_params=pltpu.CompilerParams(vmem_limit_bytes=VMEM_LIMIT),
        name="attention_a",
    )(qt, k, vt, z)


def _attn_d_kernel(sink_ref, qt_ref, k_ref, vt_ref, z_ref, o_ref, *, ctx_len, tq):
    tiles = qt_ref.shape[2] // tq
    t = k_ref.shape[1]
    span = tq + 2 * WINDOW
    first = lax.broadcasted_iota(jnp.int32, (1, 2 * tq), 1) < tq
    rel = (lax.broadcasted_iota(jnp.int32, (span, 2 * tq), 0)
           - lax.broadcasted_iota(jnp.int32, (span, 2 * tq), 1) % tq)

    def scores(u):
        tile = pl.program_id(1) * tiles + u
        q0 = tile * tq
        start = pl.multiple_of(jnp.clip(q0 - WINDOW, ctx_len, t - span), LANES)
        mask = (jnp.abs(rel + (start - q0)) <= WINDOW) & (tile > 0)
        sc = []
        for g in range(2):
            qt_g = _group_queries(qt_ref, g, slice(u * tq, (u + 1) * tq))
            sc.append((_dot(k_ref[0, :ctx_len, :], qt_g),
                       _dot(k_ref[0, pl.ds(start, span), :], qt_g)))
        return start, mask, sc

    def finish(u, start, mask, sc):
        for g in range(2):
            vrows = slice(g * V_ROWS, (g + 1) * V_ROWS)
            s_ctx = sc[g][0]
            s_loc = jnp.where(mask, sc[g][1], -jnp.inf)
            sink = jnp.where(first, sink_ref[2 * g], sink_ref[2 * g + 1]) * LOG2_E
            m = jnp.maximum(jnp.maximum(jnp.max(s_ctx, axis=0, keepdims=True),
                                        jnp.max(s_loc, axis=0, keepdims=True)), sink)
            acc = (_dot(vt_ref[0, vrows, :ctx_len], jnp.exp2(s_ctx - m).astype(BF16))
                   + _dot(vt_ref[0, vrows, pl.ds(start, span)], jnp.exp2(s_loc - m).astype(BF16)))
            l = acc[HEAD_DIM:HEAD_DIM + 1] + jnp.exp2(sink - m)
            _store_heads(o_ref, z_ref, g, acc[:HEAD_DIM] / l, slice(u * tq, (u + 1) * tq))

    pending = scores(0)
    for u in range(tiles):
        ready, pending = pending, (scores(u + 1) if u + 1 < tiles else None)
        finish(u, *ready)


def _attention_d(sink, qt, k, vt, z, ctx_len):
    nb, t, _ = k.shape
    tq = TOKEN_TILE
    step = ATTN_D_TILES * tq
    return pl.pallas_call(
        functools.partial(_attn_d_kernel, ctx_len=ctx_len, tq=tq),
        grid=(nb, t // step),
        in_specs=[pl.BlockSpec(memory_space=pltpu.SMEM),
                  pl.BlockSpec((1, BRANCH_W, step), lambda b, j: (b, 0, j)),
                  pl.BlockSpec((1, t, KV_W), lambda b, j: (b, 0, 0)),
                  pl.BlockSpec((1, 2 * V_ROWS, t), lambda b, j: (b, 0, 0)),
                  pl.BlockSpec((1, step, BRANCH_W), lambda b, j: (b, j, 0))],
        out_specs=pl.BlockSpec((1, step, BRANCH_W), lambda b, j: (b, j, 0)),
        out_shape=jax.ShapeDtypeStruct((nb, t, BRANCH_W), BF16),
        compiler_params=pltpu.CompilerParams(vmem_limit_bytes=VMEM_LIMIT),
        name="attention_d",
    )(sink, qt, k, vt, z)


def _dft_tables(seq_len, ctx_len):
    n2c = DFT_INNER
    n1c = seq_len // n2c
    scale = 1.0 / np.sqrt(float(seq_len) * GROUP_W)
    k1 = np.arange(n1c)[:, None]
    n1 = np.arange(n1c)[None, :]
    m1 = np.zeros((n2c, 2 * n1c, n1c), np.float64)
    for n2 in range(n2c):
        ang = -2.0 * np.pi * (k1 * n1 / n1c + n2 * k1 / seq_len)
        m1[n2, :n1c] = np.cos(ang) * scale
        m1[n2, n1c:] = np.sin(ang) * scale
    ang3 = 2.0 * np.pi * np.outer(np.arange(n2c), np.arange(n2c)) / n2c
    c3, s3 = np.cos(ang3), np.sin(ang3)
    m3 = np.block([[c3, s3], [-s3, c3]])
    cscale = 1.0 / np.sqrt(float(ctx_len) * GROUP_W)
    angc = 2.0 * np.pi * np.outer(np.arange(ctx_len), np.arange(ctx_len)) / ctx_len
    mc = np.concatenate([np.cos(angc), -np.sin(angc)], axis=0) * cscale
    angg = 2.0 * np.pi * np.outer(np.arange(GROUP_W), np.arange(GROUP_W)) / GROUP_W
    eye = np.eye(BRANCH_W // GROUP_W)
    chan = np.concatenate([np.kron(eye, np.cos(angg)), np.kron(eye, np.sin(angg))], axis=0)
    return _stacked(m1), _stacked(m3), _split(mc), _split(chan)


def _stacked(a):
    hi, lo = _split(a)
    return jnp.concatenate([hi, lo, hi], axis=-1)


def _dot3_stacked(m3, x):
    x_hi = x.astype(BF16)
    x_lo = (x - x_hi.astype(F32)).astype(BF16)
    return _dot(m3, jnp.concatenate([x_hi, x_hi, x_lo], axis=0))


def _split(a):
    a32 = jnp.asarray(a.astype(np.float32))
    hi = a32.astype(BF16)
    return hi, (a32 - hi.astype(F32)).astype(BF16)


def _dot3_left(m_hi, m_lo, x):
    x_hi = x.astype(BF16)
    x_lo = (x - x_hi.astype(F32)).astype(BF16)
    return _dot(m_hi, x_hi) + _dot(m_lo, x_hi) + _dot(m_hi, x_lo)


def _dot3_right(x, m_hi, m_lo):
    x_hi = x.astype(BF16)
    x_lo = (x - x_hi.astype(F32)).astype(BF16)
    return _dot(x_hi, m_hi) + _dot(x_hi, m_lo) + _dot(x_lo, m_hi)


def _fnet_stage1_kernel(f_ref, m_ref, br_ref, bi_ref):
    sub = DFT_BLOCK
    halves = range(f_ref.shape[1])
    n1c = br_ref.shape[3] // sub
    skip = f_ref.shape[3] - br_ref.shape[3]
    for q in range(sub):
        rows = pl.ds(q, n1c, stride=sub)
        x = jnp.concatenate([f_ref[0, h, 0, pl.ds(skip + q, n1c, stride=sub), :] for h in halves],
                            axis=1)
        res = _dot3_stacked(m_ref[q], x)
        for h in halves:
            br_ref[0, h, 0, rows, :] = res[:n1c, h * LANES:(h + 1) * LANES]
            bi_ref[0, h, 0, rows, :] = res[n1c:, h * LANES:(h + 1) * LANES]


def _fnet_stage2_kernel(br_ref, bi_ref, fc_ref, m_ref, mch_ref, mcl_ref,
                        pr_ref, pi_ref, prc_ref, pic_ref):
    n2c = DFT_INNER
    sub = DFT_BLOCK
    ctx_len = fc_ref.shape[1]

    @pl.when(pl.program_id(1) == 0)
    def _():
        res = _dot3_left(mch_ref[...], mcl_ref[...], fc_ref[0])
        prc_ref[0] = res[:ctx_len]
        pic_ref[0] = res[ctx_len:]

    halves = range(br_ref.shape[1])
    for q in range(sub):
        xin = jnp.concatenate(
            [jnp.concatenate([ref[0, h, hi, q * sub:(q + 1) * sub, :] for h in halves], axis=1)
             for ref in (br_ref, bi_ref) for hi in range(n2c // sub)], axis=0)
        res = _dot3_stacked(m_ref[...], xin)
        rows = pl.ds(q, n2c, stride=sub)
        for h in halves:
            pr_ref[0, h, 0, rows, :] = res[:n2c, h * LANES:(h + 1) * LANES]
            pi_ref[0, h, 0, rows, :] = res[n2c:, h * LANES:(h + 1) * LANES]


def _fnet_dft(fb, fb_ctx, m1, m3, mc):
    nb, nh, n_hi, rows_in, w = fb.shape
    ctx_len = fb_ctx.shape[1]
    n2c = DFT_INNER
    sub = DFT_BLOCK
    rows = rows_in - ctx_len // n_hi
    n1c = rows // sub
    blk0 = pl.BlockSpec((1, nh, 1, rows_in, w), lambda b, i: (b, 0, i, 0, 0))
    blk1 = pl.BlockSpec((1, nh, 1, rows, w), lambda b, i: (b, 0, i, 0, 0))
    tab1 = pl.BlockSpec((sub, 2 * n1c, 3 * n1c), lambda b, i: (i, 0, 0))
    b4 = jax.ShapeDtypeStruct((nb, nh, n_hi, rows, w), F32)
    br, bi = pl.pallas_call(
        _fnet_stage1_kernel,
        grid=(nb, n_hi),
        in_specs=[blk0, tab1],
        out_specs=[blk1, blk1],
        out_shape=[b4, b4],
        compiler_params=pltpu.CompilerParams(vmem_limit_bytes=VMEM_LIMIT),
        name="fnet_stage1",
    )(fb, m1)
    blk2 = pl.BlockSpec((1, nh, n_hi, sub * sub, w), lambda b, i: (b, 0, 0, i, 0))
    oblk = pl.BlockSpec((1, nh, 1, n2c * sub, w), lambda b, i: (b, 0, i, 0, 0))
    cblk = pl.BlockSpec((1, ctx_len, nh * w), lambda b, i: (b, 0, 0))
    const = lambda a: pl.BlockSpec(a.shape, lambda b, i: (0,) * a.ndim)
    p4 = jax.ShapeDtypeStruct((nb, nh, n1c // sub, n2c * sub, w), F32)
    pc = jax.ShapeDtypeStruct((nb, ctx_len, nh * w), F32)
    return pl.pallas_call(
        _fnet_stage2_kernel,
        grid=(nb, n1c // sub),
        in_specs=[blk2, blk2, cblk, const(m3), const(mc[0]), const(mc[1])],
        out_specs=[oblk, oblk, cblk, cblk],
        out_shape=[p4, p4, pc, pc],
        compiler_params=pltpu.CompilerParams(vmem_limit_bytes=VMEM_LIMIT),
        name="fnet_stage2",
    )(br, bi, fb_ctx, m3, *mc)


def _merge_kernel(xc_ref, xl_ref, mod_ref, nw_ref, ya_ref, yd_ref, pr_ref, pi_ref, prc_ref,
                  pic_ref, zb_ref, yc_ref, chh_ref, chl_ref, wf_ref, wbr_ref, wm_ref, bm_ref,
                  wo_ref, *out_refs, n_batch, tile_off):
    b, j = pl.program_id(0), pl.program_id(1)
    d = xl_ref.shape[-1]
    is_ctx = j + tile_off == 0
    x = jnp.where(is_ctx, xc_ref[0], xl_ref[0]) if tile_off == 0 else xl_ref[0]
    row = jnp.where(is_ctx, n_batch, b)
    h, gate = _modulated_norm(x, mod_ref, nw_ref, row, d)
    def latent_rows(ref):
        sub = DFT_BLOCK
        return jnp.concatenate(
            [jnp.concatenate([ref[0, h, kb, k2 * sub:(k2 + 1) * sub, :]
                              for k2 in range(ref.shape[3] // sub)
                              for kb in range(ref.shape[2])], axis=0)
             for h in range(ref.shape[1])], axis=1)

    pc = jnp.concatenate([jnp.where(is_ctx, prc_ref[0], latent_rows(pr_ref)),
                          jnp.where(is_ctx, pic_ref[0], latent_rows(pi_ref))], axis=1)
    yb = _dot(_dot3_right(pc, chh_ref[...], chl_ref[...]).astype(BF16), wf_ref[...]) * zb_ref[0]
    branches = (ya_ref[0], yd_ref[0], yb.astype(BF16), yc_ref[0])
    hb = h.astype(BF16)
    acc = jnp.zeros((x.shape[0], d), F32)
    for r in range(N_BRANCH):
        g = jax.nn.sigmoid(_dot(hb, wm_ref[:, r * d:(r + 1) * d]) + bm_ref[:, r * d:(r + 1) * d])
        acc = acc + g * _dot(branches[r], wbr_ref[r])
    out = x + gate * _dot(acc.astype(BF16), wo_ref[...])
    out_refs[-1][0] = out
    if tile_off == 0:
        @pl.when(is_ctx)
        def _():
            out_refs[0][0] = out


def _merge(layer, xc, xl, mod, nw, ya, yd, pr, pi, prc, pic, zb, yc, chan, wf_bd, w_br, w_merge,
           b_merge, w_out, skip_ctx):
    nb, seq, d = xl.shape
    tm = TOKEN_TILE
    off = 1 if skip_ctx else 0
    nt = (seq + xc.shape[1]) // tm - off
    tok_spec = lambda w: pl.BlockSpec((1, tm, w), lambda b, j: (b, j + off, 0))
    ctx_tile = pl.BlockSpec((1, tm, d), lambda b, j: (b, 0, 0))
    lat_tile = pl.BlockSpec((1, tm, d), lambda b, j: (b, jnp.maximum(j + off - 1, 0), 0))
    xc_shape = jax.ShapeDtypeStruct(xc.shape, F32)
    xl_shape = jax.ShapeDtypeStruct(xl.shape, F32)
    lat_spec = lambda a: pl.BlockSpec((1, a.shape[1], a.shape[2], tm // a.shape[2], a.shape[4]),
                                      lambda b, j: (b, 0, 0, jnp.maximum(j + off - 1, 0), 0))
    ctx_spec = lambda a: pl.BlockSpec((1,) + a.shape[1:], lambda b, j: (b, 0, 0))
    const = lambda a: pl.BlockSpec(a.shape, lambda b, j: (0,) * a.ndim)
    of_layer = lambda a: pl.BlockSpec((None,) + a.shape[1:],
                                      lambda b, j: (layer,) + (0,) * (a.ndim - 1))
    return pl.pallas_call(
        functools.partial(_merge_kernel, n_batch=nb, tile_off=off),
        grid=(nb, nt),
        in_specs=[ctx_tile, lat_tile, of_layer(mod), const(nw), tok_spec(BRANCH_W),
                  tok_spec(BRANCH_W), lat_spec(pr), lat_spec(pi), ctx_spec(prc), ctx_spec(pic),
                  tok_spec(BRANCH_W), tok_spec(BRANCH_W), const(chan[0]), const(chan[1]),
                  const(wf_bd), of_layer(w_br), of_layer(w_merge), const(b_merge),
                  of_layer(w_out)],
        out_specs=[lat_tile] if skip_ctx else [ctx_tile, lat_tile],
        out_shape=[xl_shape] if skip_ctx else [xc_shape, xl_shape],
        compiler_params=pltpu.CompilerParams(vmem_limit_bytes=VMEM_LIMIT),
        name="merge",
    )(xc, xl, mod, nw, ya, yd, pr, pi, prc, pic, zb, yc, *chan, wf_bd, w_br, w_merge, b_merge,
      w_out)


def _rope_tables(seq_len, ctx_len):
    t = jnp.arange(seq_len, dtype=jnp.int32)
    r = (t // GRID_W).astype(F32)
    col = (t % GRID_W).astype(F32)
    nf = HEAD_DIM // 4
    inv = ROPE_BASE ** (-jnp.arange(nf, dtype=F32) / nf)
    ar = r[:, None] * inv[None, :]
    ac = col[:, None] * inv[None, :]
    cr, sr, cc, sc = jnp.cos(ar), jnp.sin(ar), jnp.cos(ac), jnp.sin(ac)
    cos_h = jnp.concatenate([cr, cr, cc, cc], axis=1)
    sin_h = jnp.concatenate([-sr, sr, -sc, sc], axis=1)
    cos_t = jnp.concatenate([jnp.ones((ctx_len, HEAD_DIM), F32), cos_h], axis=0)
    sin_t = jnp.concatenate([jnp.zeros((ctx_len, HEAD_DIM), F32), sin_h], axis=0)
    return jnp.tile(cos_t, (1, 2)), jnp.tile(sin_t, (1, 2))


def kernel(x, c, ctx, c_ctx, norm_w, w_ada, b_ada, w_in, qn_a, kn_a, qn_d, kn_d, sink_d,
           w_fnet, w_sp, b_sp, w_br, w_merge, b_merge, w_out):
    nb, seq, d = x.shape
    ctx_len = ctx.shape[1]
    depth = norm_w.shape[0]
    assert ctx_len == TOKEN_TILE and seq % (2 * ATTN_A_KEYS) == 0 and nb < 8
    assert (seq + ctx_len) % (ATTN_A_TILES * TOKEN_TILE) == 0
    assert (seq + ctx_len) % (ATTN_D_TILES * TOKEN_TILE) == 0
    assert (seq + ctx_len) % (PROJ_TILES * TOKEN_TILE) == 0

    cvecs = jnp.zeros((8, d), F32).at[:nb].set(c).at[nb].set(c_ctx)
    mods = _modulation(cvecs, w_ada, b_ada)

    cos_t, sin_t = _rope_tables(seq, ctx_len)
    m1, m3, mc, chan = _dft_tables(seq, ctx_len)
    pair = lambda w: jnp.tile(w, 2).reshape(1, LANES)
    eye_g = jnp.eye(BRANCH_W // GROUP_W, dtype=F32)

    w_in, w_sp, w_br, w_merge, w_out = (w.astype(BF16) for w in (w_in, w_sp, w_br, w_merge, w_out))
    xc, xl = ctx, x
    for l in range(depth):
        nw = norm_w[l].reshape(1, d)
        wf_bd = jnp.einsum('gh,gcd->gchd', eye_g, w_fnet[l]).reshape(BRANCH_W, BRANCH_W)
        b_sp_t = jnp.repeat(b_sp[l].T, GROUP_W, axis=1)
        (qat, ka, vat, za, qdt, kd, vdt, zd, fb, fbc, zb, yc) = _projection(
            l, xc, xl, mods, nw, w_in, cos_t, sin_t,
            pair(qn_a[l]), pair(kn_a[l]), pair(qn_d[l]), pair(kn_d[l]),
            w_sp, b_sp_t)
        ya = _attention_a(qat, ka, vat, za, ctx_len)
        yd = _attention_d(sink_d[l], qdt, kd, vdt, zd, ctx_len)
        pr, pi, prc, pic = _fnet_dft(fb, fbc, m1, m3, mc)
        *xc_new, xl = _merge(l, xc, xl, mods, nw, ya, yd, pr, pi, prc, pic, zb, yc, chan,
                             wf_bd.astype(BF16), w_br, w_merge, b_merge[l].reshape(1, -1), w_out,
                             skip_ctx=(l == depth - 1))
        xc = xc_new[0] if xc_new else None
    return xl
```

```python
import functools

import numpy as np
import jax
import jax.numpy as jnp
from jax import lax
from jax.experimental import pallas as pl
from jax.experimental.pallas import tpu as pltpu

F32 = jnp.float32
BF16 = jnp.bfloat16

GRID_W = 64
HEAD_DIM = 64
BRANCH_W = 256
KV_W = 128
N_BRANCH = 4
GROUP_W = 64
CHUNK = 128
Q_BLOCK = 128
WINDOW = 128
ROPE_BASE = 10000.0
EPS = 1e-6
LOG2_E = float(np.log2(np.e))
LANES = 128
TOKEN_TILE = 256
V_ROWS = 80
ATTN_A_KEYS = 512
ATTN_A_TILES = 3
ATTN_D_TILES = 3
PROJ_TILES = 3
DFT_BLOCK = 8
DFT_INNER = 64
VMEM_LIMIT = 56 * 1024 * 1024

_IN_SIZES = (BRANCH_W, KV_W, KV_W, BRANCH_W, BRANCH_W, KV_W, KV_W, BRANCH_W,
             BRANCH_W, BRANCH_W, BRANCH_W, BRANCH_W, BRANCH_W)
_IN_OFF = tuple(int(v) for v in np.cumsum((0,) + _IN_SIZES))
IN_W = _IN_OFF[-1]
(_AQ, _AK, _AV, _AZ, _DQ, _DK, _DV, _DZ, _BF, _BZ, _CU, _CV, _CZ) = _IN_OFF[:-1]


def _silu(z):
    return z * jax.nn.sigmoid(z)


def _gelu(x):
    return 0.5 * x * (1.0 + lax.erf(x * np.float32(np.sqrt(0.5))))


def _dot(a, b):
    return jnp.dot(a, b, preferred_element_type=F32)


def _lane_lo(width=LANES):
    lane = lax.broadcasted_iota(jnp.int32, (1, width), 1)
    return (lane % LANES) < HEAD_DIM


def _mod_kernel(cv_ref, w_ref, b_ref, o_ref):
    s = _silu(cv_ref[...])
    o_ref[0] = _dot(s.astype(BF16), w_ref[0].astype(BF16)) + b_ref[0]


def _modulation(cvecs, w_ada, b_ada):
    depth, d, d3 = w_ada.shape
    nblk = d3 // d
    return pl.pallas_call(
        _mod_kernel,
        grid=(depth, nblk),
        in_specs=[pl.BlockSpec((8, d), lambda l, n: (0, 0)),
                  pl.BlockSpec((1, d, d), lambda l, n: (l, 0, n)),
                  pl.BlockSpec((1, 1, d), lambda l, n: (l, 0, n))],
        out_specs=pl.BlockSpec((1, 8, d), lambda l, n: (l, 0, n)),
        out_shape=jax.ShapeDtypeStruct((depth, 8, d3), F32),
        name="modulation",
    )(cvecs, w_ada, b_ada.reshape(depth, 1, d3))


def _modulated_norm(x, mod_ref, nw_ref, row, d):
    m = mod_ref[pl.ds(row, 1), :]
    sh, sc = m[:, :d], m[:, d:2 * d]
    ms = jnp.mean(x * x, axis=-1, keepdims=True)
    xn = x * lax.rsqrt(ms + EPS) * nw_ref[...]
    return xn * (1.0 + sc) + sh, m[:, 2 * d:]


def _head_norm_rope(xs, wn, cos, sin, scale):
    sq = xs * xs
    lo = _lane_lo()
    ms = jnp.where(lo, jnp.sum(jnp.where(lo, sq, 0.0), axis=1, keepdims=True),
                   jnp.sum(jnp.where(lo, 0.0, sq), axis=1, keepdims=True)) * (1.0 / HEAD_DIM)
    y = xs * lax.rsqrt(ms + EPS) * wn
    lane = lax.broadcasted_iota(jnp.int32, (1, LANES), 1)
    first = (lane % 32) < 16
    sw = jnp.where(first, pltpu.roll(y, LANES - 16, axis=1), pltpu.roll(y, 16, axis=1))
    y = y * cos + sw * sin
    return y * scale if scale != 1.0 else y


def _proj_kernel(xc_ref, *refs, n_batch, tiles):
    xl_refs, refs = refs[:tiles], refs[tiles:]
    (mod_ref, nw_ref, win_ref, cos_ref, sin_ref, qna_ref, kna_ref, qnd_ref, knd_ref, wsp_ref,
     bsp_ref, qat_ref, ka_ref, vat_ref, za_ref, qdt_ref, kd_ref, vdt_ref, zd_ref,
     fb_ref, fbc_ref, zb_ref, yc_ref) = refs
    b, j = pl.program_id(0), pl.program_id(1)
    d = xc_ref.shape[-1]
    tm = xc_ref.shape[1]
    q_scale = HEAD_DIM ** -0.5 * LOG2_E
    group = lax.broadcasted_iota(jnp.int32, (1, BRANCH_W), 1) // GROUP_W
    sub = DFT_BLOCK

    for u in range(tiles):
        tok = slice(u * tm, (u + 1) * tm)
        if u == 0:
            x = jnp.where(j == 0, xc_ref[0], xl_refs[0][0])
            row = jnp.where(j == 0, n_batch, b)
        else:
            x, row = xl_refs[u][0], b
        h, _ = _modulated_norm(x, mod_ref, nw_ref, row, d)
        hb = h.astype(BF16)
        pc = _dot(hb, win_ref[:, _CU:])
        p = _dot(hb, win_ref[:, :_CU])

        gu = _gelu(pc[:, :BRANCH_W])
        gv = _gelu(pc[:, BRANCH_W:2 * BRANCH_W]).astype(BF16)
        zc = _silu(pc[:, 2 * BRANCH_W:])
        for c in range(tm // CHUNK):
            rows = slice(c * CHUNK, (c + 1) * CHUNK)
            sp = bsp_ref[...]
            for g in range(BRANCH_W // GROUP_W):
                sp = sp + jnp.where(group == g, _dot(wsp_ref[g], gv[rows]), 0.0)
            yc_ref[0, u * tm + c * CHUNK:u * tm + (c + 1) * CHUNK, :] = (
                gu[rows] * sp * zc[rows]).astype(BF16)

        cos, sin = cos_ref[tok, :], sin_ref[tok, :]

        def attn_branch(q0, k0, v0, z0, qn_ref, kn_ref, qt_ref, k_ref, vt_ref, z_ref):
            q = [_head_norm_rope(p[:, q0 + s * LANES:q0 + (s + 1) * LANES], qn_ref[...],
                                 cos, sin, q_scale) for s in range(BRANCH_W // LANES)]
            qt_ref[0, :, tok] = jnp.concatenate(q, axis=1).T.astype(BF16)
            k_ref[0, tok, :] = _head_norm_rope(p[:, k0:k0 + KV_W], kn_ref[...], cos, sin,
                                               1.0).astype(BF16)
            vt = p[:, v0:v0 + KV_W].T
            ones = jnp.ones((V_ROWS - HEAD_DIM, tm), F32)
            vt_ref[0, :, tok] = jnp.concatenate([vt[:HEAD_DIM], ones, vt[HEAD_DIM:], ones],
                                                axis=0).astype(BF16)
            z_ref[0, tok, :] = _silu(p[:, z0:z0 + BRANCH_W])

        attn_branch(_AQ, _AK, _AV, _AZ, qna_ref, kna_ref, qat_ref, ka_ref, vat_ref, za_ref)
        attn_branch(_DQ, _DK, _DV, _DZ, qnd_ref, knd_ref, qdt_ref, kd_ref, vdt_ref, zd_ref)

        f = p[:, _BF:_BF + BRANCH_W]
        per_hi = tm // (DFT_INNER // sub)
        for hi in range(DFT_INNER // sub):
            rows = jnp.concatenate([f[n1 * DFT_INNER + hi * sub:n1 * DFT_INNER + (hi + 1) * sub]
                                    for n1 in range(tm // DFT_INNER)], axis=0)
            for half in range(BRANCH_W // LANES):
                fb_ref[0, half, hi, u * per_hi:(u + 1) * per_hi, :] = (
                    rows[:, half * LANES:(half + 1) * LANES])
        if u == 0:
            @pl.when(j == 0)
            def _():
                fbc_ref[0] = f

        zb_ref[0, tok, :] = _silu(p[:, _BZ:_BZ + BRANCH_W])


def _projection(layer, xc, xl, mod, nw, w_in, cos_t, sin_t, qna, kna, qnd, knd, w_sp, b_sp_t):
    nb, seq, d = xl.shape
    tm = TOKEN_TILE
    tiles = PROJ_TILES
    step = tiles * tm
    t = seq + xc.shape[1]
    tok = lambda w, dt: jax.ShapeDtypeStruct((nb, t, w), dt)
    tok_spec = lambda w: pl.BlockSpec((1, step, w), lambda b, j: (b, j, 0))
    tr = lambda w: jax.ShapeDtypeStruct((nb, w, t), BF16)
    tr_spec = lambda w: pl.BlockSpec((1, w, step), lambda b, j: (b, 0, j))
    const = lambda a: pl.BlockSpec(a.shape, lambda b, j: (0,) * a.ndim)
    of_layer = lambda a: pl.BlockSpec((None,) + a.shape[1:],
                                      lambda b, j: (layer,) + (0,) * (a.ndim - 1))
    lat_tile = lambda u: pl.BlockSpec(
        (1, tm, d), lambda b, j: (b, jnp.clip(tiles * j + u - 1, 0, seq // tm - 1), 0))
    attn_specs = [tr_spec(BRANCH_W), tok_spec(KV_W), tr_spec(2 * V_ROWS), tok_spec(BRANCH_W)]
    attn_shapes = [tr(BRANCH_W), tok(KV_W, BF16), tr(2 * V_ROWS), tok(BRANCH_W, F32)]
    n_hi = DFT_INNER // DFT_BLOCK
    return pl.pallas_call(
        functools.partial(_proj_kernel, n_batch=nb, tiles=tiles),
        grid=(nb, t // step),
        in_specs=[pl.BlockSpec((1, tm, d), lambda b, j: (b, 0, 0))]
                 + [lat_tile(u) for u in range(tiles)]
                 + [of_layer(mod), const(nw), of_layer(w_in),
                    pl.BlockSpec((step, LANES), lambda b, j: (j, 0)),
                    pl.BlockSpec((step, LANES), lambda b, j: (j, 0)),
                    const(qna), const(kna), const(qnd), const(knd), of_layer(w_sp),
                    const(b_sp_t)],
        out_specs=attn_specs * 2 + [
            pl.BlockSpec((1, BRANCH_W // LANES, n_hi, step // n_hi, LANES),
                         lambda b, j: (b, 0, 0, j, 0)),
            pl.BlockSpec((1, tm, BRANCH_W), lambda b, j: (b, 0, 0)),
            tok_spec(BRANCH_W), tok_spec(BRANCH_W)],
        out_shape=attn_shapes * 2 + [
            jax.ShapeDtypeStruct((nb, BRANCH_W // LANES, n_hi, t // n_hi, LANES), F32),
            jax.ShapeDtypeStruct((nb, tm, BRANCH_W), F32),
            tok(BRANCH_W, F32), tok(BRANCH_W, BF16)],
        compiler_params=pltpu.CompilerParams(vmem_limit_bytes=VMEM_LIMIT),
        name="projection",
    )(xc, *([xl] * tiles), mod, nw, w_in, cos_t, sin_t, qna, kna, qnd, knd, w_sp, b_sp_t)


def _group_queries(qt_ref, g, tokens=slice(None)):
    r0 = g * LANES
    qrow = jnp.concatenate([qt_ref[0, r0:r0 + HEAD_DIM, tokens],
                            qt_ref[0, r0 + HEAD_DIM:r0 + LANES, tokens]], axis=1)
    zeros = jnp.zeros_like(qrow)
    return jnp.concatenate([qrow, zeros] if g == 0 else [zeros, qrow], axis=0)


def _store_heads(o_ref, z_ref, g, o_t, tokens=slice(None)):
    tq = o_t.shape[1] // 2
    o = jnp.concatenate([o_t[:, :tq], o_t[:, tq:]], axis=0).T
    cols = slice(g * LANES, (g + 1) * LANES)
    o_ref[0, tokens, cols] = (o * z_ref[0, tokens, cols]).astype(BF16)


def _attn_a_kernel(qt_ref, k_ref, vt_ref, z_ref, o_ref, s_ref, smax_ref, m_ref, acc_ref,
                   *, tk, ctx_len, tq):
    tiles = qt_ref.shape[2] // tq
    t = k_ref.shape[1]
    n_lat = (t - ctx_len) // tk
    groups = range(2)

    def values(g, start, size):
        return vt_ref[0, g * V_ROWS:(g + 1) * V_ROWS, start:start + size]

    class Tile:
        def __init__(self, u):
            self.tok = slice(u * tq, (u + 1) * tq)
            self.par = u % 2

        def scores(self, g, start, size):
            return _dot(k_ref[0, start:start + size, :], self.qt_g[g])

        def produce(self, g, c):
            s = self.scores(g, ctx_len + c * tk, tk)
            smax_ref[c % 2, g] = jnp.max(s, axis=0, keepdims=True)
            s_ref[c % 2, g] = s

        def consume(self, g, c):
            m = m_ref[self.par, g]
            m_new = jnp.maximum(m, smax_ref[c % 2, g])
            pr = jnp.exp2(s_ref[c % 2, g] - m_new).astype(BF16)
            acc_ref[self.par, g] = (jnp.exp2(m - m_new) * acc_ref[self.par, g]
                                    + _dot(values(g, ctx_len + c * tk, tk), pr))
            m_ref[self.par, g] = m_new

        def head_scores(self):
            self.qt_g = [_group_queries(qt_ref, g, self.tok) for g in groups]
            self.s_ctx = [self.scores(g, 0, ctx_len) for g in groups]
            for g in groups:
                self.produce(g, 0)

        def head_finish(self):
            for g in groups:
                m = jnp.max(self.s_ctx[g], axis=0, keepdims=True)
                m_ref[self.par, g] = m
                acc_ref[self.par, g] = _dot(values(g, 0, ctx_len),
                                            jnp.exp2(self.s_ctx[g] - m).astype(BF16))

        def body(self):
            for c in range(n_lat - 1):
                for g in groups:
                    self.produce(g, c + 1)
                    self.consume(g, c)

        def last(self):
            for g in groups:
                self.consume(g, n_lat - 1)

        def store(self):
            for g in groups:
                acc = acc_ref[self.par, g]
                _store_heads(o_ref, z_ref, g, acc[:HEAD_DIM] / acc[HEAD_DIM:HEAD_DIM + 1],
                             self.tok)

    tile = Tile(0)
    tile.head_scores()
    tile.head_finish()

    @pl.when(pl.program_id(1) > 0)
    def _():
        tile.body()
        tile.last()

    for u in range(1, tiles):
        nxt = Tile(u)
        nxt.head_scores()
        if u > 1:
            tile.last()
        tile.store()
        nxt.head_finish()
        nxt.body()
        tile = nxt
    if tiles > 1:
        tile.last()
    tile.store()


def _attention_a(qt, k, vt, z, ctx_len):
    nb, t, _ = k.shape
    tq = TOKEN_TILE
    tk = ATTN_A_KEYS
    step = ATTN_A_TILES * tq
    return pl.pallas_call(
        functools.partial(_attn_a_kernel, tk=tk, ctx_len=ctx_len, tq=tq),
        grid=(nb, t // step),
        in_specs=[pl.BlockSpec((1, BRANCH_W, step), lambda b, j: (b, 0, j)),
                  pl.BlockSpec((1, t, KV_W), lambda b, j: (b, 0, 0)),
                  pl.BlockSpec((1, 2 * V_ROWS, t), lambda b, j: (b, 0, 0)),
                  pl.BlockSpec((1, step, BRANCH_W), lambda b, j: (b, j, 0))],
        out_specs=pl.BlockSpec((1, step, BRANCH_W), lambda b, j: (b, j, 0)),
        out_shape=jax.ShapeDtypeStruct((nb, t, BRANCH_W), BF16),
        scratch_shapes=[pltpu.VMEM((2, 2, tk, 2 * tq), F32),
                        pltpu.VMEM((2, 2, 1, 2 * tq), F32),
                        pltpu.VMEM((2, 2, 1, 2 * tq), F32),
                        pltpu.VMEM((2, 2, V_ROWS, 2 * tq), F32)],
        compiler_params=pltpu.CompilerParams(vmem_limit_bytes=VMEM_LIMIT),
        name="attention_a",
    )(qt, k, vt, z)


def _attn_d_kernel(sink_ref, qt_ref, k_ref, vt_ref, z_ref, o_ref, *, ctx_len, tq):
    tiles = qt_ref.shape[2] // tq
    t = k_ref.shape[1]
    span = tq + 2 * WINDOW
    first = lax.broadcasted_iota(jnp.int32, (1, 2 * tq), 1) < tq
    rel = (lax.broadcasted_iota(jnp.int32, (span, 2 * tq), 0)
           - lax.broadcasted_iota(jnp.int32, (span, 2 * tq), 1) % tq)

    def scores(u):
        tile = pl.program_id(1) * tiles + u
        q0 = tile * tq
        start = pl.multiple_of(jnp.clip(q0 - WINDOW, ctx_len, t - span), LANES)
        mask = (jnp.abs(rel + (start - q0)) <= WINDOW) & (tile > 0)
        sc = []
        for g in range(2):
            qt_g = _group_queries(qt_ref, g, slice(u * tq, (u + 1) * tq))
            sc.append((_dot(k_ref[0, :ctx_len, :], qt_g),
                       _dot(k_ref[0, pl.ds(start, span), :], qt_g)))
        return start, mask, sc

    def finish(u, start, mask, sc):
        for g in range(2):
            vrows = slice(g * V_ROWS, (g + 1) * V_ROWS)
            s_ctx = sc[g][0]
            s_loc = jnp.where(mask, sc[g][1], -jnp.inf)
            sink = jnp.where(first, sink_ref[2 * g], sink_ref[2 * g + 1]) * LOG2_E
            m = jnp.maximum(jnp.maximum(jnp.max(s_ctx, axis=0, keepdims=True),
                                        jnp.max(s_loc, axis=0, keepdims=True)), sink)
            acc = (_dot(vt_ref[0, vrows, :ctx_len], jnp.exp2(s_ctx - m).astype(BF16))
                   + _dot(vt_ref[0, vrows, pl.ds(start, span)], jnp.exp2(s_loc - m).astype(BF16)))
            l = acc[HEAD_DIM:HEAD_DIM + 1] + jnp.exp2(sink - m)
            _store_heads(o_ref, z_ref, g, acc[:HEAD_DIM] / l, slice(u * tq, (u + 1) * tq))

    pending = scores(0)
    for u in range(tiles):
        ready, pending = pending, (scores(u + 1) if u + 1 < tiles else None)
        finish(u, *ready)


def _attention_d(sink, qt, k, vt, z, ctx_len):
    nb, t, _ = k.shape
    tq = TOKEN_TILE
    step = ATTN_D_TILES * tq
    return pl.pallas_call(
        functools.partial(_attn_d_kernel, ctx_len=ctx_len, tq=tq),
        grid=(nb, t // step),
        in_specs=[pl.BlockSpec(memory_space=pltpu.SMEM),
                  pl.BlockSpec((1, BRANCH_W, step), lambda b, j: (b, 0, j)),
                  pl.BlockSpec((1, t, KV_W), lambda b, j: (b, 0, 0)),
                  pl.BlockSpec((1, 2 * V_ROWS, t), lambda b, j: (b, 0, 0)),
                  pl.BlockSpec((1, step, BRANCH_W), lambda b, j: (b, j, 0))],
        out_specs=pl.BlockSpec((1, step, BRANCH_W), lambda b, j: (b, j, 0)),
        out_shape=jax.ShapeDtypeStruct((nb, t, BRANCH_W), BF16),
        compiler_params=pltpu.CompilerParams(vmem_limit_bytes=VMEM_LIMIT),
        name="attention_d",
    )(sink, qt, k, vt, z)


def _dft_tables(seq_len, ctx_len):
    n2c = DFT_INNER
    n1c = seq_len // n2c
    scale = 1.0 / np.sqrt(float(seq_len) * GROUP_W)
    k1 = np.arange(n1c)[:, None]
    n1 = np.arange(n1c)[None, :]
    m1 = np.zeros((n2c, 2 * n1c, n1c), np.float64)
    for n2 in range(n2c):
        ang = -2.0 * np.pi * (k1 * n1 / n1c + n2 * k1 / seq_len)
        m1[n2, :n1c] = np.cos(ang) * scale
        m1[n2, n1c:] = np.sin(ang) * scale
    ang3 = 2.0 * np.pi * np.outer(np.arange(n2c), np.arange(n2c)) / n2c
    c3, s3 = np.cos(ang3), np.sin(ang3)
    m3 = np.block([[c3, s3], [-s3, c3]])
    cscale = 1.0 / np.sqrt(float(ctx_len) * GROUP_W)
    angc = 2.0 * np.pi * np.outer(np.arange(ctx_len), np.arange(ctx_len)) / ctx_len
    mc = np.concatenate([np.cos(angc), -np.sin(angc)], axis=0) * cscale
    angg = 2.0 * np.pi * np.outer(np.arange(GROUP_W), np.arange(GROUP_W)) / GROUP_W
    eye = np.eye(BRANCH_W // GROUP_W)
    chan = np.concatenate([np.kron(eye, np.cos(angg)), np.kron(eye, np.sin(angg))], axis=0)
    return _stacked(m1), _stacked(m3), _split(mc), _split(chan)


def _stacked(a):
    hi, lo = _split(a)
    return jnp.concatenate([hi, lo, hi], axis=-1)


def _dot3_stacked(m3, x):
    x_hi = x.astype(BF16)
    x_lo = (x - x_hi.astype(F32)).astype(BF16)
    return _dot(m3, jnp.concatenate([x_hi, x_hi, x_lo], axis=0))


def _split(a):
    a32 = jnp.asarray(a.astype(np.float32))
    hi = a32.astype(BF16)
    return hi, (a32 - hi.astype(F32)).astype(BF16)


def _dot3_left(m_hi, m_lo, x):
    x_hi = x.astype(BF16)
    x_lo = (x - x_hi.astype(F32)).astype(BF16)
    return _dot(m_hi, x_hi) + _dot(m_lo, x_hi) + _dot(m_hi, x_lo)


def _dot3_right(x, m_hi, m_lo):
    x_hi = x.astype(BF16)
    x_lo = (x - x_hi.astype(F32)).astype(BF16)
    return _dot(x_hi, m_hi) + _dot(x_hi, m_lo) + _dot(x_lo, m_hi)


def _fnet_dft_kernel(f_ref, m1_ref, fc_ref, m3_ref, mch_ref, mcl_ref,
                     pr_ref, pi_ref, prc_ref, pic_ref, br_ref, bi_ref, *, n_hi):
    i = pl.program_id(1)
    n2c = DFT_INNER
    sub = DFT_BLOCK
    halves = range(f_ref.shape[1])
    rows_b = br_ref.shape[2]
    n1c = rows_b // sub
    skip = f_ref.shape[3] - rows_b

    @pl.when(i < n_hi)
    def _():
        for q in range(sub):
            rows = pl.ds(q, n1c, stride=sub)
            x = jnp.concatenate([f_ref[0, h, 0, pl.ds(skip + q, n1c, stride=sub), :]
                                 for h in halves], axis=1)
            res = _dot3_stacked(m1_ref[q], x)
            for h in halves:
                br_ref[h, i, rows, :] = res[:n1c, h * LANES:(h + 1) * LANES]
                bi_ref[h, i, rows, :] = res[n1c:, h * LANES:(h + 1) * LANES]

    @pl.when(i == n_hi)
    def _():
        ctx_len = fc_ref.shape[1]
        res = _dot3_left(mch_ref[...], mcl_ref[...], fc_ref[0])
        prc_ref[0] = res[:ctx_len]
        pic_ref[0] = res[ctx_len:]

    @pl.when(i >= n_hi)
    def _():
        base = pl.multiple_of((i - n_hi) * (sub * sub), sub * sub)
        for q in range(sub):
            xin = jnp.concatenate(
                [jnp.concatenate([ref[h, hi, pl.ds(base + q * sub, sub), :] for h in halves], axis=1)
                 for ref in (br_ref, bi_ref) for hi in range(n_hi)], axis=0)
            res = _dot3_stacked(m3_ref[...], xin)
            rows = pl.ds(q, n2c, stride=sub)
            for h in halves:
                pr_ref[0, h, 0, rows, :] = res[:n2c, h * LANES:(h + 1) * LANES]
                pi_ref[0, h, 0, rows, :] = res[n2c:, h * LANES:(h + 1) * LANES]


def _fnet_dft(fb, fb_ctx, m1, m3, mc):
    nb, nh, n_hi, rows_in, w = fb.shape
    ctx_len = fb_ctx.shape[1]
    n2c = DFT_INNER
    sub = DFT_BLOCK
    rows = rows_in - ctx_len // n_hi
    n1c = rows // sub
    n_kb = n1c // sub
    first = lambda i: jnp.minimum(i, n_hi - 1)
    second = lambda i: jnp.maximum(i - n_hi, 0)
    const = lambda a: pl.BlockSpec(a.shape, lambda b, i: (0,) * a.ndim)
    cblk = pl.BlockSpec((1, ctx_len, nh * w), lambda b, i: (b, 0, 0))
    oblk = pl.BlockSpec((1, nh, 1, n2c * sub, w), lambda b, i: (b, 0, second(i), 0, 0))
    p4 = jax.ShapeDtypeStruct((nb, nh, n_kb, n2c * sub, w), F32)
    pc = jax.ShapeDtypeStruct((nb, ctx_len, nh * w), F32)
    return pl.pallas_call(
        functools.partial(_fnet_dft_kernel, n_hi=n_hi),
        grid=(nb, n_hi + n_kb),
        in_specs=[pl.BlockSpec((1, nh, 1, rows_in, w), lambda b, i: (b, 0, first(i), 0, 0)),
                  pl.BlockSpec((sub, 2 * n1c, 3 * n1c), lambda b, i: (first(i), 0, 0)),
                  cblk, const(m3), const(mc[0]), const(mc[1])],
        out_specs=[oblk, oblk, cblk, cblk],
        out_shape=[p4, p4, pc, pc],
        scratch_shapes=[pltpu.VMEM((nh, n_hi, rows, w), F32),
                        pltpu.VMEM((nh, n_hi, rows, w), F32)],
        compiler_params=pltpu.CompilerParams(vmem_limit_bytes=VMEM_LIMIT,
                                             dimension_semantics=("arbitrary", "arbitrary")),
        name="fnet_dft",
    )(fb, m1, fb_ctx, m3, *mc)


def _merge_kernel(xc_ref, xl_ref, mod_ref, nw_ref, ya_ref, yd_ref, pr_ref, pi_ref, prc_ref,
                  pic_ref, zb_ref, yc_ref, chh_ref, chl_ref, wf_ref, wbr_ref, wm_ref, bm_ref,
                  wo_ref, *out_refs, n_batch, tile_off):
    b, j = pl.program_id(0), pl.program_id(1)
    d = xl_ref.shape[-1]
    is_ctx = j + tile_off == 0
    x = jnp.where(is_ctx, xc_ref[0], xl_ref[0]) if tile_off == 0 else xl_ref[0]
    row = jnp.where(is_ctx, n_batch, b)
    h, gate = _modulated_norm(x, mod_ref, nw_ref, row, d)
    def latent_rows(ref):
        sub = DFT_BLOCK
        return jnp.concatenate(
            [jnp.concatenate([ref[0, h, kb, k2 * sub:(k2 + 1) * sub, :]
                              for k2 in range(ref.shape[3] // sub)
                              for kb in range(ref.shape[2])], axis=0)
             for h in range(ref.shape[1])], axis=1)

    pc = jnp.concatenate([jnp.where(is_ctx, prc_ref[0], latent_rows(pr_ref)),
                          jnp.where(is_ctx, pic_ref[0], latent_rows(pi_ref))], axis=1)
    yb = _dot(_dot3_right(pc, chh_ref[...], chl_ref[...]).astype(BF16), wf_ref[...]) * zb_ref[0]
    branches = (ya_ref[0], yd_ref[0], yb.astype(BF16), yc_ref[0])
    hb = h.astype(BF16)
    acc = jnp.zeros((x.shape[0], d), F32)
    for r in range(N_BRANCH):
        g = jax.nn.sigmoid(_dot(hb, wm_ref[:, r * d:(r + 1) * d]) + bm_ref[:, r * d:(r + 1) * d])
        acc = acc + g * _dot(branches[r], wbr_ref[r])
    out = x + gate * _dot(acc.astype(BF16), wo_ref[...])
    out_refs[-1][0] = out
    if tile_off == 0:
        @pl.when(is_ctx)
        def _():
            out_refs[0][0] = out


def _merge(layer, xc, xl, mod, nw, ya, yd, pr, pi, prc, pic, zb, yc, chan, wf_bd, w_br, w_merge,
           b_merge, w_out, skip_ctx):
    nb, seq, d = xl.shape
    tm = TOKEN_TILE
    off = 1 if skip_ctx else 0
    nt = (seq + xc.shape[1]) // tm - off
    tok_spec = lambda w: pl.BlockSpec((1, tm, w), lambda b, j: (b, j + off, 0))
    ctx_tile = pl.BlockSpec((1, tm, d), lambda b, j: (b, 0, 0))
    lat_tile = pl.BlockSpec((1, tm, d), lambda b, j: (b, jnp.maximum(j + off - 1, 0), 0))
    xc_shape = jax.ShapeDtypeStruct(xc.shape, F32)
    xl_shape = jax.ShapeDtypeStruct(xl.shape, F32)
    lat_spec = lambda a: pl.BlockSpec((1, a.shape[1], a.shape[2], tm // a.shape[2], a.shape[4]),
                                      lambda b, j: (b, 0, 0, jnp.maximum(j + off - 1, 0), 0))
    ctx_spec = lambda a: pl.BlockSpec((1,) + a.shape[1:], lambda b, j: (b, 0, 0))
    const = lambda a: pl.BlockSpec(a.shape, lambda b, j: (0,) * a.ndim)
    of_layer = lambda a: pl.BlockSpec((None,) + a.shape[1:],
                                      lambda b, j: (layer,) + (0,) * (a.ndim - 1))
    return pl.pallas_call(
        functools.partial(_merge_kernel, n_batch=nb, tile_off=off),
        grid=(nb, nt),
        in_specs=[ctx_tile, lat_tile, of_layer(mod), const(nw), tok_spec(BRANCH_W),
                  tok_spec(BRANCH_W), lat_spec(pr), lat_spec(pi), ctx_spec(prc), ctx_spec(pic),
                  tok_spec(BRANCH_W), tok_spec(BRANCH_W), const(chan[0]), const(chan[1]),
                  const(wf_bd), of_layer(w_br), of_layer(w_merge), const(b_merge),
                  of_layer(w_out)],
        out_specs=[lat_tile] if skip_ctx else [ctx_tile, lat_tile],
        out_shape=[xl_shape] if skip_ctx else [xc_shape, xl_shape],
        compiler_params=pltpu.CompilerParams(vmem_limit_bytes=VMEM_LIMIT),
        name="merge",
    )(xc, xl, mod, nw, ya, yd, pr, pi, prc, pic, zb, yc, *chan, wf_bd, w_br, w_merge, b_merge,
      w_out)


def _rope_tables(seq_len, ctx_len):
    t = jnp.arange(seq_len, dtype=jnp.int32)
    r = (t // GRID_W).astype(F32)
    col = (t % GRID_W).astype(F32)
    nf = HEAD_DIM // 4
    inv = ROPE_BASE ** (-jnp.arange(nf, dtype=F32) / nf)
    ar = r[:, None] * inv[None, :]
    ac = col[:, None] * inv[None, :]
    cr, sr, cc, sc = jnp.cos(ar), jnp.sin(ar), jnp.cos(ac), jnp.sin(ac)
    cos_h = jnp.concatenate([cr, cr, cc, cc], axis=1)
    sin_h = jnp.concatenate([-sr, sr, -sc, sc], axis=1)
    cos_t = jnp.concatenate([jnp.ones((ctx_len, HEAD_DIM), F32), cos_h], axis=0)
    sin_t = jnp.concatenate([jnp.zeros((ctx_len, HEAD_DIM), F32), sin_h], axis=0)
    return jnp.tile(cos_t, (1, 2)), jnp.tile(sin_t, (1, 2))


def kernel(x, c, ctx, c_ctx, norm_w, w_ada, b_ada, w_in, qn_a, kn_a, qn_d, kn_d, sink_d,
           w_fnet, w_sp, b_sp, w_br, w_merge, b_merge, w_out):
    nb, seq, d = x.shape
    ctx_len = ctx.shape[1]
    depth = norm_w.shape[0]
    assert ctx_len == TOKEN_TILE and seq % (2 * ATTN_A_KEYS) == 0 and nb < 8
    assert (seq + ctx_len) % (ATTN_A_TILES * TOKEN_TILE) == 0
    assert (seq + ctx_len) % (ATTN_D_TILES * TOKEN_TILE) == 0
    assert (seq + ctx_len) % (PROJ_TILES * TOKEN_TILE) == 0

    cvecs = jnp.zeros((8, d), F32).at[:nb].set(c).at[nb].set(c_ctx)
    mods = _modulation(cvecs, w_ada, b_ada)

    cos_t, sin_t = _rope_tables(seq, ctx_len)
    m1, m3, mc, chan = _dft_tables(seq, ctx_len)
    pair = lambda w: jnp.tile(w, 2).reshape(1, LANES)
    eye_g = jnp.eye(BRANCH_W // GROUP_W, dtype=F32)

    w_in, w_sp, w_br, w_merge, w_out = (w.astype(BF16) for w in (w_in, w_sp, w_br, w_merge, w_out))
    xc, xl = ctx, x
    for l in range(depth):
        nw = norm_w[l].reshape(1, d)
        wf_bd = jnp.einsum('gh,gcd->gchd', eye_g, w_fnet[l]).reshape(BRANCH_W, BRANCH_W)
        b_sp_t = jnp.repeat(b_sp[l].T, GROUP_W, axis=1)
        (qat, ka, vat, za, qdt, kd, vdt, zd, fb, fbc, zb, yc) = _projection(
            l, xc, xl, mods, nw, w_in, cos_t, sin_t,
            pair(qn_a[l]), pair(kn_a[l]), pair(qn_d[l]), pair(kn_d[l]),
            w_sp, b_sp_t)
        ya = _attention_a(qat, ka, vat, za, ctx_len)
        yd = _attention_d(sink_d[l], qdt, kd, vdt, zd, ctx_len)
        pr, pi, prc, pic = _fnet_dft(fb, fbc, m1, m3, mc)
        *xc_new, xl = _merge(l, xc, xl, mods, nw, ya, yd, pr, pi, prc, pic, zb, yc, chan,
                             wf_bd.astype(BF16), w_br, w_merge, b_merge[l].reshape(1, -1), w_out,
                             skip_ctx=(l == depth - 1))
        xc = xc_new[0] if xc_new else None
    return xl
```

```python
import functools

import numpy as np
import jax
import jax.numpy as jnp
from jax import lax
from jax.experimental import pallas as pl
from jax.experimental.pallas import tpu as pltpu

F32 = jnp.float32
BF16 = jnp.bfloat16

GRID_W = 64
HEAD_DIM = 64
BRANCH_W = 256
KV_W = 128
N_BRANCH = 4
GROUP_W = 64
CHUNK = 128
Q_BLOCK = 128
WINDOW = 128
ROPE_BASE = 10000.0
EPS = 1e-6
LOG2_E = float(np.log2(np.e))
LANES = 128
TOKEN_TILE = 256
V_ROWS = 80
ATTN_A_KEYS = 512
ATTN_A_TILES = 3
ATTN_D_TILES = 11
PROJ_TILES = 3
DFT_BLOCK = 8
DFT_OUT_BLOCKS = 4
DFT_INNER = 64
VMEM_LIMIT = 56 * 1024 * 1024

_IN_SIZES = (BRANCH_W, KV_W, KV_W, BRANCH_W, BRANCH_W, KV_W, KV_W, BRANCH_W,
             BRANCH_W, BRANCH_W, BRANCH_W, BRANCH_W, BRANCH_W)
_IN_OFF = tuple(int(v) for v in np.cumsum((0,) + _IN_SIZES))
IN_W = _IN_OFF[-1]
(_AQ, _AK, _AV, _AZ, _DQ, _DK, _DV, _DZ, _BF, _BZ, _CU, _CV, _CZ) = _IN_OFF[:-1]


def _silu(z):
    return z * jax.nn.sigmoid(z)


def _gelu(x):
    return 0.5 * x * (1.0 + lax.erf(x * np.float32(np.sqrt(0.5))))


def _dot(a, b):
    return jnp.dot(a, b, preferred_element_type=F32)


def _lane_lo(width=LANES):
    lane = lax.broadcasted_iota(jnp.int32, (1, width), 1)
    return (lane % LANES) < HEAD_DIM


def _mod_kernel(cv_ref, w_ref, b_ref, o_ref):
    s = _silu(cv_ref[...])
    o_ref[0] = _dot(s.astype(BF16), w_ref[0].astype(BF16)) + b_ref[0]


def _modulation(cvecs, w_ada, b_ada):
    depth, d, d3 = w_ada.shape
    nblk = d3 // d
    return pl.pallas_call(
        _mod_kernel,
        grid=(depth, nblk),
        in_specs=[pl.BlockSpec((8, d), lambda l, n: (0, 0)),
                  pl.BlockSpec((1, d, d), lambda l, n: (l, 0, n)),
                  pl.BlockSpec((1, 1, d), lambda l, n: (l, 0, n))],
        out_specs=pl.BlockSpec((1, 8, d), lambda l, n: (l, 0, n)),
        out_shape=jax.ShapeDtypeStruct((depth, 8, d3), F32),
        name="modulation",
    )(cvecs, w_ada, b_ada.reshape(depth, 1, d3))


def _modulated_norm(x, mod_ref, nw_ref, row, d):
    m = mod_ref[pl.ds(row, 1), :]
    sh, sc = m[:, :d], m[:, d:2 * d]
    ms = jnp.mean(x * x, axis=-1, keepdims=True)
    xn = x * lax.rsqrt(ms + EPS) * nw_ref[...]
    return xn * (1.0 + sc) + sh, m[:, 2 * d:]


def _head_norm_rope(xs, wn, cos, sin, scale):
    sq = xs * xs
    lo = _lane_lo()
    ms = jnp.where(lo, jnp.sum(jnp.where(lo, sq, 0.0), axis=1, keepdims=True),
                   jnp.sum(jnp.where(lo, 0.0, sq), axis=1, keepdims=True)) * (1.0 / HEAD_DIM)
    y = xs * lax.rsqrt(ms + EPS) * wn
    lane = lax.broadcasted_iota(jnp.int32, (1, LANES), 1)
    first = (lane % 32) < 16
    sw = jnp.where(first, pltpu.roll(y, LANES - 16, axis=1), pltpu.roll(y, 16, axis=1))
    y = y * cos + sw * sin
    return y * scale if scale != 1.0 else y


def _proj_kernel(xc_ref, *refs, n_batch, tiles):
    xl_refs, refs = refs[:tiles], refs[tiles:]
    (mod_ref, nw_ref, win_ref, cos_ref, sin_ref, qna_ref, kna_ref, qnd_ref, knd_ref, wsp_ref,
     bsp_ref, qat_ref, ka_ref, vat_ref, za_ref, qdt_ref, kd_ref, vdt_ref, zd_ref,
     fb_ref, fbc_ref, zb_ref, yc_ref) = refs
    b, j = pl.program_id(0), pl.program_id(1)
    d = xc_ref.shape[-1]
    tm = xc_ref.shape[1]
    q_scale = HEAD_DIM ** -0.5 * LOG2_E
    group = lax.broadcasted_iota(jnp.int32, (1, BRANCH_W), 1) // GROUP_W
    sub = DFT_BLOCK

    for u in range(tiles):
        tok = slice(u * tm, (u + 1) * tm)
        if u == 0:
            x = jnp.where(j == 0, xc_ref[0], xl_refs[0][0])
            row = jnp.where(j == 0, n_batch, b)
        else:
            x, row = xl_refs[u][0], b
        h, _ = _modulated_norm(x, mod_ref, nw_ref, row, d)
        hb = h.astype(BF16)
        pc = _dot(hb, win_ref[:, _CU:])
        p = _dot(hb, win_ref[:, :_CU])

        gu = _gelu(pc[:, :BRANCH_W])
        gv = _gelu(pc[:, BRANCH_W:2 * BRANCH_W]).astype(BF16)
        zc = _silu(pc[:, 2 * BRANCH_W:])
        for c in range(tm // CHUNK):
            rows = slice(c * CHUNK, (c + 1) * CHUNK)
            sp = bsp_ref[...]
            for g in range(BRANCH_W // GROUP_W):
                sp = sp + jnp.where(group == g, _dot(wsp_ref[g], gv[rows]), 0.0)
            yc_ref[0, u * tm + c * CHUNK:u * tm + (c + 1) * CHUNK, :] = (
                gu[rows] * sp * zc[rows]).astype(BF16)

        cos, sin = cos_ref[tok, :], sin_ref[tok, :]

        def attn_branch(q0, k0, v0, z0, qn_ref, kn_ref, qt_ref, k_ref, vt_ref, z_ref):
            q = [_head_norm_rope(p[:, q0 + s * LANES:q0 + (s + 1) * LANES], qn_ref[...],
                                 cos, sin, q_scale) for s in range(BRANCH_W // LANES)]
            qt_ref[0, :, tok] = jnp.concatenate(q, axis=1).T.astype(BF16)
            k_ref[0, tok, :] = _head_norm_rope(p[:, k0:k0 + KV_W], kn_ref[...], cos, sin,
                                               1.0).astype(BF16)
            vt = p[:, v0:v0 + KV_W].T
            ones = jnp.ones((V_ROWS - HEAD_DIM, tm), F32)
            vt_ref[0, :, tok] = jnp.concatenate([vt[:HEAD_DIM], ones, vt[HEAD_DIM:], ones],
                                                axis=0).astype(BF16)
            z_ref[0, tok, :] = _silu(p[:, z0:z0 + BRANCH_W])

        attn_branch(_AQ, _AK, _AV, _AZ, qna_ref, kna_ref, qat_ref, ka_ref, vat_ref, za_ref)
        attn_branch(_DQ, _DK, _DV, _DZ, qnd_ref, knd_ref, qdt_ref, kd_ref, vdt_ref, zd_ref)

        f = p[:, _BF:_BF + BRANCH_W]
        per_hi = tm // (DFT_INNER // sub)
        for hi in range(DFT_INNER // sub):
            rows = jnp.concatenate([f[n1 * DFT_INNER + hi * sub:n1 * DFT_INNER + (hi + 1) * sub]
                                    for n1 in range(tm // DFT_INNER)], axis=0)
            for half in range(BRANCH_W // LANES):
                fb_ref[0, half, hi, u * per_hi:(u + 1) * per_hi, :] = (
                    rows[:, half * LANES:(half + 1) * LANES])
        if u == 0:
            @pl.when(j == 0)
            def _():
                fbc_ref[0] = f

        zb_ref[0, tok, :] = _silu(p[:, _BZ:_BZ + BRANCH_W])


def _projection(layer, xc, xl, mod, nw, w_in, cos_t, sin_t, qna, kna, qnd, knd, w_sp, b_sp_t):
    nb, seq, d = xl.shape
    tm = TOKEN_TILE
    tiles = PROJ_TILES
    step = tiles * tm
    t = seq + xc.shape[1]
    tok = lambda w, dt: jax.ShapeDtypeStruct((nb, t, w), dt)
    tok_spec = lambda w: pl.BlockSpec((1, step, w), lambda b, j: (b, j, 0))
    tr = lambda w: jax.ShapeDtypeStruct((nb, w, t), BF16)
    tr_spec = lambda w: pl.BlockSpec((1, w, step), lambda b, j: (b, 0, j))
    const = lambda a: pl.BlockSpec(a.shape, lambda b, j: (0,) * a.ndim)
    of_layer = lambda a: pl.BlockSpec((None,) + a.shape[1:],
                                      lambda b, j: (layer,) + (0,) * (a.ndim - 1))
    lat_tile = lambda u: pl.BlockSpec(
        (1, tm, d), lambda b, j: (b, jnp.clip(tiles * j + u - 1, 0, seq // tm - 1), 0))
    attn_specs = [tr_spec(BRANCH_W), tok_spec(KV_W), tr_spec(2 * V_ROWS), tok_spec(BRANCH_W)]
    attn_shapes = [tr(BRANCH_W), tok(KV_W, BF16), tr(2 * V_ROWS), tok(BRANCH_W, F32)]
    n_hi = DFT_INNER // DFT_BLOCK
    return pl.pallas_call(
        functools.partial(_proj_kernel, n_batch=nb, tiles=tiles),
        grid=(nb, t // step),
        in_specs=[pl.BlockSpec((1, tm, d), lambda b, j: (b, 0, 0))]
                 + [lat_tile(u) for u in range(tiles)]
                 + [of_layer(mod), const(nw), of_layer(w_in),
                    pl.BlockSpec((step, LANES), lambda b, j: (j, 0)),
                    pl.BlockSpec((step, LANES), lambda b, j: (j, 0)),
                    const(qna), const(kna), const(qnd), const(knd), of_layer(w_sp),
                    const(b_sp_t)],
        out_specs=attn_specs * 2 + [
            pl.BlockSpec((1, BRANCH_W // LANES, n_hi, step // n_hi, LANES),
                         lambda b, j: (b, 0, 0, j, 0)),
            pl.BlockSpec((1, tm, BRANCH_W), lambda b, j: (b, 0, 0)),
            tok_spec(BRANCH_W), tok_spec(BRANCH_W)],
        out_shape=attn_shapes * 2 + [
            jax.ShapeDtypeStruct((nb, BRANCH_W // LANES, n_hi, t // n_hi, LANES), F32),
            jax.ShapeDtypeStruct((nb, tm, BRANCH_W), F32),
            tok(BRANCH_W, F32), tok(BRANCH_W, BF16)],
        compiler_params=pltpu.CompilerParams(vmem_limit_bytes=VMEM_LIMIT),
        name="projection",
    )(xc, *([xl] * tiles), mod, nw, w_in, cos_t, sin_t, qna, kna, qnd, knd, w_sp, b_sp_t)


def _group_queries(qt_ref, g, tokens=slice(None)):
    r0 = g * LANES
    qrow = jnp.concatenate([qt_ref[0, r0:r0 + HEAD_DIM, tokens],
                            qt_ref[0, r0 + HEAD_DIM:r0 + LANES, tokens]], axis=1)
    zeros = jnp.zeros_like(qrow)
    return jnp.concatenate([qrow, zeros] if g == 0 else [zeros, qrow], axis=0)


def _store_heads(o_ref, z_ref, g, o_t, tokens=slice(None)):
    tq = o_t.shape[1] // 2
    o = jnp.concatenate([o_t[:, :tq], o_t[:, tq:]], axis=0).T
    cols = slice(g * LANES, (g + 1) * LANES)
    o_ref[0, tokens, cols] = (o * z_ref[0, tokens, cols]).astype(BF16)


def _attn_a_kernel(qt_ref, k_ref, vt_ref, z_ref, o_ref, s_ref, smax_ref, m_ref, acc_ref,
                   *, tk, ctx_len, tq):
    tiles = qt_ref.shape[2] // tq
    t = k_ref.shape[1]
    n_lat = (t - ctx_len) // tk
    groups = range(2)

    def values(g, start, size):
        return vt_ref[0, g * V_ROWS:(g + 1) * V_ROWS, start:start + size]

    class Tile:
        def __init__(self, u):
            self.tok = slice(u * tq, (u + 1) * tq)
            self.par = u % 2

        def scores(self, g, start, size):
            return _dot(k_ref[0, start:start + size, :], self.qt_g[g])

        def produce(self, g, c):
            s = self.scores(g, ctx_len + c * tk, tk)
            smax_ref[c % 2, g] = jnp.max(s, axis=0, keepdims=True)
            s_ref[c % 2, g] = s

        def consume(self, g, c):
            m = m_ref[self.par, g]
            m_new = jnp.maximum(m, smax_ref[c % 2, g])
            pr = jnp.exp2(s_ref[c % 2, g] - m_new).astype(BF16)
            acc_ref[self.par, g] = (jnp.exp2(m - m_new) * acc_ref[self.par, g]
                                    + _dot(values(g, ctx_len + c * tk, tk), pr))
            m_ref[self.par, g] = m_new

        def head_scores(self):
            self.qt_g = [_group_queries(qt_ref, g, self.tok) for g in groups]
            self.s_ctx = [self.scores(g, 0, ctx_len) for g in groups]
            for g in groups:
                self.produce(g, 0)

        def head_finish(self):
            for g in groups:
                m = jnp.max(self.s_ctx[g], axis=0, keepdims=True)
                m_ref[self.par, g] = m
                acc_ref[self.par, g] = _dot(values(g, 0, ctx_len),
                                            jnp.exp2(self.s_ctx[g] - m).astype(BF16))

        def body(self):
            for c in range(n_lat - 1):
                for g in groups:
                    self.produce(g, c + 1)
                    self.consume(g, c)

        def last(self):
            for g in groups:
                self.consume(g, n_lat - 1)

        def store(self):
            for g in groups:
                acc = acc_ref[self.par, g]
                _store_heads(o_ref, z_ref, g, acc[:HEAD_DIM] / acc[HEAD_DIM:HEAD_DIM + 1],
                             self.tok)

    tile = Tile(0)
    tile.head_scores()
    tile.head_finish()

    @pl.when(pl.program_id(1) > 0)
    def _():
        tile.body()
        tile.last()

    for u in range(1, tiles):
        nxt = Tile(u)
        nxt.head_scores()
        if u > 1:
            tile.last()
        tile.store()
        nxt.head_finish()
        nxt.body()
        tile = nxt
    if tiles > 1:
        tile.last()
    tile.store()


def _attention_a(qt, k, vt, z, ctx_len):
    nb, t, _ = k.shape
    tq = TOKEN_TILE
    tk = ATTN_A_KEYS
    step = ATTN_A_TILES * tq
    return pl.pallas_call(
        functools.partial(_attn_a_kernel, tk=tk, ctx_len=ctx_len, tq=tq),
        grid=(nb, t // step),
        in_specs=[pl.BlockSpec((1, BRANCH_W, step), lambda b, j: (b, 0, j)),
                  pl.BlockSpec((1, t, KV_W), lambda b, j: (b, 0, 0)),
                  pl.BlockSpec((1, 2 * V_ROWS, t), lambda b, j: (b, 0, 0)),
                  pl.BlockSpec((1, step, BRANCH_W), lambda b, j: (b, j, 0))],
        out_specs=pl.BlockSpec((1, step, BRANCH_W), lambda b, j: (b, j, 0)),
        out_shape=jax.ShapeDtypeStruct((nb, t, BRANCH_W), BF16),
        scratch_shapes=[pltpu.VMEM((2, 2, tk, 2 * tq), F32),
                        pltpu.VMEM((2, 2, 1, 2 * tq), F32),
                        pltpu.VMEM((2, 2, 1, 2 * tq), F32),
                        pltpu.VMEM((2, 2, V_ROWS, 2 * tq), F32)],
        compiler_params=pltpu.CompilerParams(vmem_limit_bytes=VMEM_LIMIT),
        name="attention_a",
    )(qt, k, vt, z)


def _attn_d_kernel(sink_ref, qt_ref, k_ref, vt_ref, z_ref, o_ref, *, ctx_len, tq):
    tiles = qt_ref.shape[2] // tq
    t = k_ref.shape[1]
    span = tq + 2 * WINDOW
    first = lax.broadcasted_iota(jnp.int32, (1, 2 * tq), 1) < tq
    rel = (lax.broadcasted_iota(jnp.int32, (span, 2 * tq), 0)
           - lax.broadcasted_iota(jnp.int32, (span, 2 * tq), 1) % tq)

    def scores(u):
        tile = pl.program_id(1) * tiles + u
        q0 = tile * tq
        start = pl.multiple_of(jnp.clip(q0 - WINDOW, ctx_len, t - span), LANES)
        mask = (jnp.abs(rel + (start - q0)) <= WINDOW) & (tile > 0)
        sc = []
        for g in range(2):
            qt_g = _group_queries(qt_ref, g, slice(u * tq, (u + 1) * tq))
            sc.append((_dot(k_ref[0, :ctx_len, :], qt_g),
                       _dot(k_ref[0, pl.ds(start, span), :], qt_g)))
        return start, mask, sc

    def finish(u, start, mask, sc):
        for g in range(2):
            vrows = slice(g * V_ROWS, (g + 1) * V_ROWS)
            s_ctx = sc[g][0]
            s_loc = jnp.where(mask, sc[g][1], -jnp.inf)
            sink = jnp.where(first, sink_ref[2 * g], sink_ref[2 * g + 1]) * LOG2_E
            m = jnp.maximum(jnp.maximum(jnp.max(s_ctx, axis=0, keepdims=True),
                                        jnp.max(s_loc, axis=0, keepdims=True)), sink)
            acc = (_dot(vt_ref[0, vrows, :ctx_len], jnp.exp2(s_ctx - m).astype(BF16))
                   + _dot(vt_ref[0, vrows, pl.ds(start, span)], jnp.exp2(s_loc - m).astype(BF16)))
            l = acc[HEAD_DIM:HEAD_DIM + 1] + jnp.exp2(sink - m)
            _store_heads(o_ref, z_ref, g, acc[:HEAD_DIM] / l, slice(u * tq, (u + 1) * tq))

    pending = scores(0)
    for u in range(tiles):
        ready, pending = pending, (scores(u + 1) if u + 1 < tiles else None)
        finish(u, *ready)


def _attention_d(sink, qt, k, vt, z, ctx_len):
    nb, t, _ = k.shape
    tq = TOKEN_TILE
    step = ATTN_D_TILES * tq
    return pl.pallas_call(
        functools.partial(_attn_d_kernel, ctx_len=ctx_len, tq=tq),
        grid=(nb, t // step),
        in_specs=[pl.BlockSpec(memory_space=pltpu.SMEM),
                  pl.BlockSpec((1, BRANCH_W, step), lambda b, j: (b, 0, j)),
                  pl.BlockSpec((1, t, KV_W), lambda b, j: (b, 0, 0)),
                  pl.BlockSpec((1, 2 * V_ROWS, t), lambda b, j: (b, 0, 0)),
                  pl.BlockSpec((1, step, BRANCH_W), lambda b, j: (b, j, 0))],
        out_specs=pl.BlockSpec((1, step, BRANCH_W), lambda b, j: (b, j, 0)),
        out_shape=jax.ShapeDtypeStruct((nb, t, BRANCH_W), BF16),
        compiler_params=pltpu.CompilerParams(vmem_limit_bytes=VMEM_LIMIT),
        name="attention_d",
    )(sink, qt, k, vt, z)


def _dft_tables(seq_len, ctx_len):
    n2c = DFT_INNER
    n1c = seq_len // n2c
    scale = 1.0 / np.sqrt(float(seq_len) * GROUP_W)
    k1 = np.arange(n1c)[:, None]
    n1 = np.arange(n1c)[None, :]
    m1 = np.zeros((n2c, 2 * n1c, n1c), np.float64)
    for n2 in range(n2c):
        ang = -2.0 * np.pi * (k1 * n1 / n1c + n2 * k1 / seq_len)
        m1[n2, :n1c] = np.cos(ang) * scale
        m1[n2, n1c:] = np.sin(ang) * scale
    ang3 = 2.0 * np.pi * np.outer(np.arange(n2c), np.arange(n2c)) / n2c
    c3, s3 = np.cos(ang3), np.sin(ang3)
    m3 = np.block([[c3, s3], [-s3, c3]])
    cscale = 1.0 / np.sqrt(float(ctx_len) * GROUP_W)
    angc = 2.0 * np.pi * np.outer(np.arange(ctx_len), np.arange(ctx_len)) / ctx_len
    mc = np.concatenate([np.cos(angc), -np.sin(angc)], axis=0) * cscale
    angg = 2.0 * np.pi * np.outer(np.arange(GROUP_W), np.arange(GROUP_W)) / GROUP_W
    eye = np.eye(BRANCH_W // GROUP_W)
    chan = np.concatenate([np.kron(eye, np.cos(angg)), np.kron(eye, np.sin(angg))], axis=0)
    return _stacked(m1), _stacked(m3), _split(mc), _split(chan)


def _stacked(a):
    hi, lo = _split(a)
    return jnp.concatenate([hi, lo, hi], axis=-1)


def _dot3_stacked(m3, x):
    x_hi = x.astype(BF16)
    x_lo = (x - x_hi.astype(F32)).astype(BF16)
    return _dot(m3, jnp.concatenate([x_hi, x_hi, x_lo], axis=0))


def _split(a):
    a32 = jnp.asarray(a.astype(np.float32))
    hi = a32.astype(BF16)
    return hi, (a32 - hi.astype(F32)).astype(BF16)


def _dot3_left(m_hi, m_lo, x):
    x_hi = x.astype(BF16)
    x_lo = (x - x_hi.astype(F32)).astype(BF16)
    return _dot(m_hi, x_hi) + _dot(m_lo, x_hi) + _dot(m_hi, x_lo)


def _dot3_right(x, m_hi, m_lo):
    x_hi = x.astype(BF16)
    x_lo = (x - x_hi.astype(F32)).astype(BF16)
    return _dot(x_hi, m_hi) + _dot(x_hi, m_lo) + _dot(x_lo, m_hi)


def _fnet_dft_kernel(f_ref, m1_ref, fc_ref, m3_ref, mch_ref, mcl_ref,
                     pr_ref, pi_ref, prc_ref, pic_ref, br_ref, bi_ref, *, n_hi):
    i = pl.program_id(1)
    n2c = DFT_INNER
    sub = DFT_BLOCK
    halves = range(f_ref.shape[1])
    rows_b = br_ref.shape[2]
    n1c = rows_b // sub
    skip = f_ref.shape[3] - rows_b

    @pl.when(i < n_hi)
    def _():
        for q in range(sub):
            rows = pl.ds(q, n1c, stride=sub)
            x = jnp.concatenate([f_ref[0, h, 0, pl.ds(skip + q, n1c, stride=sub), :]
                                 for h in halves], axis=1)
            res = _dot3_stacked(m1_ref[i * sub + q], x)
            for h in halves:
                br_ref[h, i, rows, :] = res[:n1c, h * LANES:(h + 1) * LANES]
                bi_ref[h, i, rows, :] = res[n1c:, h * LANES:(h + 1) * LANES]

    @pl.when(i == n_hi)
    def _():
        ctx_len = fc_ref.shape[1]
        res = _dot3_left(mch_ref[...], mcl_ref[...], fc_ref[0])
        prc_ref[0] = res[:ctx_len]
        pic_ref[0] = res[ctx_len:]

    @pl.when(i >= n_hi)
    def _():
        for kb in range(pr_ref.shape[2]):
            base = pl.multiple_of(((i - n_hi) * pr_ref.shape[2] + kb) * (sub * sub), sub * sub)
            for q in range(sub):
                xin = jnp.concatenate(
                    [jnp.concatenate([ref[h, hi, pl.ds(base + q * sub, sub), :] for h in halves],
                                     axis=1)
                     for ref in (br_ref, bi_ref) for hi in range(n_hi)], axis=0)
                res = _dot3_stacked(m3_ref[...], xin)
                rows = pl.ds(q, n2c, stride=sub)
                for h in halves:
                    pr_ref[0, h, kb, rows, :] = res[:n2c, h * LANES:(h + 1) * LANES]
                    pi_ref[0, h, kb, rows, :] = res[n2c:, h * LANES:(h + 1) * LANES]


def _fnet_dft(fb, fb_ctx, m1, m3, mc):
    nb, nh, n_hi, rows_in, w = fb.shape
    ctx_len = fb_ctx.shape[1]
    n2c = DFT_INNER
    sub = DFT_BLOCK
    rows = rows_in - ctx_len // n_hi
    n1c = rows // sub
    n_kb = n1c // sub
    kb_step = DFT_OUT_BLOCKS
    first = lambda i: jnp.minimum(i, n_hi - 1)
    second = lambda i: jnp.maximum(i - n_hi, 0)
    const = lambda a: pl.BlockSpec(a.shape, lambda b, i: (0,) * a.ndim)
    cblk = pl.BlockSpec((1, ctx_len, nh * w), lambda b, i: (b, 0, 0))
    oblk = pl.BlockSpec((1, nh, kb_step, n2c * sub, w), lambda b, i: (b, 0, second(i), 0, 0))
    p4 = jax.ShapeDtypeStruct((nb, nh, n_kb, n2c * sub, w), F32)
    pc = jax.ShapeDtypeStruct((nb, ctx_len, nh * w), F32)
    return pl.pallas_call(
        functools.partial(_fnet_dft_kernel, n_hi=n_hi),
        grid=(nb, n_hi + n_kb // kb_step),
        in_specs=[pl.BlockSpec((1, nh, 1, rows_in, w), lambda b, i: (b, 0, first(i), 0, 0)),
                  const(m1),
                  cblk, const(m3), const(mc[0]), const(mc[1])],
        out_specs=[oblk, oblk, cblk, cblk],
        out_shape=[p4, p4, pc, pc],
        scratch_shapes=[pltpu.VMEM((nh, n_hi, rows, w), F32),
                        pltpu.VMEM((nh, n_hi, rows, w), F32)],
        compiler_params=pltpu.CompilerParams(vmem_limit_bytes=VMEM_LIMIT,
                                             dimension_semantics=("arbitrary", "arbitrary")),
        name="fnet_dft",
    )(fb, m1, fb_ctx, m3, *mc)


def _merge_kernel(xc_ref, xl_ref, mod_ref, nw_ref, ya_ref, yd_ref, pr_ref, pi_ref, prc_ref,
                  pic_ref, zb_ref, yc_ref, chh_ref, chl_ref, wf_ref, wbr_ref, wm_ref, bm_ref,
                  wo_ref, *out_refs, n_batch, tile_off):
    b, j = pl.program_id(0), pl.program_id(1)
    d = xl_ref.shape[-1]
    is_ctx = j + tile_off == 0
    x = jnp.where(is_ctx, xc_ref[0], xl_ref[0]) if tile_off == 0 else xl_ref[0]
    row = jnp.where(is_ctx, n_batch, b)
    h, gate = _modulated_norm(x, mod_ref, nw_ref, row, d)
    def latent_rows(ref):
        sub = DFT_BLOCK
        return jnp.concatenate(
            [jnp.concatenate([ref[0, h, kb, k2 * sub:(k2 + 1) * sub, :]
                              for k2 in range(ref.shape[3] // sub)
                              for kb in range(ref.shape[2])], axis=0)
             for h in range(ref.shape[1])], axis=1)

    pc = jnp.concatenate([jnp.where(is_ctx, prc_ref[0], latent_rows(pr_ref)),
                          jnp.where(is_ctx, pic_ref[0], latent_rows(pi_ref))], axis=1)
    yb = _dot(_dot3_right(pc, chh_ref[...], chl_ref[...]).astype(BF16), wf_ref[...]) * zb_ref[0]
    branches = (ya_ref[0], yd_ref[0], yb.astype(BF16), yc_ref[0])
    hb = h.astype(BF16)
    acc = jnp.zeros((x.shape[0], d), F32)
    for r in range(N_BRANCH):
        g = jax.nn.sigmoid(_dot(hb, wm_ref[:, r * d:(r + 1) * d]) + bm_ref[:, r * d:(r + 1) * d])
        acc = acc + g * _dot(branches[r], wbr_ref[r])
    out = x + gate * _dot(acc.astype(BF16), wo_ref[...])
    out_refs[-1][0] = out
    if tile_off == 0:
        @pl.when(is_ctx)
        def _():
            out_refs[0][0] = out


def _merge(layer, xc, xl, mod, nw, ya, yd, pr, pi, prc, pic, zb, yc, chan, wf_bd, w_br, w_merge,
           b_merge, w_out, skip_ctx):
    nb, seq, d = xl.shape
    tm = TOKEN_TILE
    off = 1 if skip_ctx else 0
    nt = (seq + xc.shape[1]) // tm - off
    tok_spec = lambda w: pl.BlockSpec((1, tm, w), lambda b, j: (b, j + off, 0))
    ctx_tile = pl.BlockSpec((1, tm, d), lambda b, j: (b, 0, 0))
    lat_tile = pl.BlockSpec((1, tm, d), lambda b, j: (b, jnp.maximum(j + off - 1, 0), 0))
    xc_shape = jax.ShapeDtypeStruct(xc.shape, F32)
    xl_shape = jax.ShapeDtypeStruct(xl.shape, F32)
    lat_spec = lambda a: pl.BlockSpec((1, a.shape[1], a.shape[2], tm // a.shape[2], a.shape[4]),
                                      lambda b, j: (b, 0, 0, jnp.maximum(j + off - 1, 0), 0))
    ctx_spec = lambda a: pl.BlockSpec((1,) + a.shape[1:], lambda b, j: (b, 0, 0))
    const = lambda a: pl.BlockSpec(a.shape, lambda b, j: (0,) * a.ndim)
    of_layer = lambda a: pl.BlockSpec((None,) + a.shape[1:],
                                      lambda b, j: (layer,) + (0,) * (a.ndim - 1))
    return pl.pallas_call(
        functools.partial(_merge_kernel, n_batch=nb, tile_off=off),
        grid=(nb, nt),
        in_specs=[ctx_tile, lat_tile, of_layer(mod), const(nw), tok_spec(BRANCH_W),
                  tok_spec(BRANCH_W), lat_spec(pr), lat_spec(pi), ctx_spec(prc), ctx_spec(pic),
                  tok_spec(BRANCH_W), tok_spec(BRANCH_W), const(chan[0]), const(chan[1]),
                  const(wf_bd), of_layer(w_br), of_layer(w_merge), const(b_merge),
                  of_layer(w_out)],
        out_specs=[lat_tile] if skip_ctx else [ctx_tile, lat_tile],
        out_shape=[xl_shape] if skip_ctx else [xc_shape, xl_shape],
        compiler_params=pltpu.CompilerParams(vmem_limit_bytes=VMEM_LIMIT),
        name="merge",
    )(xc, xl, mod, nw, ya, yd, pr, pi, prc, pic, zb, yc, *chan, wf_bd, w_br, w_merge, b_merge,
      w_out)


def _rope_tables(seq_len, ctx_len):
    t = jnp.arange(seq_len, dtype=jnp.int32)
    r = (t // GRID_W).astype(F32)
    col = (t % GRID_W).astype(F32)
    nf = HEAD_DIM // 4
    inv = ROPE_BASE ** (-jnp.arange(nf, dtype=F32) / nf)
    ar = r[:, None] * inv[None, :]
    ac = col[:, None] * inv[None, :]
    cr, sr, cc, sc = jnp.cos(ar), jnp.sin(ar), jnp.cos(ac), jnp.sin(ac)
    cos_h = jnp.concatenate([cr, cr, cc, cc], axis=1)
    sin_h = jnp.concatenate([-sr, sr, -sc, sc], axis=1)
    cos_t = jnp.concatenate([jnp.ones((ctx_len, HEAD_DIM), F32), cos_h], axis=0)
    sin_t = jnp.concatenate([jnp.zeros((ctx_len, HEAD_DIM), F32), sin_h], axis=0)
    return jnp.tile(cos_t, (1, 2)), jnp.tile(sin_t, (1, 2))


def kernel(x, c, ctx, c_ctx, norm_w, w_ada, b_ada, w_in, qn_a, kn_a, qn_d, kn_d, sink_d,
           w_fnet, w_sp, b_sp, w_br, w_merge, b_merge, w_out):
    nb, seq, d = x.shape
    ctx_len = ctx.shape[1]
    depth = norm_w.shape[0]
    assert ctx_len == TOKEN_TILE and seq % (2 * ATTN_A_KEYS) == 0 and nb < 8
    assert (seq + ctx_len) % (ATTN_A_TILES * TOKEN_TILE) == 0
    assert (seq + ctx_len) % (ATTN_D_TILES * TOKEN_TILE) == 0
    assert (seq + ctx_len) % (PROJ_TILES * TOKEN_TILE) == 0

    cvecs = jnp.zeros((8, d), F32).at[:nb].set(c).at[nb].set(c_ctx)
    mods = _modulation(cvecs, w_ada, b_ada)

    cos_t, sin_t = _rope_tables(seq, ctx_len)
    m1, m3, mc, chan = _dft_tables(seq, ctx_len)
    pair = lambda w: jnp.tile(w, 2).reshape(1, LANES)
    eye_g = jnp.eye(BRANCH_W // GROUP_W, dtype=F32)

    w_in, w_sp, w_br, w_merge, w_out = (w.astype(BF16) for w in (w_in, w_sp, w_br, w_merge, w_out))
    xc, xl = ctx, x
    for l in range(depth):
        nw = norm_w[l].reshape(1, d)
        wf_bd = jnp.einsum('gh,gcd->gchd', eye_g, w_fnet[l]).reshape(BRANCH_W, BRANCH_W)
        b_sp_t = jnp.repeat(b_sp[l].T, GROUP_W, axis=1)
        (qat, ka, vat, za, qdt, kd, vdt, zd, fb, fbc, zb, yc) = _projection(
            l, xc, xl, mods, nw, w_in, cos_t, sin_t,
            pair(qn_a[l]), pair(kn_a[l]), pair(qn_d[l]), pair(kn_d[l]),
            w_sp, b_sp_t)
        ya = _attention_a(qat, ka, vat, za, ctx_len)
        yd = _attention_d(sink_d[l], qdt, kd, vdt, zd, ctx_len)
        pr, pi, prc, pic = _fnet_dft(fb, fbc, m1, m3, mc)
        *xc_new, xl = _merge(l, xc, xl, mods, nw, ya, yd, pr, pi, prc, pic, zb, yc, chan,
                             wf_bd.astype(BF16), w_br, w_merge, b_merge[l].reshape(1, -1), w_out,
                             skip_ctx=(l == depth - 1))
        xc = xc_new[0] if xc_new else None
    return xl
```

```python
import functools

import numpy as np
import jax
import jax.numpy as jnp
from jax import lax
from jax.experimental import pallas as pl
from jax.experimental.pallas import tpu as pltpu

F32 = jnp.float32
BF16 = jnp.bfloat16

GRID_W = 64
HEAD_DIM = 64
BRANCH_W = 256
KV_W = 128
N_BRANCH = 4
GROUP_W = 64
CHUNK = 128
Q_BLOCK = 128
WINDOW = 128
ROPE_BASE = 10000.0
EPS = 1e-6
LOG2_E = float(np.log2(np.e))
LANES = 128
TOKEN_TILE = 256
V_ROWS = 80
ATTN_A_KEYS = 512
ATTN_A_TILES = 3
ATTN_D_TILES = 11
PROJ_TILES = 3
MERGE_TILES = 4
DFT_BLOCK = 8
DFT_OUT_BLOCKS = 4
DFT_INNER = 64
VMEM_LIMIT = 56 * 1024 * 1024

_IN_SIZES = (BRANCH_W, KV_W, KV_W, BRANCH_W, BRANCH_W, KV_W, KV_W, BRANCH_W,
             BRANCH_W, BRANCH_W, BRANCH_W, BRANCH_W, BRANCH_W)
_IN_OFF = tuple(int(v) for v in np.cumsum((0,) + _IN_SIZES))
IN_W = _IN_OFF[-1]
(_AQ, _AK, _AV, _AZ, _DQ, _DK, _DV, _DZ, _BF, _BZ, _CU, _CV, _CZ) = _IN_OFF[:-1]


def _silu(z):
    return z * jax.nn.sigmoid(z)


def _gelu(x):
    return 0.5 * x * (1.0 + lax.erf(x * np.float32(np.sqrt(0.5))))


def _dot(a, b):
    return jnp.dot(a, b, preferred_element_type=F32)


def _lane_lo(width=LANES):
    lane = lax.broadcasted_iota(jnp.int32, (1, width), 1)
    return (lane % LANES) < HEAD_DIM


def _mod_kernel(cv_ref, w_ref, b_ref, o_ref):
    s = _silu(cv_ref[...])
    o_ref[0] = _dot(s.astype(BF16), w_ref[0].astype(BF16)) + b_ref[0]


def _modulation(cvecs, w_ada, b_ada):
    depth, d, d3 = w_ada.shape
    nblk = d3 // d
    return pl.pallas_call(
        _mod_kernel,
        grid=(depth, nblk),
        in_specs=[pl.BlockSpec((8, d), lambda l, n: (0, 0)),
                  pl.BlockSpec((1, d, d), lambda l, n: (l, 0, n)),
                  pl.BlockSpec((1, 1, d), lambda l, n: (l, 0, n))],
        out_specs=pl.BlockSpec((1, 8, d), lambda l, n: (l, 0, n)),
        out_shape=jax.ShapeDtypeStruct((depth, 8, d3), F32),
        name="modulation",
    )(cvecs, w_ada, b_ada.reshape(depth, 1, d3))


def _modulated_norm(x, mod_ref, nw_ref, row, d):
    m = mod_ref[pl.ds(row, 1), :]
    sh, sc = m[:, :d], m[:, d:2 * d]
    ms = jnp.mean(x * x, axis=-1, keepdims=True)
    xn = x * lax.rsqrt(ms + EPS) * nw_ref[...]
    return xn * (1.0 + sc) + sh, m[:, 2 * d:]


def _head_norm_rope(xs, wn, cos, sin, scale):
    sq = xs * xs
    lo = _lane_lo()
    ms = jnp.where(lo, jnp.sum(jnp.where(lo, sq, 0.0), axis=1, keepdims=True),
                   jnp.sum(jnp.where(lo, 0.0, sq), axis=1, keepdims=True)) * (1.0 / HEAD_DIM)
    y = xs * lax.rsqrt(ms + EPS) * wn
    lane = lax.broadcasted_iota(jnp.int32, (1, LANES), 1)
    first = (lane % 32) < 16
    sw = jnp.where(first, pltpu.roll(y, LANES - 16, axis=1), pltpu.roll(y, 16, axis=1))
    y = y * cos + sw * sin
    return y * scale if scale != 1.0 else y


def _proj_kernel(xc_ref, *refs, n_batch, tiles):
    xl_refs, refs = refs[:tiles], refs[tiles:]
    (mod_ref, nw_ref, win_ref, cos_ref, sin_ref, qna_ref, kna_ref, qnd_ref, knd_ref, wsp_ref,
     bsp_ref, qat_ref, ka_ref, vat_ref, za_ref, qdt_ref, kd_ref, vdt_ref, zd_ref,
     fb_ref, fbc_ref, zb_ref, yc_ref) = refs
    b, j = pl.program_id(0), pl.program_id(1)
    d = xc_ref.shape[-1]
    tm = xc_ref.shape[1]
    q_scale = HEAD_DIM ** -0.5 * LOG2_E
    group = lax.broadcasted_iota(jnp.int32, (1, BRANCH_W), 1) // GROUP_W
    sub = DFT_BLOCK

    for u in range(tiles):
        tok = slice(u * tm, (u + 1) * tm)
        if u == 0:
            x = jnp.where(j == 0, xc_ref[0], xl_refs[0][0])
            row = jnp.where(j == 0, n_batch, b)
        else:
            x, row = xl_refs[u][0], b
        h, _ = _modulated_norm(x, mod_ref, nw_ref, row, d)
        hb = h.astype(BF16)
        pc = _dot(hb, win_ref[:, _CU:])
        p = _dot(hb, win_ref[:, :_CU])

        gu = _gelu(pc[:, :BRANCH_W])
        gv = _gelu(pc[:, BRANCH_W:2 * BRANCH_W]).astype(BF16)
        zc = _silu(pc[:, 2 * BRANCH_W:])
        for c in range(tm // CHUNK):
            rows = slice(c * CHUNK, (c + 1) * CHUNK)
            sp = bsp_ref[...]
            for g in range(BRANCH_W // GROUP_W):
                sp = sp + jnp.where(group == g, _dot(wsp_ref[g], gv[rows]), 0.0)
            yc_ref[0, u * tm + c * CHUNK:u * tm + (c + 1) * CHUNK, :] = (
                gu[rows] * sp * zc[rows]).astype(BF16)

        cos, sin = cos_ref[tok, :], sin_ref[tok, :]

        def attn_branch(q0, k0, v0, z0, qn_ref, kn_ref, qt_ref, k_ref, vt_ref, z_ref):
            q = [_head_norm_rope(p[:, q0 + s * LANES:q0 + (s + 1) * LANES], qn_ref[...],
                                 cos, sin, q_scale) for s in range(BRANCH_W // LANES)]
            qt_ref[0, :, tok] = jnp.concatenate(q, axis=1).T.astype(BF16)
            k_ref[0, tok, :] = _head_norm_rope(p[:, k0:k0 + KV_W], kn_ref[...], cos, sin,
                                               1.0).astype(BF16)
            vt = p[:, v0:v0 + KV_W].T
            ones = jnp.ones((V_ROWS - HEAD_DIM, tm), F32)
            vt_ref[0, :, tok] = jnp.concatenate([vt[:HEAD_DIM], ones, vt[HEAD_DIM:], ones],
                                                axis=0).astype(BF16)
            z_ref[0, tok, :] = _silu(p[:, z0:z0 + BRANCH_W])

        attn_branch(_AQ, _AK, _AV, _AZ, qna_ref, kna_ref, qat_ref, ka_ref, vat_ref, za_ref)
        attn_branch(_DQ, _DK, _DV, _DZ, qnd_ref, knd_ref, qdt_ref, kd_ref, vdt_ref, zd_ref)

        f = p[:, _BF:_BF + BRANCH_W]
        per_hi = tm // (DFT_INNER // sub)
        for hi in range(DFT_INNER // sub):
            rows = jnp.concatenate([f[n1 * DFT_INNER + hi * sub:n1 * DFT_INNER + (hi + 1) * sub]
                                    for n1 in range(tm // DFT_INNER)], axis=0)
            for half in range(BRANCH_W // LANES):
                fb_ref[0, half, hi, u * per_hi:(u + 1) * per_hi, :] = (
                    rows[:, half * LANES:(half + 1) * LANES])
        if u == 0:
            @pl.when(j == 0)
            def _():
                fbc_ref[0] = f

        zb_ref[0, tok, :] = _silu(p[:, _BZ:_BZ + BRANCH_W])


def _projection(layer, xc, xl, mod, nw, w_in, cos_t, sin_t, qna, kna, qnd, knd, w_sp, b_sp_t):
    nb, seq, d = xl.shape
    tm = TOKEN_TILE
    tiles = PROJ_TILES
    step = tiles * tm
    t = seq + xc.shape[1]
    tok = lambda w, dt: jax.ShapeDtypeStruct((nb, t, w), dt)
    tok_spec = lambda w: pl.BlockSpec((1, step, w), lambda b, j: (b, j, 0))
    tr = lambda w: jax.ShapeDtypeStruct((nb, w, t), BF16)
    tr_spec = lambda w: pl.BlockSpec((1, w, step), lambda b, j: (b, 0, j))
    const = lambda a: pl.BlockSpec(a.shape, lambda b, j: (0,) * a.ndim)
    of_layer = lambda a: pl.BlockSpec((None,) + a.shape[1:],
                                      lambda b, j: (layer,) + (0,) * (a.ndim - 1))
    lat_tile = lambda u: pl.BlockSpec(
        (1, tm, d), lambda b, j: (b, jnp.clip(tiles * j + u - 1, 0, seq // tm - 1), 0))
    attn_specs = [tr_spec(BRANCH_W), tok_spec(KV_W), tr_spec(2 * V_ROWS), tok_spec(BRANCH_W)]
    attn_shapes = [tr(BRANCH_W), tok(KV_W, BF16), tr(2 * V_ROWS), tok(BRANCH_W, F32)]
    n_hi = DFT_INNER // DFT_BLOCK
    return pl.pallas_call(
        functools.partial(_proj_kernel, n_batch=nb, tiles=tiles),
        grid=(nb, t // step),
        in_specs=[pl.BlockSpec((1, tm, d), lambda b, j: (b, 0, 0))]
                 + [lat_tile(u) for u in range(tiles)]
                 + [of_layer(mod), const(nw), of_layer(w_in),
                    pl.BlockSpec((step, LANES), lambda b, j: (j, 0)),
                    pl.BlockSpec((step, LANES), lambda b, j: (j, 0)),
                    const(qna), const(kna), const(qnd), const(knd), of_layer(w_sp),
                    const(b_sp_t)],
        out_specs=attn_specs * 2 + [
            pl.BlockSpec((1, BRANCH_W // LANES, n_hi, step // n_hi, LANES),
                         lambda b, j: (b, 0, 0, j, 0)),
            pl.BlockSpec((1, tm, BRANCH_W), lambda b, j: (b, 0, 0)),
            tok_spec(BRANCH_W), tok_spec(BRANCH_W)],
        out_shape=attn_shapes * 2 + [
            jax.ShapeDtypeStruct((nb, BRANCH_W // LANES, n_hi, t // n_hi, LANES), F32),
            jax.ShapeDtypeStruct((nb, tm, BRANCH_W), F32),
            tok(BRANCH_W, F32), tok(BRANCH_W, BF16)],
        compiler_params=pltpu.CompilerParams(vmem_limit_bytes=VMEM_LIMIT),
        name="projection",
    )(xc, *([xl] * tiles), mod, nw, w_in, cos_t, sin_t, qna, kna, qnd, knd, w_sp, b_sp_t)


def _group_queries(qt_ref, g, tokens=slice(None)):
    r0 = g * LANES
    qrow = jnp.concatenate([qt_ref[0, r0:r0 + HEAD_DIM, tokens],
                            qt_ref[0, r0 + HEAD_DIM:r0 + LANES, tokens]], axis=1)
    zeros = jnp.zeros_like(qrow)
    return jnp.concatenate([qrow, zeros] if g == 0 else [zeros, qrow], axis=0)


def _store_heads(o_ref, z_ref, g, o_t, tokens=slice(None)):
    tq = o_t.shape[1] // 2
    o = jnp.concatenate([o_t[:, :tq], o_t[:, tq:]], axis=0).T
    cols = slice(g * LANES, (g + 1) * LANES)
    o_ref[0, tokens, cols] = (o * z_ref[0, tokens, cols]).astype(BF16)


def _attn_a_kernel(qt_ref, k_ref, vt_ref, z_ref, o_ref, s_ref, smax_ref, m_ref, acc_ref,
                   *, tk, ctx_len, tq):
    tiles = qt_ref.shape[2] // tq
    t = k_ref.shape[1]
    n_lat = (t - ctx_len) // tk
    groups = range(2)

    def values(g, start, size):
        return vt_ref[0, g * V_ROWS:(g + 1) * V_ROWS, start:start + size]

    class Tile:
        def __init__(self, u):
            self.tok = slice(u * tq, (u + 1) * tq)
            self.par = u % 2

        def scores(self, g, start, size):
            return _dot(k_ref[0, start:start + size, :], self.qt_g[g])

        def produce(self, g, c):
            s = self.scores(g, ctx_len + c * tk, tk)
            smax_ref[c % 2, g] = jnp.max(s, axis=0, keepdims=True)
            s_ref[c % 2, g] = s

        def consume(self, g, c):
            m = m_ref[self.par, g]
            m_new = jnp.maximum(m, smax_ref[c % 2, g])
            pr = jnp.exp2(s_ref[c % 2, g] - m_new).astype(BF16)
            acc_ref[self.par, g] = (jnp.exp2(m - m_new) * acc_ref[self.par, g]
                                    + _dot(values(g, ctx_len + c * tk, tk), pr))
            m_ref[self.par, g] = m_new

        def head_scores(self):
            self.qt_g = [_group_queries(qt_ref, g, self.tok) for g in groups]
            self.s_ctx = [self.scores(g, 0, ctx_len) for g in groups]
            for g in groups:
                self.produce(g, 0)

        def head_finish(self):
            for g in groups:
                m = jnp.max(self.s_ctx[g], axis=0, keepdims=True)
                m_ref[self.par, g] = m
                acc_ref[self.par, g] = _dot(values(g, 0, ctx_len),
                                            jnp.exp2(self.s_ctx[g] - m).astype(BF16))

        def body(self):
            for c in range(n_lat - 1):
                for g in groups:
                    self.produce(g, c + 1)
                    self.consume(g, c)

        def last(self):
            for g in groups:
                self.consume(g, n_lat - 1)

        def store(self):
            for g in groups:
                acc = acc_ref[self.par, g]
                _store_heads(o_ref, z_ref, g, acc[:HEAD_DIM] / acc[HEAD_DIM:HEAD_DIM + 1],
                             self.tok)

    tile = Tile(0)
    tile.head_scores()
    tile.head_finish()

    @pl.when(pl.program_id(1) > 0)
    def _():
        tile.body()
        tile.last()

    for u in range(1, tiles):
        nxt = Tile(u)
        nxt.head_scores()
        if u > 1:
            tile.last()
        tile.store()
        nxt.head_finish()
        nxt.body()
        tile = nxt
    if tiles > 1:
        tile.last()
    tile.store()


def _attention_a(qt, k, vt, z, ctx_len):
    nb, t, _ = k.shape
    tq = TOKEN_TILE
    tk = ATTN_A_KEYS
    step = ATTN_A_TILES * tq
    return pl.pallas_call(
        functools.partial(_attn_a_kernel, tk=tk, ctx_len=ctx_len, tq=tq),
        grid=(nb, t // step),
        in_specs=[pl.BlockSpec((1, BRANCH_W, step), lambda b, j: (b, 0, j)),
                  pl.BlockSpec((1, t, KV_W), lambda b, j: (b, 0, 0)),
                  pl.BlockSpec((1, 2 * V_ROWS, t), lambda b, j: (b, 0, 0)),
                  pl.BlockSpec((1, step, BRANCH_W), lambda b, j: (b, j, 0))],
        out_specs=pl.BlockSpec((1, step, BRANCH_W), lambda b, j: (b, j, 0)),
        out_shape=jax.ShapeDtypeStruct((nb, t, BRANCH_W), BF16),
        scratch_shapes=[pltpu.VMEM((2, 2, tk, 2 * tq), F32),
                        pltpu.VMEM((2, 2, 1, 2 * tq), F32),
                        pltpu.VMEM((2, 2, 1, 2 * tq), F32),
                        pltpu.VMEM((2, 2, V_ROWS, 2 * tq), F32)],
        compiler_params=pltpu.CompilerParams(vmem_limit_bytes=VMEM_LIMIT),
        name="attention_a",
    )(qt, k, vt, z)


def _attn_d_kernel(sink_ref, qt_ref, k_ref, vt_ref, z_ref, o_ref, *, ctx_len, tq):
    tiles = qt_ref.shape[2] // tq
    t = k_ref.shape[1]
    span = tq + 2 * WINDOW
    first = lax.broadcasted_iota(jnp.int32, (1, 2 * tq), 1) < tq
    rel = (lax.broadcasted_iota(jnp.int32, (span, 2 * tq), 0)
           - lax.broadcasted_iota(jnp.int32, (span, 2 * tq), 1) % tq)

    def scores(u):
        tile = pl.program_id(1) * tiles + u
        q0 = tile * tq
        start = pl.multiple_of(jnp.clip(q0 - WINDOW, ctx_len, t - span), LANES)
        mask = (jnp.abs(rel + (start - q0)) <= WINDOW) & (tile > 0)
        sc = []
        for g in range(2):
            qt_g = _group_queries(qt_ref, g, slice(u * tq, (u + 1) * tq))
            sc.append((_dot(k_ref[0, :ctx_len, :], qt_g),
                       _dot(k_ref[0, pl.ds(start, span), :], qt_g)))
        return start, mask, sc

    def finish(u, start, mask, sc):
        for g in range(2):
            vrows = slice(g * V_ROWS, (g + 1) * V_ROWS)
            s_ctx = sc[g][0]
            s_loc = jnp.where(mask, sc[g][1], -jnp.inf)
            sink = jnp.where(first, sink_ref[2 * g], sink_ref[2 * g + 1]) * LOG2_E
            m = jnp.maximum(jnp.maximum(jnp.max(s_ctx, axis=0, keepdims=True),
                                        jnp.max(s_loc, axis=0, keepdims=True)), sink)
            acc = (_dot(vt_ref[0, vrows, :ctx_len], jnp.exp2(s_ctx - m).astype(BF16))
                   + _dot(vt_ref[0, vrows, pl.ds(start, span)], jnp.exp2(s_loc - m).astype(BF16)))
            l = acc[HEAD_DIM:HEAD_DIM + 1] + jnp.exp2(sink - m)
            _store_heads(o_ref, z_ref, g, acc[:HEAD_DIM] / l, slice(u * tq, (u + 1) * tq))

    pending = scores(0)
    for u in range(tiles):
        ready, pending = pending, (scores(u + 1) if u + 1 < tiles else None)
        finish(u, *ready)


def _attention_d(sink, qt, k, vt, z, ctx_len):
    nb, t, _ = k.shape
    tq = TOKEN_TILE
    step = ATTN_D_TILES * tq
    return pl.pallas_call(
        functools.partial(_attn_d_kernel, ctx_len=ctx_len, tq=tq),
        grid=(nb, t // step),
        in_specs=[pl.BlockSpec(memory_space=pltpu.SMEM),
                  pl.BlockSpec((1, BRANCH_W, step), lambda b, j: (b, 0, j)),
                  pl.BlockSpec((1, t, KV_W), lambda b, j: (b, 0, 0)),
                  pl.BlockSpec((1, 2 * V_ROWS, t), lambda b, j: (b, 0, 0)),
                  pl.BlockSpec((1, step, BRANCH_W), lambda b, j: (b, j, 0))],
        out_specs=pl.BlockSpec((1, step, BRANCH_W), lambda b, j: (b, j, 0)),
        out_shape=jax.ShapeDtypeStruct((nb, t, BRANCH_W), BF16),
        compiler_params=pltpu.CompilerParams(vmem_limit_bytes=VMEM_LIMIT),
        name="attention_d",
    )(sink, qt, k, vt, z)


def _dft_tables(seq_len, ctx_len):
    n2c = DFT_INNER
    n1c = seq_len // n2c
    scale = 1.0 / np.sqrt(float(seq_len) * GROUP_W)
    k1 = np.arange(n1c)[:, None]
    n1 = np.arange(n1c)[None, :]
    m1 = np.zeros((n2c, 2 * n1c, n1c), np.float64)
    for n2 in range(n2c):
        ang = -2.0 * np.pi * (k1 * n1 / n1c + n2 * k1 / seq_len)
        m1[n2, :n1c] = np.cos(ang) * scale
        m1[n2, n1c:] = np.sin(ang) * scale
    ang3 = 2.0 * np.pi * np.outer(np.arange(n2c), np.arange(n2c)) / n2c
    c3, s3 = np.cos(ang3), np.sin(ang3)
    m3 = np.block([[c3, s3], [-s3, c3]])
    cscale = 1.0 / np.sqrt(float(ctx_len) * GROUP_W)
    angc = 2.0 * np.pi * np.outer(np.arange(ctx_len), np.arange(ctx_len)) / ctx_len
    mc = np.concatenate([np.cos(angc), -np.sin(angc)], axis=0) * cscale
    angg = 2.0 * np.pi * np.outer(np.arange(GROUP_W), np.arange(GROUP_W)) / GROUP_W
    eye = np.eye(BRANCH_W // GROUP_W)
    chan = np.concatenate([np.kron(eye, np.cos(angg)), np.kron(eye, np.sin(angg))], axis=0)
    return _stacked(m1), _stacked(m3), _split(mc), _split(chan)


def _stacked(a):
    hi, lo = _split(a)
    return jnp.concatenate([hi, lo, hi], axis=-1)


def _dot3_stacked(m3, x):
    x_hi = x.astype(BF16)
    x_lo = (x - x_hi.astype(F32)).astype(BF16)
    return _dot(m3, jnp.concatenate([x_hi, x_hi, x_lo], axis=0))


def _split(a):
    a32 = jnp.asarray(a.astype(np.float32))
    hi = a32.astype(BF16)
    return hi, (a32 - hi.astype(F32)).astype(BF16)


def _dot3_left(m_hi, m_lo, x):
    x_hi = x.astype(BF16)
    x_lo = (x - x_hi.astype(F32)).astype(BF16)
    return _dot(m_hi, x_hi) + _dot(m_lo, x_hi) + _dot(m_hi, x_lo)


def _dot3_right(x, m_hi, m_lo):
    x_hi = x.astype(BF16)
    x_lo = (x - x_hi.astype(F32)).astype(BF16)
    return _dot(x_hi, m_hi) + _dot(x_hi, m_lo) + _dot(x_lo, m_hi)


def _fnet_dft_kernel(f_ref, m1_ref, fc_ref, m3_ref, mch_ref, mcl_ref,
                     pr_ref, pi_ref, prc_ref, pic_ref, br_ref, bi_ref, *, n_hi):
    i = pl.program_id(1)
    n2c = DFT_INNER
    sub = DFT_BLOCK
    halves = range(f_ref.shape[1])
    rows_b = br_ref.shape[2]
    n1c = rows_b // sub
    skip = f_ref.shape[3] - rows_b

    @pl.when(i < n_hi)
    def _():
        for q in range(sub):
            rows = pl.ds(q, n1c, stride=sub)
            x = jnp.concatenate([f_ref[0, h, 0, pl.ds(skip + q, n1c, stride=sub), :]
                                 for h in halves], axis=1)
            res = _dot3_stacked(m1_ref[i * sub + q], x)
            for h in halves:
                br_ref[h, i, rows, :] = res[:n1c, h * LANES:(h + 1) * LANES]
                bi_ref[h, i, rows, :] = res[n1c:, h * LANES:(h + 1) * LANES]

    @pl.when(i == n_hi)
    def _():
        ctx_len = fc_ref.shape[1]
        res = _dot3_left(mch_ref[...], mcl_ref[...], fc_ref[0])
        prc_ref[0] = res[:ctx_len]
        pic_ref[0] = res[ctx_len:]

    @pl.when(i >= n_hi)
    def _():
        for kb in range(pr_ref.shape[2]):
            base = pl.multiple_of(((i - n_hi) * pr_ref.shape[2] + kb) * (sub * sub), sub * sub)
            for q in range(sub):
                xin = jnp.concatenate(
                    [jnp.concatenate([ref[h, hi, pl.ds(base + q * sub, sub), :] for h in halves],
                                     axis=1)
                     for ref in (br_ref, bi_ref) for hi in range(n_hi)], axis=0)
                res = _dot3_stacked(m3_ref[...], xin)
                rows = pl.ds(q, n2c, stride=sub)
                for h in halves:
                    pr_ref[0, h, kb, rows, :] = res[:n2c, h * LANES:(h + 1) * LANES]
                    pi_ref[0, h, kb, rows, :] = res[n2c:, h * LANES:(h + 1) * LANES]


def _fnet_dft(fb, fb_ctx, m1, m3, mc):
    nb, nh, n_hi, rows_in, w = fb.shape
    ctx_len = fb_ctx.shape[1]
    n2c = DFT_INNER
    sub = DFT_BLOCK
    rows = rows_in - ctx_len // n_hi
    n1c = rows // sub
    n_kb = n1c // sub
    kb_step = DFT_OUT_BLOCKS
    first = lambda i: jnp.minimum(i, n_hi - 1)
    second = lambda i: jnp.maximum(i - n_hi, 0)
    const = lambda a: pl.BlockSpec(a.shape, lambda b, i: (0,) * a.ndim)
    cblk = pl.BlockSpec((1, ctx_len, nh * w), lambda b, i: (b, 0, 0))
    oblk = pl.BlockSpec((1, nh, kb_step, n2c * sub, w), lambda b, i: (b, 0, second(i), 0, 0))
    p4 = jax.ShapeDtypeStruct((nb, nh, n_kb, n2c * sub, w), F32)
    pc = jax.ShapeDtypeStruct((nb, ctx_len, nh * w), F32)
    return pl.pallas_call(
        functools.partial(_fnet_dft_kernel, n_hi=n_hi),
        grid=(nb, n_hi + n_kb // kb_step),
        in_specs=[pl.BlockSpec((1, nh, 1, rows_in, w), lambda b, i: (b, 0, first(i), 0, 0)),
                  const(m1),
                  cblk, const(m3), const(mc[0]), const(mc[1])],
        out_specs=[oblk, oblk, cblk, cblk],
        out_shape=[p4, p4, pc, pc],
        scratch_shapes=[pltpu.VMEM((nh, n_hi, rows, w), F32),
                        pltpu.VMEM((nh, n_hi, rows, w), F32)],
        compiler_params=pltpu.CompilerParams(vmem_limit_bytes=VMEM_LIMIT,
                                             dimension_semantics=("arbitrary", "arbitrary")),
        name="fnet_dft",
    )(fb, m1, fb_ctx, m3, *mc)


def _merge_kernel(xc_ref, xl_ref, mod_ref, nw_ref, *refs, n_batch, tile_off, tiles):
    per_tile = lambda k: refs[k * tiles:(k + 1) * tiles]
    ya_refs, yd_refs, zb_refs, yc_refs = (per_tile(k) for k in range(4))
    (pr_ref, pi_ref, prc_ref, pic_ref, chh_ref, chl_ref, wf_ref, wbr_ref, wm_ref, bm_ref,
     wo_ref, *out_refs) = refs[4 * tiles:]
    b, j = pl.program_id(0), pl.program_id(1)
    d = xl_ref.shape[-1]
    tm = xl_ref.shape[1] // tiles
    is_ctx = j + tile_off == 0
    row = jnp.where(is_ctx, n_batch, b)
    sub = DFT_BLOCK

    for u in range(tiles):
        tok = slice(u * tm, (u + 1) * tm)
        x = xl_ref[0, tok, :]
        if tile_off == 0:
            x = jnp.where(is_ctx, xc_ref[0], x)
        h, gate = _modulated_norm(x, mod_ref, nw_ref, row, d)

        def latent_rows(ref):
            per = ref.shape[3] // tiles
            return jnp.concatenate(
                [jnp.concatenate([ref[0, hf, kb, u * per + k2 * sub:u * per + (k2 + 1) * sub, :]
                                  for k2 in range(per // sub)
                                  for kb in range(ref.shape[2])], axis=0)
                 for hf in range(ref.shape[1])], axis=1)

        pc = jnp.concatenate([jnp.where(is_ctx, prc_ref[0], latent_rows(pr_ref)),
                              jnp.where(is_ctx, pic_ref[0], latent_rows(pi_ref))], axis=1)
        yb = (_dot(_dot3_right(pc, chh_ref[...], chl_ref[...]).astype(BF16), wf_ref[...])
              * zb_refs[u][0])
        branches = (ya_refs[u][0], yd_refs[u][0], yb.astype(BF16), yc_refs[u][0])
        hb = h.astype(BF16)
        acc = jnp.zeros((tm, d), F32)
        for r in range(N_BRANCH):
            g = jax.nn.sigmoid(_dot(hb, wm_ref[:, r * d:(r + 1) * d])
                               + bm_ref[:, r * d:(r + 1) * d])
            acc = acc + g * _dot(branches[r], wbr_ref[r])
        out = x + gate * _dot(acc.astype(BF16), wo_ref[...])
        out_refs[-1][0, tok, :] = out
        if tile_off == 0:
            @pl.when(is_ctx)
            def _():
                out_refs[0][0] = out


def _merge(layer, xc, xl, mod, nw, ya, yd, pr, pi, prc, pic, zb, yc, chan, wf_bd, w_br, w_merge,
           b_merge, w_out, skip_ctx):
    nb, seq, d = xl.shape
    tm = TOKEN_TILE
    off = 1 if skip_ctx else 0
    tiles = MERGE_TILES if skip_ctx else 1
    step = tiles * tm
    nt = ((seq + xc.shape[1]) // tm - off) // tiles
    lat_idx = (lambda j: j) if skip_ctx else (lambda j: jnp.maximum(j - 1, 0))
    tok_specs = lambda w: [pl.BlockSpec((1, tm, w), lambda b, j, u=u: (b, tiles * j + u + off, 0))
                           for u in range(tiles)]
    ctx_tile = pl.BlockSpec((1, tm, d), lambda b, j: (b, 0, 0))
    lat_tile = pl.BlockSpec((1, step, d), lambda b, j: (b, lat_idx(j), 0))
    xc_shape = jax.ShapeDtypeStruct(xc.shape, F32)
    xl_shape = jax.ShapeDtypeStruct(xl.shape, F32)
    lat_spec = lambda a: pl.BlockSpec((1, a.shape[1], a.shape[2], step // a.shape[2], a.shape[4]),
                                      lambda b, j: (b, 0, 0, lat_idx(j), 0))
    ctx_spec = lambda a: pl.BlockSpec((1,) + a.shape[1:], lambda b, j: (b, 0, 0))
    const = lambda a: pl.BlockSpec(a.shape, lambda b, j: (0,) * a.ndim)
    of_layer = lambda a: pl.BlockSpec((None,) + a.shape[1:],
                                      lambda b, j: (layer,) + (0,) * (a.ndim - 1))
    return pl.pallas_call(
        functools.partial(_merge_kernel, n_batch=nb, tile_off=off, tiles=tiles),
        grid=(nb, nt),
        in_specs=[ctx_tile, lat_tile, of_layer(mod), const(nw)]
                 + tok_specs(BRANCH_W) * 4
                 + [lat_spec(pr), lat_spec(pi), ctx_spec(prc), ctx_spec(pic),
                    const(chan[0]), const(chan[1]), const(wf_bd), of_layer(w_br),
                    of_layer(w_merge), const(b_merge), of_layer(w_out)],
        out_specs=[lat_tile] if skip_ctx else [ctx_tile, lat_tile],
        out_shape=[xl_shape] if skip_ctx else [xc_shape, xl_shape],
        compiler_params=pltpu.CompilerParams(vmem_limit_bytes=VMEM_LIMIT),
        name="merge",
    )(xc, xl, mod, nw, *([ya] * tiles), *([yd] * tiles), *([zb] * tiles), *([yc] * tiles),
      pr, pi, prc, pic, *chan, wf_bd, w_br, w_merge, b_merge, w_out)


def _rope_tables(seq_len, ctx_len):
    t = jnp.arange(seq_len, dtype=jnp.int32)
    r = (t // GRID_W).astype(F32)
    col = (t % GRID_W).astype(F32)
    nf = HEAD_DIM // 4
    inv = ROPE_BASE ** (-jnp.arange(nf, dtype=F32) / nf)
    ar = r[:, None] * inv[None, :]
    ac = col[:, None] * inv[None, :]
    cr, sr, cc, sc = jnp.cos(ar), jnp.sin(ar), jnp.cos(ac), jnp.sin(ac)
    cos_h = jnp.concatenate([cr, cr, cc, cc], axis=1)
    sin_h = jnp.concatenate([-sr, sr, -sc, sc], axis=1)
    cos_t = jnp.concatenate([jnp.ones((ctx_len, HEAD_DIM), F32), cos_h], axis=0)
    sin_t = jnp.concatenate([jnp.zeros((ctx_len, HEAD_DIM), F32), sin_h], axis=0)
    return jnp.tile(cos_t, (1, 2)), jnp.tile(sin_t, (1, 2))


def kernel(x, c, ctx, c_ctx, norm_w, w_ada, b_ada, w_in, qn_a, kn_a, qn_d, kn_d, sink_d,
           w_fnet, w_sp, b_sp, w_br, w_merge, b_merge, w_out):
    nb, seq, d = x.shape
    ctx_len = ctx.shape[1]
    depth = norm_w.shape[0]
    assert ctx_len == TOKEN_TILE and seq % (2 * ATTN_A_KEYS) == 0 and nb < 8
    assert (seq + ctx_len) % (ATTN_A_TILES * TOKEN_TILE) == 0
    assert (seq + ctx_len) % (ATTN_D_TILES * TOKEN_TILE) == 0
    assert (seq + ctx_len) % (PROJ_TILES * TOKEN_TILE) == 0

    cvecs = jnp.zeros((8, d), F32).at[:nb].set(c).at[nb].set(c_ctx)
    mods = _modulation(cvecs, w_ada, b_ada)

    cos_t, sin_t = _rope_tables(seq, ctx_len)
    m1, m3, mc, chan = _dft_tables(seq, ctx_len)
    pair = lambda w: jnp.tile(w, 2).reshape(1, LANES)
    eye_g = jnp.eye(BRANCH_W // GROUP_W, dtype=F32)

    w_in, w_sp, w_br, w_merge, w_out = (w.astype(BF16) for w in (w_in, w_sp, w_br, w_merge, w_out))
    xc, xl = ctx, x
    for l in range(depth):
        nw = norm_w[l].reshape(1, d)
        wf_bd = jnp.einsum('gh,gcd->gchd', eye_g, w_fnet[l]).reshape(BRANCH_W, BRANCH_W)
        b_sp_t = jnp.repeat(b_sp[l].T, GROUP_W, axis=1)
        (qat, ka, vat, za, qdt, kd, vdt, zd, fb, fbc, zb, yc) = _projection(
            l, xc, xl, mods, nw, w_in, cos_t, sin_t,
            pair(qn_a[l]), pair(kn_a[l]), pair(qn_d[l]), pair(kn_d[l]),
            w_sp, b_sp_t)
        ya = _attention_a(qat, ka, vat, za, ctx_len)
        yd = _attention_d(sink_d[l], qdt, kd, vdt, zd, ctx_len)
        pr, pi, prc, pic = _fnet_dft(fb, fbc, m1, m3, mc)
        *xc_new, xl = _merge(l, xc, xl, mods, nw, ya, yd, pr, pi, prc, pic, zb, yc, chan,
                             wf_bd.astype(BF16), w_br, w_merge, b_merge[l].reshape(1, -1), w_out,
                             skip_ctx=(l == depth - 1))
        xc = xc_new[0] if xc_new else None
    return xl
```

```python
import functools

import numpy as np
import jax
import jax.numpy as jnp
from jax import lax
from jax.experimental import pallas as pl
from jax.experimental.pallas import tpu as pltpu

F32 = jnp.float32
BF16 = jnp.bfloat16

GRID_W = 64
HEAD_DIM = 64
BRANCH_W = 256
KV_W = 128
N_BRANCH = 4
GROUP_W = 64
CHUNK = 128
WINDOW = 128
ROPE_BASE = 10000.0
EPS = 1e-6
LOG2_E = float(np.log2(np.e))
LANES = 128
TOKEN_TILE = 256
V_ROWS = 80
ATTN_A_KEYS = 512
ATTN_A_TILES = 3
ATTN_D_TILES = 11
PROJ_TILES = 3
MERGE_TILES = 4
DFT_BLOCK = 8
DFT_OUT_BLOCKS = 4
DFT_INNER = 64
VMEM_LIMIT = 56 * 1024 * 1024

_IN_SIZES = (BRANCH_W, KV_W, KV_W, BRANCH_W, BRANCH_W, KV_W, KV_W, BRANCH_W,
             BRANCH_W, BRANCH_W, BRANCH_W, BRANCH_W, BRANCH_W)
_IN_OFF = tuple(int(v) for v in np.cumsum((0,) + _IN_SIZES))
(_AQ, _AK, _AV, _AZ, _DQ, _DK, _DV, _DZ, _BF, _BZ, _CU, _CV, _CZ) = _IN_OFF[:-1]


def _silu(z):
    return z * jax.nn.sigmoid(z)


def _gelu(x):
    return 0.5 * x * (1.0 + lax.erf(x * np.float32(np.sqrt(0.5))))


def _dot(a, b):
    return jnp.dot(a, b, preferred_element_type=F32)


def _lane_lo(width=LANES):
    lane = lax.broadcasted_iota(jnp.int32, (1, width), 1)
    return (lane % LANES) < HEAD_DIM


def _mod_kernel(cv_ref, w_ref, b_ref, o_ref):
    s = _silu(cv_ref[...])
    o_ref[0] = _dot(s.astype(BF16), w_ref[0].astype(BF16)) + b_ref[0]


def _modulation(cvecs, w_ada, b_ada):
    depth, d, d3 = w_ada.shape
    nblk = d3 // d
    return pl.pallas_call(
        _mod_kernel,
        grid=(depth, nblk),
        in_specs=[pl.BlockSpec((8, d), lambda l, n: (0, 0)),
                  pl.BlockSpec((1, d, d), lambda l, n: (l, 0, n)),
                  pl.BlockSpec((1, 1, d), lambda l, n: (l, 0, n))],
        out_specs=pl.BlockSpec((1, 8, d), lambda l, n: (l, 0, n)),
        out_shape=jax.ShapeDtypeStruct((depth, 8, d3), F32),
        name="modulation",
    )(cvecs, w_ada, b_ada.reshape(depth, 1, d3))


def _modulated_norm(x, mod_ref, nw_ref, row, d):
    m = mod_ref[pl.ds(row, 1), :]
    sh, sc = m[:, :d], m[:, d:2 * d]
    ms = jnp.mean(x * x, axis=-1, keepdims=True)
    xn = x * lax.rsqrt(ms + EPS) * nw_ref[...]
    return xn * (1.0 + sc) + sh, m[:, 2 * d:]


def _head_norm_rope(xs, wn, cos, sin, scale):
    sq = xs * xs
    lo = _lane_lo()
    ms = jnp.where(lo, jnp.sum(jnp.where(lo, sq, 0.0), axis=1, keepdims=True),
                   jnp.sum(jnp.where(lo, 0.0, sq), axis=1, keepdims=True)) * (1.0 / HEAD_DIM)
    y = xs * lax.rsqrt(ms + EPS) * wn
    lane = lax.broadcasted_iota(jnp.int32, (1, LANES), 1)
    first = (lane % 32) < 16
    sw = jnp.where(first, pltpu.roll(y, LANES - 16, axis=1), pltpu.roll(y, 16, axis=1))
    y = y * cos + sw * sin
    return y * scale if scale != 1.0 else y


def _proj_kernel(xc_ref, *refs, n_batch, tiles):
    xl_refs, refs = refs[:tiles], refs[tiles:]
    (mod_ref, nw_ref, win_ref, cos_ref, sin_ref, qna_ref, kna_ref, qnd_ref, knd_ref, wsp_ref,
     bsp_ref, qat_ref, ka_ref, vat_ref, za_ref, qdt_ref, kd_ref, vdt_ref, zd_ref,
     fb_ref, fbc_ref, zb_ref, yc_ref) = refs
    b, j = pl.program_id(0), pl.program_id(1)
    d = xc_ref.shape[-1]
    tm = xc_ref.shape[1]
    q_scale = HEAD_DIM ** -0.5 * LOG2_E
    group = lax.broadcasted_iota(jnp.int32, (1, BRANCH_W), 1) // GROUP_W
    sub = DFT_BLOCK

    for u in range(tiles):
        tok = slice(u * tm, (u + 1) * tm)
        if u == 0:
            x = jnp.where(j == 0, xc_ref[0], xl_refs[0][0])
            row = jnp.where(j == 0, n_batch, b)
        else:
            x, row = xl_refs[u][0], b
        h, _ = _modulated_norm(x, mod_ref, nw_ref, row, d)
        hb = h.astype(BF16)
        pc = _dot(hb, win_ref[:, _CU:])
        p = _dot(hb, win_ref[:, :_CU])

        gu = _gelu(pc[:, :BRANCH_W])
        gv = _gelu(pc[:, BRANCH_W:2 * BRANCH_W]).astype(BF16)
        zc = _silu(pc[:, 2 * BRANCH_W:])
        for c in range(tm // CHUNK):
            rows = slice(c * CHUNK, (c + 1) * CHUNK)
            sp = bsp_ref[...]
            for g in range(BRANCH_W // GROUP_W):
                sp = sp + jnp.where(group == g, _dot(wsp_ref[g], gv[rows]), 0.0)
            yc_ref[0, u * tm + c * CHUNK:u * tm + (c + 1) * CHUNK, :] = (
                gu[rows] * sp * zc[rows]).astype(BF16)

        cos, sin = cos_ref[tok, :], sin_ref[tok, :]

        def attn_branch(q0, k0, v0, z0, qn_ref, kn_ref, qt_ref, k_ref, vt_ref, z_ref):
            q = [_head_norm_rope(p[:, q0 + s * LANES:q0 + (s + 1) * LANES], qn_ref[...],
                                 cos, sin, q_scale) for s in range(BRANCH_W // LANES)]
            qt_ref[0, :, tok] = jnp.concatenate(q, axis=1).T.astype(BF16)
            k_ref[0, tok, :] = _head_norm_rope(p[:, k0:k0 + KV_W], kn_ref[...], cos, sin,
                                               1.0).astype(BF16)
            vt = p[:, v0:v0 + KV_W].T
            ones = jnp.ones((V_ROWS - HEAD_DIM, tm), F32)
            vt_ref[0, :, tok] = jnp.concatenate([vt[:HEAD_DIM], ones, vt[HEAD_DIM:], ones],
                                                axis=0).astype(BF16)
            z_ref[0, tok, :] = _silu(p[:, z0:z0 + BRANCH_W])

        attn_branch(_AQ, _AK, _AV, _AZ, qna_ref, kna_ref, qat_ref, ka_ref, vat_ref, za_ref)
        attn_branch(_DQ, _DK, _DV, _DZ, qnd_ref, knd_ref, qdt_ref, kd_ref, vdt_ref, zd_ref)

        f = p[:, _BF:_BF + BRANCH_W]
        per_hi = tm // (DFT_INNER // sub)
        for hi in range(DFT_INNER // sub):
            rows = jnp.concatenate([f[n1 * DFT_INNER + hi * sub:n1 * DFT_INNER + (hi + 1) * sub]
                                    for n1 in range(tm // DFT_INNER)], axis=0)
            for half in range(BRANCH_W // LANES):
                fb_ref[0, half, hi, u * per_hi:(u + 1) * per_hi, :] = (
                    rows[:, half * LANES:(half + 1) * LANES])
        if u == 0:
            @pl.when(j == 0)
            def _():
                fbc_ref[0] = f

        zb_ref[0, tok, :] = _silu(p[:, _BZ:_BZ + BRANCH_W])


def _projection(layer, xc, xl, mod, nw, w_in, cos_t, sin_t, qna, kna, qnd, knd, w_sp, b_sp_t):
    nb, seq, d = xl.shape
    tm = TOKEN_TILE
    tiles = PROJ_TILES
    step = tiles * tm
    t = seq + xc.shape[1]
    tok = lambda w, dt: jax.ShapeDtypeStruct((nb, t, w), dt)
    tok_spec = lambda w: pl.BlockSpec((1, step, w), lambda b, j: (b, j, 0))
    tr = lambda w: jax.ShapeDtypeStruct((nb, w, t), BF16)
    tr_spec = lambda w: pl.BlockSpec((1, w, step), lambda b, j: (b, 0, j))
    const = lambda a: pl.BlockSpec(a.shape, lambda b, j: (0,) * a.ndim)
    of_layer = lambda a: pl.BlockSpec((None,) + a.shape[1:],
                                      lambda b, j: (layer,) + (0,) * (a.ndim - 1))
    lat_tile = lambda u: pl.BlockSpec(
        (1, tm, d), lambda b, j: (b, jnp.clip(tiles * j + u - 1, 0, seq // tm - 1), 0))
    attn_specs = [tr_spec(BRANCH_W), tok_spec(KV_W), tr_spec(2 * V_ROWS), tok_spec(BRANCH_W)]
    attn_shapes = [tr(BRANCH_W), tok(KV_W, BF16), tr(2 * V_ROWS), tok(BRANCH_W, F32)]
    n_hi = DFT_INNER // DFT_BLOCK
    return pl.pallas_call(
        functools.partial(_proj_kernel, n_batch=nb, tiles=tiles),
        grid=(nb, t // step),
        in_specs=[pl.BlockSpec((1, tm, d), lambda b, j: (b, 0, 0))]
                 + [lat_tile(u) for u in range(tiles)]
                 + [of_layer(mod), const(nw), of_layer(w_in),
                    pl.BlockSpec((step, LANES), lambda b, j: (j, 0)),
                    pl.BlockSpec((step, LANES), lambda b, j: (j, 0)),
                    const(qna), const(kna), const(qnd), const(knd), of_layer(w_sp),
                    const(b_sp_t)],
        out_specs=attn_specs * 2 + [
            pl.BlockSpec((1, BRANCH_W // LANES, n_hi, step // n_hi, LANES),
                         lambda b, j: (b, 0, 0, j, 0)),
            pl.BlockSpec((1, tm, BRANCH_W), lambda b, j: (b, 0, 0)),
            tok_spec(BRANCH_W), tok_spec(BRANCH_W)],
        out_shape=attn_shapes * 2 + [
            jax.ShapeDtypeStruct((nb, BRANCH_W // LANES, n_hi, t // n_hi, LANES), F32),
            jax.ShapeDtypeStruct((nb, tm, BRANCH_W), F32),
            tok(BRANCH_W, F32), tok(BRANCH_W, BF16)],
        compiler_params=pltpu.CompilerParams(vmem_limit_bytes=VMEM_LIMIT),
        name="projection",
    )(xc, *([xl] * tiles), mod, nw, w_in, cos_t, sin_t, qna, kna, qnd, knd, w_sp, b_sp_t)


def _group_queries(qt_ref, g, tokens=slice(None)):
    r0 = g * LANES
    qrow = jnp.concatenate([qt_ref[0, r0:r0 + HEAD_DIM, tokens],
                            qt_ref[0, r0 + HEAD_DIM:r0 + LANES, tokens]], axis=1)
    zeros = jnp.zeros_like(qrow)
    return jnp.concatenate([qrow, zeros] if g == 0 else [zeros, qrow], axis=0)


def _store_heads(o_ref, z_ref, g, o_t, tokens=slice(None)):
    tq = o_t.shape[1] // 2
    o = jnp.concatenate([o_t[:, :tq], o_t[:, tq:]], axis=0).T
    cols = slice(g * LANES, (g + 1) * LANES)
    o_ref[0, tokens, cols] = (o * z_ref[0, tokens, cols]).astype(BF16)


def _attn_a_kernel(qt_ref, k_ref, vt_ref, z_ref, o_ref, s_ref, smax_ref, m_ref, acc_ref,
                   *, tk, ctx_len, tq):
    tiles = qt_ref.shape[2] // tq
    t = k_ref.shape[1]
    n_lat = (t - ctx_len) // tk
    groups = range(2)

    def values(g, start, size):
        return vt_ref[0, g * V_ROWS:(g + 1) * V_ROWS, start:start + size]

    class Tile:
        def __init__(self, u):
            self.tok = slice(u * tq, (u + 1) * tq)
            self.par = u % 2

        def scores(self, g, start, size):
            return _dot(k_ref[0, start:start + size, :], self.qt_g[g])

        def produce(self, g, c):
            s = self.scores(g, ctx_len + c * tk, tk)
            smax_ref[c % 2, g] = jnp.max(s, axis=0, keepdims=True)
            s_ref[c % 2, g] = s

        def consume(self, g, c):
            m = m_ref[self.par, g]
            m_new = jnp.maximum(m, smax_ref[c % 2, g])
            pr = jnp.exp2(s_ref[c % 2, g] - m_new).astype(BF16)
            acc_ref[self.par, g] = (jnp.exp2(m - m_new) * acc_ref[self.par, g]
                                    + _dot(values(g, ctx_len + c * tk, tk), pr))
            m_ref[self.par, g] = m_new

        def head_scores(self):
            self.qt_g = [_group_queries(qt_ref, g, self.tok) for g in groups]
            self.s_ctx = [self.scores(g, 0, ctx_len) for g in groups]
            for g in groups:
                self.produce(g, 0)

        def head_finish(self):
            for g in groups:
                m = jnp.max(self.s_ctx[g], axis=0, keepdims=True)
                m_ref[self.par, g] = m
                acc_ref[self.par, g] = _dot(values(g, 0, ctx_len),
                                            jnp.exp2(self.s_ctx[g] - m).astype(BF16))

        def body(self):
            for c in range(n_lat - 1):
                for g in groups:
                    self.produce(g, c + 1)
                    self.consume(g, c)

        def last(self):
            for g in groups:
                self.consume(g, n_lat - 1)

        def store(self):
            for g in groups:
                acc = acc_ref[self.par, g]
                _store_heads(o_ref, z_ref, g, acc[:HEAD_DIM] / acc[HEAD_DIM:HEAD_DIM + 1],
                             self.tok)

    tile = Tile(0)
    tile.head_scores()
    tile.head_finish()

    @pl.when(pl.program_id(1) > 0)
    def _():
        tile.body()
        tile.last()

    for u in range(1, tiles):
        nxt = Tile(u)
        nxt.head_scores()
        if u > 1:
            tile.last()
        tile.store()
        nxt.head_finish()
        nxt.body()
        tile = nxt
    if tiles > 1:
        tile.last()
    tile.store()


def _attention_a(qt, k, vt, z, ctx_len):
    nb, t, _ = k.shape
    tq = TOKEN_TILE
    tk = ATTN_A_KEYS
    step = ATTN_A_TILES * tq
    return pl.pallas_call(
        functools.partial(_attn_a_kernel, tk=tk, ctx_len=ctx_len, tq=tq),
        grid=(nb, t // step),
        in_specs=[pl.BlockSpec((1, BRANCH_W, step), lambda b, j: (b, 0, j)),
                  pl.BlockSpec((1, t, KV_W), lambda b, j: (b, 0, 0)),
                  pl.BlockSpec((1, 2 * V_ROWS, t), lambda b, j: (b, 0, 0)),
                  pl.BlockSpec((1, step, BRANCH_W), lambda b, j: (b, j, 0))],
        out_specs=pl.BlockSpec((1, step, BRANCH_W), lambda b, j: (b, j, 0)),
        out_shape=jax.ShapeDtypeStruct((nb, t, BRANCH_W), BF16),
        scratch_shapes=[pltpu.VMEM((2, 2, tk, 2 * tq), F32),
                        pltpu.VMEM((2, 2, 1, 2 * tq), F32),
                        pltpu.VMEM((2, 2, 1, 2 * tq), F32),
                        pltpu.VMEM((2, 2, V_ROWS, 2 * tq), F32)],
        compiler_params=pltpu.CompilerParams(vmem_limit_bytes=VMEM_LIMIT),
        name="attention_a",
    )(qt, k, vt, z)


def _attn_d_kernel(sink_ref, qt_ref, k_ref, vt_ref, z_ref, o_ref, *, ctx_len, tq):
    tiles = qt_ref.shape[2] // tq
    t = k_ref.shape[1]
    span = tq + 2 * WINDOW
    first = lax.broadcasted_iota(jnp.int32, (1, 2 * tq), 1) < tq
    rel = (lax.broadcasted_iota(jnp.int32, (span, 2 * tq), 0)
           - lax.broadcasted_iota(jnp.int32, (span, 2 * tq), 1) % tq)

    def scores(u):
        tile = pl.program_id(1) * tiles + u
        q0 = tile * tq
        start = pl.multiple_of(jnp.clip(q0 - WINDOW, ctx_len, t - span), LANES)
        mask = (jnp.abs(rel + (start - q0)) <= WINDOW) & (tile > 0)
        sc = []
        for g in range(2):
            qt_g = _group_queries(qt_ref, g, slice(u * tq, (u + 1) * tq))
            sc.append((_dot(k_ref[0, :ctx_len, :], qt_g),
                       _dot(k_ref[0, pl.ds(start, span), :], qt_g)))
        return start, mask, sc

    def finish(u, start, mask, sc):
        for g in range(2):
            vrows = slice(g * V_ROWS, (g + 1) * V_ROWS)
            s_ctx = sc[g][0]
            s_loc = jnp.where(mask, sc[g][1], -jnp.inf)
            sink = jnp.where(first, sink_ref[2 * g], sink_ref[2 * g + 1]) * LOG2_E
            m = jnp.maximum(jnp.maximum(jnp.max(s_ctx, axis=0, keepdims=True),
                                        jnp.max(s_loc, axis=0, keepdims=True)), sink)
            acc = (_dot(vt_ref[0, vrows, :ctx_len], jnp.exp2(s_ctx - m).astype(BF16))
                   + _dot(vt_ref[0, vrows, pl.ds(start, span)], jnp.exp2(s_loc - m).astype(BF16)))
            l = acc[HEAD_DIM:HEAD_DIM + 1] + jnp.exp2(sink - m)
            _store_heads(o_ref, z_ref, g, acc[:HEAD_DIM] / l, slice(u * tq, (u + 1) * tq))

    pending = scores(0)
    for u in range(tiles):
        ready, pending = pending, (scores(u + 1) if u + 1 < tiles else None)
        finish(u, *ready)


def _attention_d(sink, qt, k, vt, z, ctx_len):
    nb, t, _ = k.shape
    tq = TOKEN_TILE
    step = ATTN_D_TILES * tq
    return pl.pallas_call(
        functools.partial(_attn_d_kernel, ctx_len=ctx_len, tq=tq),
        grid=(nb, t // step),
        in_specs=[pl.BlockSpec(memory_space=pltpu.SMEM),
                  pl.BlockSpec((1, BRANCH_W, step), lambda b, j: (b, 0, j)),
                  pl.BlockSpec((1, t, KV_W), lambda b, j: (b, 0, 0)),
                  pl.BlockSpec((1, 2 * V_ROWS, t), lambda b, j: (b, 0, 0)),
                  pl.BlockSpec((1, step, BRANCH_W), lambda b, j: (b, j, 0))],
        out_specs=pl.BlockSpec((1, step, BRANCH_W), lambda b, j: (b, j, 0)),
        out_shape=jax.ShapeDtypeStruct((nb, t, BRANCH_W), BF16),
        compiler_params=pltpu.CompilerParams(vmem_limit_bytes=VMEM_LIMIT),
        name="attention_d",
    )(sink, qt, k, vt, z)


def _dft_tables(seq_len, ctx_len):
    n2c = DFT_INNER
    n1c = seq_len // n2c
    scale = 1.0 / np.sqrt(float(seq_len) * GROUP_W)
    k1 = np.arange(n1c)[:, None]
    n1 = np.arange(n1c)[None, :]
    m1 = np.zeros((n2c, 2 * n1c, n1c), np.float64)
    for n2 in range(n2c):
        ang = -2.0 * np.pi * (k1 * n1 / n1c + n2 * k1 / seq_len)
        m1[n2, :n1c] = np.cos(ang) * scale
        m1[n2, n1c:] = np.sin(ang) * scale
    ang3 = 2.0 * np.pi * np.outer(np.arange(n2c), np.arange(n2c)) / n2c
    c3, s3 = np.cos(ang3), np.sin(ang3)
    m3 = np.block([[c3, s3], [-s3, c3]])
    cscale = 1.0 / np.sqrt(float(ctx_len) * GROUP_W)
    angc = 2.0 * np.pi * np.outer(np.arange(ctx_len), np.arange(ctx_len)) / ctx_len
    mc = np.concatenate([np.cos(angc), -np.sin(angc)], axis=0) * cscale
    angg = 2.0 * np.pi * np.outer(np.arange(GROUP_W), np.arange(GROUP_W)) / GROUP_W
    eye = np.eye(BRANCH_W // GROUP_W)
    chan = np.concatenate([np.kron(eye, np.cos(angg)), np.kron(eye, np.sin(angg))], axis=0)
    return _stacked(m1), _stacked(m3), _split(mc), _split(chan)


def _stacked(a):
    hi, lo = _split(a)
    return jnp.concatenate([hi, lo, hi], axis=-1)


def _dot3_stacked(m3, x):
    x_hi = x.astype(BF16)
    x_lo = (x - x_hi.astype(F32)).astype(BF16)
    return _dot(m3, jnp.concatenate([x_hi, x_hi, x_lo], axis=0))


def _split(a):
    a32 = jnp.asarray(a.astype(np.float32))
    hi = a32.astype(BF16)
    return hi, (a32 - hi.astype(F32)).astype(BF16)


def _dot3_left(m_hi, m_lo, x):
    x_hi = x.astype(BF16)
    x_lo = (x - x_hi.astype(F32)).astype(BF16)
    return _dot(m_hi, x_hi) + _dot(m_lo, x_hi) + _dot(m_hi, x_lo)


def _dot3_right(x, m_hi, m_lo):
    x_hi = x.astype(BF16)
    x_lo = (x - x_hi.astype(F32)).astype(BF16)
    return _dot(x_hi, m_hi) + _dot(x_hi, m_lo) + _dot(x_lo, m_hi)


def _fnet_dft_kernel(f_ref, m1_ref, fc_ref, m3_ref, mch_ref, mcl_ref,
                     pr_ref, pi_ref, prc_ref, pic_ref, br_ref, bi_ref, *, n_hi):
    i = pl.program_id(1)
    n2c = DFT_INNER
    sub = DFT_BLOCK
    halves = range(f_ref.shape[1])
    rows_b = br_ref.shape[2]
    n1c = rows_b // sub
    skip = f_ref.shape[3] - rows_b

    @pl.when(i < n_hi)
    def _():
        for q in range(sub):
            rows = pl.ds(q, n1c, stride=sub)
            x = jnp.concatenate([f_ref[0, h, 0, pl.ds(skip + q, n1c, stride=sub), :]
                                 for h in halves], axis=1)
            res = _dot3_stacked(m1_ref[i * sub + q], x)
            for h in halves:
                br_ref[h, i, rows, :] = res[:n1c, h * LANES:(h + 1) * LANES]
                bi_ref[h, i, rows, :] = res[n1c:, h * LANES:(h + 1) * LANES]

    @pl.when(i == n_hi)
    def _():
        ctx_len = fc_ref.shape[1]
        res = _dot3_left(mch_ref[...], mcl_ref[...], fc_ref[0])
        prc_ref[0] = res[:ctx_len]
        pic_ref[0] = res[ctx_len:]

    @pl.when(i >= n_hi)
    def _():
        for kb in range(pr_ref.shape[2]):
            base = pl.multiple_of(((i - n_hi) * pr_ref.shape[2] + kb) * (sub * sub), sub * sub)
            for q in range(sub):
                xin = jnp.concatenate(
                    [jnp.concatenate([ref[h, hi, pl.ds(base + q * sub, sub), :] for h in halves],
                                     axis=1)
                     for ref in (br_ref, bi_ref) for hi in range(n_hi)], axis=0)
                res = _dot3_stacked(m3_ref[...], xin)
                rows = pl.ds(q, n2c, stride=sub)
                for h in halves:
                    pr_ref[0, h, kb, rows, :] = res[:n2c, h * LANES:(h + 1) * LANES]
                    pi_ref[0, h, kb, rows, :] = res[n2c:, h * LANES:(h + 1) * LANES]


def _fnet_dft(fb, fb_ctx, m1, m3, mc):
    nb, nh, n_hi, rows_in, w = fb.shape
    ctx_len = fb_ctx.shape[1]
    n2c = DFT_INNER
    sub = DFT_BLOCK
    rows = rows_in - ctx_len // n_hi
    n1c = rows // sub
    n_kb = n1c // sub
    kb_step = DFT_OUT_BLOCKS
    first = lambda i: jnp.minimum(i, n_hi - 1)
    second = lambda i: jnp.maximum(i - n_hi, 0)
    const = lambda a: pl.BlockSpec(a.shape, lambda b, i: (0,) * a.ndim)
    cblk = pl.BlockSpec((1, ctx_len, nh * w), lambda b, i: (b, 0, 0))
    oblk = pl.BlockSpec((1, nh, kb_step, n2c * sub, w), lambda b, i: (b, 0, second(i), 0, 0))
    p4 = jax.ShapeDtypeStruct((nb, nh, n_kb, n2c * sub, w), F32)
    pc = jax.ShapeDtypeStruct((nb, ctx_len, nh * w), F32)
    return pl.pallas_call(
        functools.partial(_fnet_dft_kernel, n_hi=n_hi),
        grid=(nb, n_hi + n_kb // kb_step),
        in_specs=[pl.BlockSpec((1, nh, 1, rows_in, w), lambda b, i: (b, 0, first(i), 0, 0)),
                  const(m1),
                  cblk, const(m3), const(mc[0]), const(mc[1])],
        out_specs=[oblk, oblk, cblk, cblk],
        out_shape=[p4, p4, pc, pc],
        scratch_shapes=[pltpu.VMEM((nh, n_hi, rows, w), F32),
                        pltpu.VMEM((nh, n_hi, rows, w), F32)],
        compiler_params=pltpu.CompilerParams(vmem_limit_bytes=VMEM_LIMIT,
                                             dimension_semantics=("arbitrary", "arbitrary")),
        name="fnet_dft",
    )(fb, m1, fb_ctx, m3, *mc)


def _merge_kernel(xc_ref, xl_ref, mod_ref, nw_ref, *refs, n_batch, ctx_only, tiles):
    per_tile = lambda k: refs[k * tiles:(k + 1) * tiles]
    ya_refs, yd_refs, zb_refs, yc_refs = (per_tile(k) for k in range(4))
    (pr_ref, pi_ref, prc_ref, pic_ref, chh_ref, chl_ref, wf_ref, wbr_ref, wm_ref, bm_ref,
     wo_ref, o_ref) = refs[4 * tiles:]
    d = xl_ref.shape[-1]
    tm = o_ref.shape[1] // tiles
    row = n_batch if ctx_only else pl.program_id(0)
    sub = DFT_BLOCK

    for u in range(tiles):
        tok = slice(u * tm, (u + 1) * tm)
        x = xc_ref[0] if ctx_only else xl_ref[0, tok, :]
        h, gate = _modulated_norm(x, mod_ref, nw_ref, row, d)

        def latent_rows(ref):
            per = ref.shape[3] // tiles
            return jnp.concatenate(
                [jnp.concatenate([ref[0, hf, kb, u * per + k2 * sub:u * per + (k2 + 1) * sub, :]
                                  for k2 in range(per // sub)
                                  for kb in range(ref.shape[2])], axis=0)
                 for hf in range(ref.shape[1])], axis=1)

        if ctx_only:
            pc = jnp.concatenate([prc_ref[0], pic_ref[0]], axis=1)
        else:
            pc = jnp.concatenate([latent_rows(pr_ref), latent_rows(pi_ref)], axis=1)
        yb = (_dot(_dot3_right(pc, chh_ref[...], chl_ref[...]).astype(BF16), wf_ref[...])
              * zb_refs[u][0])
        branches = (ya_refs[u][0], yd_refs[u][0], yb.astype(BF16), yc_refs[u][0])
        hb = h.astype(BF16)
        acc = jnp.zeros((tm, d), F32)
        for r in range(N_BRANCH):
            g = jax.nn.sigmoid(_dot(hb, wm_ref[:, r * d:(r + 1) * d])
                               + bm_ref[:, r * d:(r + 1) * d])
            acc = acc + g * _dot(branches[r], wbr_ref[r])
        o_ref[0, tok, :] = x + gate * _dot(acc.astype(BF16), wo_ref[...])


def _merge(layer, xc, xl, mod, nw, ya, yd, pr, pi, prc, pic, zb, yc, chan, wf_bd, w_br, w_merge,
           b_merge, w_out, ctx_only):
    nb, seq, d = xl.shape
    tm = TOKEN_TILE
    tiles = 1 if ctx_only else MERGE_TILES
    off = 0 if ctx_only else xc.shape[1] // tm
    step = tiles * tm
    nt = 1 if ctx_only else seq // step
    tok_specs = lambda w: [pl.BlockSpec((1, tm, w), lambda b, j, u=u: (b, tiles * j + u + off, 0))
                           for u in range(tiles)]
    ctx_tile = pl.BlockSpec((1, tm, d), lambda b, j: (b, 0, 0))
    lat_tile = pl.BlockSpec((1, step, d), lambda b, j: (b, j, 0))
    lat_spec = lambda a: pl.BlockSpec((1, a.shape[1], a.shape[2], step // a.shape[2], a.shape[4]),
                                      lambda b, j: (b, 0, 0, j, 0))
    ctx_spec = lambda a: pl.BlockSpec((1,) + a.shape[1:], lambda b, j: (b, 0, 0))
    const = lambda a: pl.BlockSpec(a.shape, lambda b, j: (0,) * a.ndim)
    of_layer = lambda a: pl.BlockSpec((None,) + a.shape[1:],
                                      lambda b, j: (layer,) + (0,) * (a.ndim - 1))
    return pl.pallas_call(
        functools.partial(_merge_kernel, n_batch=nb, ctx_only=ctx_only, tiles=tiles),
        grid=(nb, nt),
        in_specs=[ctx_tile, lat_tile, of_layer(mod), const(nw)]
                 + tok_specs(BRANCH_W) * 4
                 + [lat_spec(pr), lat_spec(pi), ctx_spec(prc), ctx_spec(pic),
                    const(chan[0]), const(chan[1]), const(wf_bd), of_layer(w_br),
                    of_layer(w_merge), const(b_merge), of_layer(w_out)],
        out_specs=ctx_tile if ctx_only else lat_tile,
        out_shape=jax.ShapeDtypeStruct(xc.shape if ctx_only else xl.shape, F32),
        compiler_params=pltpu.CompilerParams(vmem_limit_bytes=VMEM_LIMIT),
        name="merge_context" if ctx_only else "merge",
    )(xc, xl, mod, nw, *([ya] * tiles), *([yd] * tiles), *([zb] * tiles), *([yc] * tiles),
      pr, pi, prc, pic, *chan, wf_bd, w_br, w_merge, b_merge, w_out)


def _rope_tables(seq_len, ctx_len):
    t = jnp.arange(seq_len, dtype=jnp.int32)
    r = (t // GRID_W).astype(F32)
    col = (t % GRID_W).astype(F32)
    nf = HEAD_DIM // 4
    inv = ROPE_BASE ** (-jnp.arange(nf, dtype=F32) / nf)
    ar = r[:, None] * inv[None, :]
    ac = col[:, None] * inv[None, :]
    cr, sr, cc, sc = jnp.cos(ar), jnp.sin(ar), jnp.cos(ac), jnp.sin(ac)
    cos_h = jnp.concatenate([cr, cr, cc, cc], axis=1)
    sin_h = jnp.concatenate([-sr, sr, -sc, sc], axis=1)
    cos_t = jnp.concatenate([jnp.ones((ctx_len, HEAD_DIM), F32), cos_h], axis=0)
    sin_t = jnp.concatenate([jnp.zeros((ctx_len, HEAD_DIM), F32), sin_h], axis=0)
    return jnp.tile(cos_t, (1, 2)), jnp.tile(sin_t, (1, 2))


def kernel(x, c, ctx, c_ctx, norm_w, w_ada, b_ada, w_in, qn_a, kn_a, qn_d, kn_d, sink_d,
           w_fnet, w_sp, b_sp, w_br, w_merge, b_merge, w_out):
    nb, seq, d = x.shape
    ctx_len = ctx.shape[1]
    depth = norm_w.shape[0]
    assert ctx_len == TOKEN_TILE and seq % (2 * ATTN_A_KEYS) == 0 and nb < 8
    assert (seq + ctx_len) % (ATTN_A_TILES * TOKEN_TILE) == 0
    assert (seq + ctx_len) % (ATTN_D_TILES * TOKEN_TILE) == 0
    assert (seq + ctx_len) % (PROJ_TILES * TOKEN_TILE) == 0

    cvecs = jnp.zeros((8, d), F32).at[:nb].set(c).at[nb].set(c_ctx)
    mods = _modulation(cvecs, w_ada, b_ada)

    cos_t, sin_t = _rope_tables(seq, ctx_len)
    m1, m3, mc, chan = _dft_tables(seq, ctx_len)
    pair = lambda w: jnp.tile(w, 2).reshape(1, LANES)
    eye_g = jnp.eye(BRANCH_W // GROUP_W, dtype=F32)

    w_in, w_sp, w_br, w_merge, w_out = (w.astype(BF16) for w in (w_in, w_sp, w_br, w_merge, w_out))
    xc, xl = ctx, x
    for l in range(depth):
        nw = norm_w[l].reshape(1, d)
        wf_bd = jnp.einsum('gh,gcd->gchd', eye_g, w_fnet[l]).reshape(BRANCH_W, BRANCH_W)
        b_sp_t = jnp.repeat(b_sp[l].T, GROUP_W, axis=1)
        (qat, ka, vat, za, qdt, kd, vdt, zd, fb, fbc, zb, yc) = _projection(
            l, xc, xl, mods, nw, w_in, cos_t, sin_t,
            pair(qn_a[l]), pair(kn_a[l]), pair(qn_d[l]), pair(kn_d[l]),
            w_sp, b_sp_t)
        ya = _attention_a(qat, ka, vat, za, ctx_len)
        yd = _attention_d(sink_d[l], qdt, kd, vdt, zd, ctx_len)
        pr, pi, prc, pic = _fnet_dft(fb, fbc, m1, m3, mc)
        merged = functools.partial(
            _merge, l, xc, xl, mods, nw, ya, yd, pr, pi, prc, pic, zb, yc, chan,
            wf_bd.astype(BF16), w_br, w_merge, b_merge[l].reshape(1, -1), w_out)
        if l < depth - 1:
            xc = merged(ctx_only=True)
        xl = merged(ctx_only=False)
    return xl
```

```python
import functools

import numpy as np
import jax
import jax.numpy as jnp
from jax import lax
from jax.experimental import pallas as pl
from jax.experimental.pallas import tpu as pltpu

F32 = jnp.float32
BF16 = jnp.bfloat16

GRID_W = 64
HEAD_DIM = 64
BRANCH_W = 256
KV_W = 128
N_BRANCH = 4
GROUP_W = 64
CHUNK = 128
WINDOW = 128
ROPE_BASE = 10000.0
EPS = 1e-6
LOG2_E = float(np.log2(np.e))
LANES = 128
TOKEN_TILE = 256
V_ROWS = 80
ATTN_A_KEYS = 512
ATTN_A_TILES = 3
ATTN_D_TILES = 11
PROJ_TILES = 3
MERGE_TILES = 4
DFT_BLOCK = 8
DFT_OUT_BLOCKS = 4
DFT_INNER = 64
VMEM_LIMIT = 48 * 1024 * 1024

_IN_SIZES = (BRANCH_W, KV_W, KV_W, BRANCH_W, BRANCH_W, KV_W, KV_W, BRANCH_W,
             BRANCH_W, BRANCH_W, BRANCH_W, BRANCH_W, BRANCH_W)
_IN_OFF = tuple(int(v) for v in np.cumsum((0,) + _IN_SIZES))
(_AQ, _AK, _AV, _AZ, _DQ, _DK, _DV, _DZ, _BF, _BZ, _CU, _CV, _CZ) = _IN_OFF[:-1]


def _silu(z):
    return z * jax.nn.sigmoid(z)


def _gelu(x):
    return 0.5 * x * (1.0 + lax.erf(x * np.float32(np.sqrt(0.5))))


def _dot(a, b):
    return jnp.dot(a, b, preferred_element_type=F32)


def _lane_lo(width=LANES):
    lane = lax.broadcasted_iota(jnp.int32, (1, width), 1)
    return (lane % LANES) < HEAD_DIM


def _mod_kernel(cv_ref, w_ref, b_ref, o_ref):
    s = _silu(cv_ref[...])
    o_ref[0] = _dot(s.astype(BF16), w_ref[0].astype(BF16)) + b_ref[0]


def _modulation(cvecs, w_ada, b_ada):
    depth, d, d3 = w_ada.shape
    nblk = d3 // d
    return pl.pallas_call(
        _mod_kernel,
        grid=(depth, nblk),
        in_specs=[pl.BlockSpec((8, d), lambda l, n: (0, 0)),
                  pl.BlockSpec((1, d, d), lambda l, n: (l, 0, n)),
                  pl.BlockSpec((1, 1, d), lambda l, n: (l, 0, n))],
        out_specs=pl.BlockSpec((1, 8, d), lambda l, n: (l, 0, n)),
        out_shape=jax.ShapeDtypeStruct((depth, 8, d3), F32),
        name="modulation",
    )(cvecs, w_ada, b_ada.reshape(depth, 1, d3))


def _modulated_norm(x, mod_ref, nw_ref, row, d):
    m = mod_ref[pl.ds(row, 1), :]
    sh, sc = m[:, :d], m[:, d:2 * d]
    ms = jnp.mean(x * x, axis=-1, keepdims=True)
    xn = x * lax.rsqrt(ms + EPS) * nw_ref[...]
    return xn * (1.0 + sc) + sh, m[:, 2 * d:]


def _head_norm_rope(xs, wn, cos, sin, scale):
    sq = xs * xs
    lo = _lane_lo()
    ms = jnp.where(lo, jnp.sum(jnp.where(lo, sq, 0.0), axis=1, keepdims=True),
                   jnp.sum(jnp.where(lo, 0.0, sq), axis=1, keepdims=True)) * (1.0 / HEAD_DIM)
    y = xs * lax.rsqrt(ms + EPS) * wn
    lane = lax.broadcasted_iota(jnp.int32, (1, LANES), 1)
    first = (lane % 32) < 16
    sw = jnp.where(first, pltpu.roll(y, LANES - 16, axis=1), pltpu.roll(y, 16, axis=1))
    y = y * cos + sw * sin
    return y * scale if scale != 1.0 else y


def _proj_kernel(xc_ref, *refs, n_batch, tiles):
    xl_refs, refs = refs[:tiles], refs[tiles:]
    (mod_ref, nw_ref, win_ref, cos_ref, sin_ref, qna_ref, kna_ref, qnd_ref, knd_ref, wsp_ref,
     bsp_ref, qat_ref, ka_ref, vat_ref, za_ref, qdt_ref, kd_ref, vdt_ref, zd_ref,
     fb_ref, fbc_ref, zb_ref, yc_ref) = refs
    b, j = pl.program_id(0), pl.program_id(1)
    d = xc_ref.shape[-1]
    tm = xc_ref.shape[1]
    q_scale = HEAD_DIM ** -0.5 * LOG2_E
    group = lax.broadcasted_iota(jnp.int32, (1, BRANCH_W), 1) // GROUP_W
    sub = DFT_BLOCK

    for u in range(tiles):
        tok = slice(u * tm, (u + 1) * tm)
        if u == 0:
            x = jnp.where(j == 0, xc_ref[0], xl_refs[0][0])
            row = jnp.where(j == 0, n_batch, b)
        else:
            x, row = xl_refs[u][0], b
        h, _ = _modulated_norm(x, mod_ref, nw_ref, row, d)
        hb = h.astype(BF16)
        pc = _dot(hb, win_ref[:, _CU:])
        p = _dot(hb, win_ref[:, :_CU])

        gu = _gelu(pc[:, :BRANCH_W])
        gv = _gelu(pc[:, BRANCH_W:2 * BRANCH_W]).astype(BF16)
        zc = _silu(pc[:, 2 * BRANCH_W:])
        for c in range(tm // CHUNK):
            rows = slice(c * CHUNK, (c + 1) * CHUNK)
            sp = bsp_ref[...]
            for g in range(BRANCH_W // GROUP_W):
                sp = sp + jnp.where(group == g, _dot(wsp_ref[g], gv[rows]), 0.0)
            yc_ref[0, u * tm + c * CHUNK:u * tm + (c + 1) * CHUNK, :] = (
                gu[rows] * sp * zc[rows]).astype(BF16)

        cos, sin = cos_ref[tok, :], sin_ref[tok, :]

        def attn_branch(q0, k0, v0, z0, qn_ref, kn_ref, qt_ref, k_ref, vt_ref, z_ref):
            q = [_head_norm_rope(p[:, q0 + s * LANES:q0 + (s + 1) * LANES], qn_ref[...],
                                 cos, sin, q_scale) for s in range(BRANCH_W // LANES)]
            qt_ref[0, :, tok] = jnp.concatenate(q, axis=1).T.astype(BF16)
            k_ref[0, tok, :] = _head_norm_rope(p[:, k0:k0 + KV_W], kn_ref[...], cos, sin,
                                               1.0).astype(BF16)
            vt = p[:, v0:v0 + KV_W].T
            ones = jnp.ones((V_ROWS - HEAD_DIM, tm), F32)
            vt_ref[0, :, tok] = jnp.concatenate([vt[:HEAD_DIM], ones, vt[HEAD_DIM:], ones],
                                                axis=0).astype(BF16)
            z_ref[0, tok, :] = _silu(p[:, z0:z0 + BRANCH_W])

        attn_branch(_AQ, _AK, _AV, _AZ, qna_ref, kna_ref, qat_ref, ka_ref, vat_ref, za_ref)
        attn_branch(_DQ, _DK, _DV, _DZ, qnd_ref, knd_ref, qdt_ref, kd_ref, vdt_ref, zd_ref)

        f = p[:, _BF:_BF + BRANCH_W]
        per_hi = tm // (DFT_INNER // sub)
        for hi in range(DFT_INNER // sub):
            rows = jnp.concatenate([f[n1 * DFT_INNER + hi * sub:n1 * DFT_INNER + (hi + 1) * sub]
                                    for n1 in range(tm // DFT_INNER)], axis=0)
            for half in range(BRANCH_W // LANES):
                fb_ref[0, half, hi, u * per_hi:(u + 1) * per_hi, :] = (
                    rows[:, half * LANES:(half + 1) * LANES])
        if u == 0:
            @pl.when(j == 0)
            def _():
                fbc_ref[0] = f

        zb_ref[0, tok, :] = _silu(p[:, _BZ:_BZ + BRANCH_W])


def _projection(layer, xc, xl, mod, nw, w_in, cos_t, sin_t, qna, kna, qnd, knd, w_sp, b_sp_t):
    nb, seq, d = xl.shape
    tm = TOKEN_TILE
    tiles = PROJ_TILES
    step = tiles * tm
    t = seq + xc.shape[1]
    tok = lambda w, dt: jax.ShapeDtypeStruct((nb, t, w), dt)
    tok_spec = lambda w: pl.BlockSpec((1, step, w), lambda b, j: (b, j, 0))
    tr = lambda w: jax.ShapeDtypeStruct((nb, w, t), BF16)
    tr_spec = lambda w: pl.BlockSpec((1, w, step), lambda b, j: (b, 0, j))
    const = lambda a: pl.BlockSpec(a.shape, lambda b, j: (0,) * a.ndim)
    of_layer = lambda a: pl.BlockSpec((None,) + a.shape[1:],
                                      lambda b, j: (layer,) + (0,) * (a.ndim - 1))
    lat_tile = lambda u: pl.BlockSpec(
        (1, tm, d), lambda b, j: (b, jnp.clip(tiles * j + u - 1, 0, seq // tm - 1), 0))
    attn_specs = [tr_spec(BRANCH_W), tok_spec(KV_W), tr_spec(2 * V_ROWS), tok_spec(BRANCH_W)]
    attn_shapes = [tr(BRANCH_W), tok(KV_W, BF16), tr(2 * V_ROWS), tok(BRANCH_W, F32)]
    n_hi = DFT_INNER // DFT_BLOCK
    return pl.pallas_call(
        functools.partial(_proj_kernel, n_batch=nb, tiles=tiles),
        grid=(nb, t // step),
        in_specs=[pl.BlockSpec((1, tm, d), lambda b, j: (b, 0, 0))]
                 + [lat_tile(u) for u in range(tiles)]
                 + [of_layer(mod), const(nw), of_layer(w_in),
                    pl.BlockSpec((step, LANES), lambda b, j: (j, 0)),
                    pl.BlockSpec((step, LANES), lambda b, j: (j, 0)),
                    const(qna), const(kna), const(qnd), const(knd), of_layer(w_sp),
                    const(b_sp_t)],
        out_specs=attn_specs * 2 + [
            pl.BlockSpec((1, BRANCH_W // LANES, n_hi, step // n_hi, LANES),
                         lambda b, j: (b, 0, 0, j, 0)),
            pl.BlockSpec((1, tm, BRANCH_W), lambda b, j: (b, 0, 0)),
            tok_spec(BRANCH_W), tok_spec(BRANCH_W)],
        out_shape=attn_shapes * 2 + [
            jax.ShapeDtypeStruct((nb, BRANCH_W // LANES, n_hi, t // n_hi, LANES), F32),
            jax.ShapeDtypeStruct((nb, tm, BRANCH_W), F32),
            tok(BRANCH_W, F32), tok(BRANCH_W, BF16)],
        compiler_params=pltpu.CompilerParams(vmem_limit_bytes=VMEM_LIMIT),
        name="projection",
    )(xc, *([xl] * tiles), mod, nw, w_in, cos_t, sin_t, qna, kna, qnd, knd, w_sp, b_sp_t)


def _group_queries(qt_ref, g, tokens=slice(None)):
    r0 = g * LANES
    qrow = jnp.concatenate([qt_ref[0, r0:r0 + HEAD_DIM, tokens],
                            qt_ref[0, r0 + HEAD_DIM:r0 + LANES, tokens]], axis=1)
    zeros = jnp.zeros_like(qrow)
    return jnp.concatenate([qrow, zeros] if g == 0 else [zeros, qrow], axis=0)


def _store_heads(o_ref, z_ref, g, o_t, tokens=slice(None)):
    tq = o_t.shape[1] // 2
    o = jnp.concatenate([o_t[:, :tq], o_t[:, tq:]], axis=0).T
    cols = slice(g * LANES, (g + 1) * LANES)
    o_ref[0, tokens, cols] = (o * z_ref[0, tokens, cols]).astype(BF16)


def _attn_a_kernel(qt_ref, k_ref, vt_ref, z_ref, o_ref, s_ref, smax_ref, m_ref, acc_ref,
                   *, tk, ctx_len, tq):
    tiles = qt_ref.shape[2] // tq
    t = k_ref.shape[1]
    n_lat = (t - ctx_len) // tk
    groups = range(2)

    def values(g, start, size):
        return vt_ref[0, g * V_ROWS:(g + 1) * V_ROWS, start:start + size]

    class Tile:
        def __init__(self, u):
            self.tok = slice(u * tq, (u + 1) * tq)
            self.par = u % 2

        def scores(self, g, start, size):
            return _dot(k_ref[0, start:start + size, :], self.qt_g[g])

        def produce(self, g, c):
            s = self.scores(g, ctx_len + c * tk, tk)
            smax_ref[c % 2, g] = jnp.max(s, axis=0, keepdims=True)
            s_ref[c % 2, g] = s

        def consume(self, g, c):
            m = m_ref[self.par, g]
            m_new = jnp.maximum(m, smax_ref[c % 2, g])
            pr = jnp.exp2(s_ref[c % 2, g] - m_new).astype(BF16)
            acc_ref[self.par, g] = (jnp.exp2(m - m_new) * acc_ref[self.par, g]
                                    + _dot(values(g, ctx_len + c * tk, tk), pr))
            m_ref[self.par, g] = m_new

        def head_scores(self):
            self.qt_g = [_group_queries(qt_ref, g, self.tok) for g in groups]
            self.s_ctx = [self.scores(g, 0, ctx_len) for g in groups]
            for g in groups:
                self.produce(g, 0)

        def head_finish(self):
            for g in groups:
                m = jnp.max(self.s_ctx[g], axis=0, keepdims=True)
                m_ref[self.par, g] = m
                acc_ref[self.par, g] = _dot(values(g, 0, ctx_len),
                                            jnp.exp2(self.s_ctx[g] - m).astype(BF16))

        def body(self):
            for c in range(n_lat - 1):
                for g in groups:
                    self.produce(g, c + 1)
                    self.consume(g, c)

        def last(self):
            for g in groups:
                self.consume(g, n_lat - 1)

        def store(self):
            for g in groups:
                acc = acc_ref[self.par, g]
                _store_heads(o_ref, z_ref, g, acc[:HEAD_DIM] / acc[HEAD_DIM:HEAD_DIM + 1],
                             self.tok)

    tile = Tile(0)
    tile.head_scores()
    tile.head_finish()

    @pl.when(pl.program_id(1) > 0)
    def _():
        tile.body()
        tile.last()

    for u in range(1, tiles):
        nxt = Tile(u)
        nxt.head_scores()
        if u > 1:
            tile.last()
        tile.store()
        nxt.head_finish()
        nxt.body()
        tile = nxt
    if tiles > 1:
        tile.last()
    tile.store()


def _attention_a(qt, k, vt, z, ctx_len):
    nb, t, _ = k.shape
    tq = TOKEN_TILE
    tk = ATTN_A_KEYS
    step = ATTN_A_TILES * tq
    return pl.pallas_call(
        functools.partial(_attn_a_kernel, tk=tk, ctx_len=ctx_len, tq=tq),
        grid=(nb, t // step),
        in_specs=[pl.BlockSpec((1, BRANCH_W, step), lambda b, j: (b, 0, j)),
                  pl.BlockSpec((1, t, KV_W), lambda b, j: (b, 0, 0)),
                  pl.BlockSpec((1, 2 * V_ROWS, t), lambda b, j: (b, 0, 0)),
                  pl.BlockSpec((1, step, BRANCH_W), lambda b, j: (b, j, 0))],
        out_specs=pl.BlockSpec((1, step, BRANCH_W), lambda b, j: (b, j, 0)),
        out_shape=jax.ShapeDtypeStruct((nb, t, BRANCH_W), BF16),
        scratch_shapes=[pltpu.VMEM((2, 2, tk, 2 * tq), F32),
                        pltpu.VMEM((2, 2, 1, 2 * tq), F32),
                        pltpu.VMEM((2, 2, 1, 2 * tq), F32),
                        pltpu.VMEM((2, 2, V_ROWS, 2 * tq), F32)],
        compiler_params=pltpu.CompilerParams(vmem_limit_bytes=VMEM_LIMIT),
        name="attention_a",
    )(qt, k, vt, z)


def _attn_d_kernel(sink_ref, qt_ref, k_ref, vt_ref, z_ref, o_ref, *, ctx_len, tq):
    tiles = qt_ref.shape[2] // tq
    t = k_ref.shape[1]
    span = tq + 2 * WINDOW
    first = lax.broadcasted_iota(jnp.int32, (1, 2 * tq), 1) < tq
    rel = (lax.broadcasted_iota(jnp.int32, (span, 2 * tq), 0)
           - lax.broadcasted_iota(jnp.int32, (span, 2 * tq), 1) % tq)

    def scores(u):
        tile = pl.program_id(1) * tiles + u
        q0 = tile * tq
        start = pl.multiple_of(jnp.clip(q0 - WINDOW, ctx_len, t - span), LANES)
        mask = (jnp.abs(rel + (start - q0)) <= WINDOW) & (tile > 0)
        sc = []
        for g in range(2):
            qt_g = _group_queries(qt_ref, g, slice(u * tq, (u + 1) * tq))
            sc.append((_dot(k_ref[0, :ctx_len, :], qt_g),
                       _dot(k_ref[0, pl.ds(start, span), :], qt_g)))
        return start, mask, sc

    def finish(u, start, mask, sc):
        for g in range(2):
            vrows = slice(g * V_ROWS, (g + 1) * V_ROWS)
            s_ctx = sc[g][0]
            s_loc = jnp.where(mask, sc[g][1], -jnp.inf)
            sink = jnp.where(first, sink_ref[2 * g], sink_ref[2 * g + 1]) * LOG2_E
            m = jnp.maximum(jnp.maximum(jnp.max(s_ctx, axis=0, keepdims=True),
                                        jnp.max(s_loc, axis=0, keepdims=True)), sink)
            acc = (_dot(vt_ref[0, vrows, :ctx_len], jnp.exp2(s_ctx - m).astype(BF16))
                   + _dot(vt_ref[0, vrows, pl.ds(start, span)], jnp.exp2(s_loc - m).astype(BF16)))
            l = acc[HEAD_DIM:HEAD_DIM + 1] + jnp.exp2(sink - m)
            _store_heads(o_ref, z_ref, g, acc[:HEAD_DIM] / l, slice(u * tq, (u + 1) * tq))

    pending = scores(0)
    for u in range(tiles):
        ready, pending = pending, (scores(u + 1) if u + 1 < tiles else None)
        finish(u, *ready)


def _attention_d(sink, qt, k, vt, z, ctx_len):
    nb, t, _ = k.shape
    tq = TOKEN_TILE
    step = ATTN_D_TILES * tq
    return pl.pallas_call(
        functools.partial(_attn_d_kernel, ctx_len=ctx_len, tq=tq),
        grid=(nb, t // step),
        in_specs=[pl.BlockSpec(memory_space=pltpu.SMEM),
                  pl.BlockSpec((1, BRANCH_W, step), lambda b, j: (b, 0, j)),
                  pl.BlockSpec((1, t, KV_W), lambda b, j: (b, 0, 0)),
                  pl.BlockSpec((1, 2 * V_ROWS, t), lambda b, j: (b, 0, 0)),
                  pl.BlockSpec((1, step, BRANCH_W), lambda b, j: (b, j, 0))],
        out_specs=pl.BlockSpec((1, step, BRANCH_W), lambda b, j: (b, j, 0)),
        out_shape=jax.ShapeDtypeStruct((nb, t, BRANCH_W), BF16),
        compiler_params=pltpu.CompilerParams(vmem_limit_bytes=VMEM_LIMIT),
        name="attention_d",
    )(sink, qt, k, vt, z)


def _dft_tables(seq_len, ctx_len):
    n2c = DFT_INNER
    n1c = seq_len // n2c
    scale = 1.0 / np.sqrt(float(seq_len) * GROUP_W)
    k1 = np.arange(n1c)[:, None]
    n1 = np.arange(n1c)[None, :]
    m1 = np.zeros((n2c, 2 * n1c, n1c), np.float64)
    for n2 in range(n2c):
        ang = -2.0 * np.pi * (k1 * n1 / n1c + n2 * k1 / seq_len)
        m1[n2, :n1c] = np.cos(ang) * scale
        m1[n2, n1c:] = np.sin(ang) * scale
    ang3 = 2.0 * np.pi * np.outer(np.arange(n2c), np.arange(n2c)) / n2c
    c3, s3 = np.cos(ang3), np.sin(ang3)
    m3 = np.block([[c3, s3], [-s3, c3]])
    cscale = 1.0 / np.sqrt(float(ctx_len) * GROUP_W)
    angc = 2.0 * np.pi * np.outer(np.arange(ctx_len), np.arange(ctx_len)) / ctx_len
    mc = np.concatenate([np.cos(angc), -np.sin(angc)], axis=0) * cscale
    angg = 2.0 * np.pi * np.outer(np.arange(GROUP_W), np.arange(GROUP_W)) / GROUP_W
    eye = np.eye(BRANCH_W // GROUP_W)
    chan = np.concatenate([np.kron(eye, np.cos(angg)), np.kron(eye, np.sin(angg))], axis=0)
    return _stacked(m1), _stacked(m3), _split(mc), _split(chan)


def _stacked(a):
    hi, lo = _split(a)
    return jnp.concatenate([hi, lo, hi], axis=-1)


def _dot3_stacked(m3, x):
    x_hi = x.astype(BF16)
    x_lo = (x - x_hi.astype(F32)).astype(BF16)
    return _dot(m3, jnp.concatenate([x_hi, x_hi, x_lo], axis=0))


def _split(a):
    a32 = jnp.asarray(a.astype(np.float32))
    hi = a32.astype(BF16)
    return hi, (a32 - hi.astype(F32)).astype(BF16)


def _dot3_left(m_hi, m_lo, x):
    x_hi = x.astype(BF16)
    x_lo = (x - x_hi.astype(F32)).astype(BF16)
    return _dot(m_hi, x_hi) + _dot(m_lo, x_hi) + _dot(m_hi, x_lo)


def _dot3_right(x, m_hi, m_lo):
    x_hi = x.astype(BF16)
    x_lo = (x - x_hi.astype(F32)).astype(BF16)
    return _dot(x_hi, m_hi) + _dot(x_hi, m_lo) + _dot(x_lo, m_hi)


def _fnet_dft_kernel(f_ref, m1_ref, fc_ref, m3_ref, mch_ref, mcl_ref,
                     pr_ref, pi_ref, prc_ref, pic_ref, br_ref, bi_ref, *, n_hi):
    i = pl.program_id(1)
    n2c = DFT_INNER
    sub = DFT_BLOCK
    halves = range(f_ref.shape[1])
    rows_b = br_ref.shape[2]
    n1c = rows_b // sub
    skip = f_ref.shape[3] - rows_b

    @pl.when(i < n_hi)
    def _():
        for q in range(sub):
            rows = pl.ds(q, n1c, stride=sub)
            x = jnp.concatenate([f_ref[0, h, 0, pl.ds(skip + q, n1c, stride=sub), :]
                                 for h in halves], axis=1)
            res = _dot3_stacked(m1_ref[i * sub + q], x)
            for h in halves:
                br_ref[h, i, rows, :] = res[:n1c, h * LANES:(h + 1) * LANES]
                bi_ref[h, i, rows, :] = res[n1c:, h * LANES:(h + 1) * LANES]

    @pl.when(i == n_hi)
    def _():
        ctx_len = fc_ref.shape[1]
        res = _dot3_left(mch_ref[...], mcl_ref[...], fc_ref[0])
        prc_ref[0] = res[:ctx_len]
        pic_ref[0] = res[ctx_len:]

    @pl.when(i >= n_hi)
    def _():
        for kb in range(pr_ref.shape[2]):
            base = pl.multiple_of(((i - n_hi) * pr_ref.shape[2] + kb) * (sub * sub), sub * sub)
            for q in range(sub):
                xin = jnp.concatenate(
                    [jnp.concatenate([ref[h, hi, pl.ds(base + q * sub, sub), :] for h in halves],
                                     axis=1)
                     for ref in (br_ref, bi_ref) for hi in range(n_hi)], axis=0)
                res = _dot3_stacked(m3_ref[...], xin)
                rows = pl.ds(q, n2c, stride=sub)
                for h in halves:
                    pr_ref[0, h, kb, rows, :] = res[:n2c, h * LANES:(h + 1) * LANES]
                    pi_ref[0, h, kb, rows, :] = res[n2c:, h * LANES:(h + 1) * LANES]


def _fnet_dft(fb, fb_ctx, m1, m3, mc):
    nb, nh, n_hi, rows_in, w = fb.shape
    ctx_len = fb_ctx.shape[1]
    n2c = DFT_INNER
    sub = DFT_BLOCK
    rows = rows_in - ctx_len // n_hi
    n1c = rows // sub
    n_kb = n1c // sub
    kb_step = DFT_OUT_BLOCKS
    first = lambda i: jnp.minimum(i, n_hi - 1)
    second = lambda i: jnp.maximum(i - n_hi, 0)
    const = lambda a: pl.BlockSpec(a.shape, lambda b, i: (0,) * a.ndim)
    cblk = pl.BlockSpec((1, ctx_len, nh * w), lambda b, i: (b, 0, 0))
    oblk = pl.BlockSpec((1, nh, kb_step, n2c * sub, w), lambda b, i: (b, 0, second(i), 0, 0))
    p4 = jax.ShapeDtypeStruct((nb, nh, n_kb, n2c * sub, w), F32)
    pc = jax.ShapeDtypeStruct((nb, ctx_len, nh * w), F32)
    return pl.pallas_call(
        functools.partial(_fnet_dft_kernel, n_hi=n_hi),
        grid=(nb, n_hi + n_kb // kb_step),
        in_specs=[pl.BlockSpec((1, nh, 1, rows_in, w), lambda b, i: (b, 0, first(i), 0, 0)),
                  const(m1),
                  cblk, const(m3), const(mc[0]), const(mc[1])],
        out_specs=[oblk, oblk, cblk, cblk],
        out_shape=[p4, p4, pc, pc],
        scratch_shapes=[pltpu.VMEM((nh, n_hi, rows, w), F32),
                        pltpu.VMEM((nh, n_hi, rows, w), F32)],
        compiler_params=pltpu.CompilerParams(vmem_limit_bytes=VMEM_LIMIT,
                                             dimension_semantics=("arbitrary", "arbitrary")),
        name="fnet_dft",
    )(fb, m1, fb_ctx, m3, *mc)


def _merge_kernel(xc_ref, xl_ref, mod_ref, nw_ref, *refs, n_batch, ctx_only, tiles):
    per_tile = lambda k: refs[k * tiles:(k + 1) * tiles]
    ya_refs, yd_refs, zb_refs, yc_refs = (per_tile(k) for k in range(4))
    (pr_ref, pi_ref, prc_ref, pic_ref, chh_ref, chl_ref, wf_ref, wbr_ref, wm_ref, bm_ref,
     wo_ref, o_ref) = refs[4 * tiles:]
    d = xl_ref.shape[-1]
    tm = o_ref.shape[1] // tiles
    row = n_batch if ctx_only else pl.program_id(0)
    sub = DFT_BLOCK

    for u in range(tiles):
        tok = slice(u * tm, (u + 1) * tm)
        x = xc_ref[0] if ctx_only else xl_ref[0, tok, :]
        h, gate = _modulated_norm(x, mod_ref, nw_ref, row, d)

        def latent_rows(ref):
            per = ref.shape[3] // tiles
            return jnp.concatenate(
                [jnp.concatenate([ref[0, hf, kb, u * per + k2 * sub:u * per + (k2 + 1) * sub, :]
                                  for k2 in range(per // sub)
                                  for kb in range(ref.shape[2])], axis=0)
                 for hf in range(ref.shape[1])], axis=1)

        if ctx_only:
            pc = jnp.concatenate([prc_ref[0], pic_ref[0]], axis=1)
        else:
            pc = jnp.concatenate([latent_rows(pr_ref), latent_rows(pi_ref)], axis=1)
        yb = (_dot(_dot3_right(pc, chh_ref[...], chl_ref[...]).astype(BF16), wf_ref[...])
              * zb_refs[u][0])
        branches = (ya_refs[u][0], yd_refs[u][0], yb.astype(BF16), yc_refs[u][0])
        hb = h.astype(BF16)
        acc = jnp.zeros((tm, d), F32)
        for r in range(N_BRANCH):
            g = jax.nn.sigmoid(_dot(hb, wm_ref[:, r * d:(r + 1) * d])
                               + bm_ref[:, r * d:(r + 1) * d])
            acc = acc + g * _dot(branches[r], wbr_ref[r])
        o_ref[0, tok, :] = x + gate * _dot(acc.astype(BF16), wo_ref[...])


def _merge(layer, xc, xl, mod, nw, ya, yd, pr, pi, prc, pic, zb, yc, chan, wf_bd, w_br, w_merge,
           b_merge, w_out, ctx_only):
    nb, seq, d = xl.shape
    tm = TOKEN_TILE
    tiles = 1 if ctx_only else MERGE_TILES
    off = 0 if ctx_only else xc.shape[1] // tm
    step = tiles * tm
    nt = 1 if ctx_only else seq // step
    tok_specs = lambda w: [pl.BlockSpec((1, tm, w), lambda b, j, u=u: (b, tiles * j + u + off, 0))
                           for u in range(tiles)]
    ctx_tile = pl.BlockSpec((1, tm, d), lambda b, j: (b, 0, 0))
    lat_tile = pl.BlockSpec((1, step, d), lambda b, j: (b, j, 0))
    lat_spec = lambda a: pl.BlockSpec((1, a.shape[1], a.shape[2], step // a.shape[2], a.shape[4]),
                                      lambda b, j: (b, 0, 0, j, 0))
    ctx_spec = lambda a: pl.BlockSpec((1,) + a.shape[1:], lambda b, j: (b, 0, 0))
    const = lambda a: pl.BlockSpec(a.shape, lambda b, j: (0,) * a.ndim)
    of_layer = lambda a: pl.BlockSpec((None,) + a.shape[1:],
                                      lambda b, j: (layer,) + (0,) * (a.ndim - 1))
    return pl.pallas_call(
        functools.partial(_merge_kernel, n_batch=nb, ctx_only=ctx_only, tiles=tiles),
        grid=(nb, nt),
        in_specs=[ctx_tile, lat_tile, of_layer(mod), const(nw)]
                 + tok_specs(BRANCH_W) * 4
                 + [lat_spec(pr), lat_spec(pi), ctx_spec(prc), ctx_spec(pic),
                    const(chan[0]), const(chan[1]), const(wf_bd), of_layer(w_br),
                    of_layer(w_merge), const(b_merge), of_layer(w_out)],
        out_specs=ctx_tile if ctx_only else lat_tile,
        out_shape=jax.ShapeDtypeStruct(xc.shape if ctx_only else xl.shape, F32),
        compiler_params=pltpu.CompilerParams(vmem_limit_bytes=VMEM_LIMIT),
        name="merge_context" if ctx_only else "merge",
    )(xc, xl, mod, nw, *([ya] * tiles), *([yd] * tiles), *([zb] * tiles), *([yc] * tiles),
      pr, pi, prc, pic, *chan, wf_bd, w_br, w_merge, b_merge, w_out)


def _rope_tables(seq_len, ctx_len):
    t = jnp.arange(seq_len, dtype=jnp.int32)
    r = (t // GRID_W).astype(F32)
    col = (t % GRID_W).astype(F32)
    nf = HEAD_DIM // 4
    inv = ROPE_BASE ** (-jnp.arange(nf, dtype=F32) / nf)
    ar = r[:, None] * inv[None, :]
    ac = col[:, None] * inv[None, :]
    cr, sr, cc, sc = jnp.cos(ar), jnp.sin(ar), jnp.cos(ac), jnp.sin(ac)
    cos_h = jnp.concatenate([cr, cr, cc, cc], axis=1)
    sin_h = jnp.concatenate([-sr, sr, -sc, sc], axis=1)
    cos_t = jnp.concatenate([jnp.ones((ctx_len, HEAD_DIM), F32), cos_h], axis=0)
    sin_t = jnp.concatenate([jnp.zeros((ctx_len, HEAD_DIM), F32), sin_h], axis=0)
    return jnp.tile(cos_t, (1, 2)), jnp.tile(sin_t, (1, 2))


def kernel(x, c, ctx, c_ctx, norm_w, w_ada, b_ada, w_in, qn_a, kn_a, qn_d, kn_d, sink_d,
           w_fnet, w_sp, b_sp, w_br, w_merge, b_merge, w_out):
    nb, seq, d = x.shape
    ctx_len = ctx.shape[1]
    depth = norm_w.shape[0]
    assert ctx_len == TOKEN_TILE and seq % (2 * ATTN_A_KEYS) == 0 and nb < 8
    assert (seq + ctx_len) % (ATTN_A_TILES * TOKEN_TILE) == 0
    assert (seq + ctx_len) % (ATTN_D_TILES * TOKEN_TILE) == 0
    assert (seq + ctx_len) % (PROJ_TILES * TOKEN_TILE) == 0

    cvecs = jnp.zeros((8, d), F32).at[:nb].set(c).at[nb].set(c_ctx)
    mods = _modulation(cvecs, w_ada, b_ada)

    cos_t, sin_t = _rope_tables(seq, ctx_len)
    m1, m3, mc, chan = _dft_tables(seq, ctx_len)
    pair = lambda w: jnp.tile(w, 2).reshape(1, LANES)
    eye_g = jnp.eye(BRANCH_W // GROUP_W, dtype=F32)

    w_in, w_sp, w_br, w_merge, w_out = (w.astype(BF16) for w in (w_in, w_sp, w_br, w_merge, w_out))
    xc, xl = ctx, x
    for l in range(depth):
        nw = norm_w[l].reshape(1, d)
        wf_bd = jnp.einsum('gh,gcd->gchd', eye_g, w_fnet[l]).reshape(BRANCH_W, BRANCH_W)
        b_sp_t = jnp.repeat(b_sp[l].T, GROUP_W, axis=1)
        (qat, ka, vat, za, qdt, kd, vdt, zd, fb, fbc, zb, yc) = _projection(
            l, xc, xl, mods, nw, w_in, cos_t, sin_t,
            pair(qn_a[l]), pair(kn_a[l]), pair(qn_d[l]), pair(kn_d[l]),
            w_sp, b_sp_t)
        ya = _attention_a(qat, ka, vat, za, ctx_len)
        yd = _attention_d(sink_d[l], qdt, kd, vdt, zd, ctx_len)
        pr, pi, prc, pic = _fnet_dft(fb, fbc, m1, m3, mc)
        merged = functools.partial(
            _merge, l, xc, xl, mods, nw, ya, yd, pr, pi, prc, pic, zb, yc, chan,
            wf_bd.astype(BF16), w_br, w_merge, b_merge[l].reshape(1, -1), w_out)
        if l < depth - 1:
            xc = merged(ctx_only=True)
        xl = merged(ctx_only=False)
    return xl
```
